```python
import jax, jax.numpy as jnp
from jax import lax
import numpy as np

D_MODEL = 4096
BATCH = 16
SEQ = 256
DEPTH = 4
DEC_BATCH = 2
DEC_SEQ = 1024
PAST_LEN = 512

GRID_W = 64
HEAD_DIM = 128
Q_BLOCK = 128
EPS = 1e-6
NEG = -1e30
NA_HEADS = 8
NA_WIN_ROWS = 8
NA_WIN_COLS = 16
NA_W = NA_HEADS * HEAD_DIM
GQA_HEADS = 8
GQA_KV_HEADS = 2
GQA_GROUP = GQA_HEADS // GQA_KV_HEADS
GQA_Q_W = GQA_HEADS * HEAD_DIM
GQA_KV_W = GQA_KV_HEADS * HEAD_DIM
GQA_QKV_W = GQA_Q_W + 2 * GQA_KV_W
ROPE_THETA = 10000.0
SSM_WIDTH = 1024
SSM_GROUP = 16
SSM_GROUPS = SSM_WIDTH // SSM_GROUP
SSM_STATE = 64
RWKV_WIDTH = 1024
RWKV_HEAD = 64
RWKV_HEADS = RWKV_WIDTH // RWKV_HEAD
DECAY_RANK = 64
ICLR_RANK = 64
GATE_RANK = 128
RWKV_IN_W = 3 * RWKV_WIDTH + DECAY_RANK + ICLR_RANK + GATE_RANK
RWKV_SPLITS = (RWKV_WIDTH, 2 * RWKV_WIDTH, 3 * RWKV_WIDTH, 3 * RWKV_WIDTH + DECAY_RANK,
               3 * RWKV_WIDTH + DECAY_RANK + ICLR_RANK)
RWKV_GN_EPS = 64e-5
N_BRANCH = 4
BRANCH_W = 1024
IN_OFFSETS = (3 * NA_W, 3 * NA_W + GQA_QKV_W, 3 * NA_W + GQA_QKV_W + SSM_WIDTH,
              3 * NA_W + GQA_QKV_W + SSM_WIDTH + RWKV_IN_W)
IN_W = IN_OFFSETS[-1] + N_BRANCH * D_MODEL
N_EXPERTS = 16
N_EXPERT_GROUPS = 4
EXPERTS_PER_GROUP = N_EXPERTS // N_EXPERT_GROUPS
EXPERT_TOPK = 2
EXPERT_FF = 1024
MOE_BLOCK = 128

kernel_name = 'hybrid_flow_trunk_step'


def rms_norm(x, g):
    xf = x.astype(jnp.float32)
    y = xf * lax.rsqrt(jnp.mean(xf * xf, axis=-1, keepdims=True) + EPS)
    return (y * g.astype(jnp.float32)).astype(x.dtype)


def axial_rope(x):
    n = x.shape[-2]
    t = jnp.arange(n)
    rows = (t // GRID_W).astype(jnp.float32)
    cols = (t % GRID_W).astype(jnp.float32)
    half = HEAD_DIM // 2
    inv = ROPE_THETA ** (-jnp.arange(0, half, 2, dtype=jnp.float32) / half)

    def rot(xa, pos):
        ang = pos[:, None] * inv[None, :]
        cos, sin = jnp.cos(ang), jnp.sin(ang)
        x1, x2 = xa[..., :half // 2], xa[..., half // 2:]
        return jnp.concatenate([x1 * cos - x2 * sin, x2 * cos + x1 * sin], axis=-1)

    xf = x.astype(jnp.float32)
    out = jnp.concatenate([rot(xf[..., :half], rows), rot(xf[..., half:], cols)], axis=-1)
    return out.astype(x.dtype)


def block_attention(q, k, v):
    b, hk, g, s, d = q.shape
    nb = s // Q_BLOCK
    qb = jnp.moveaxis(q.reshape(b, hk, g, nb, Q_BLOCK, d), 3, 0)
    scale = d ** -0.5

    def one(qi):
        sc = jnp.einsum('bkgqd,bkld->bkgql', qi, k).astype(jnp.float32) * scale
        p = jax.nn.softmax(sc, axis=-1).astype(v.dtype)
        return jnp.einsum('bkgql,bkld->bkgqd', p, v)

    o = lax.map(one, qb)
    return jnp.moveaxis(o, 0, 3).reshape(b, hk, g, s, d)


def neighbourhood_attention(q, k, v, ctx_k, ctx_v, rpb):
    b, h, s, d = q.shape
    rows = s // GRID_W
    wr = min(NA_WIN_ROWS, rows)
    wc = NA_WIN_COLS
    qb_w = wc
    kb_w = 2 * wc
    ncb = GRID_W // qb_w
    r = np.arange(rows)
    row_idx = np.clip(r - wr // 2, 0, rows - wr)[:, None] + np.arange(wr)[None, :]
    band_start = np.clip(np.arange(ncb) * qb_w - wc // 2, 0, GRID_W - kb_w)
    col_idx = band_start[:, None] + np.arange(kb_w)[None, :]
    qc = np.arange(GRID_W).reshape(ncb, qb_w)
    c_start = np.clip(qc - wc // 2, 0, GRID_W - wc)
    kc = col_idx[:, None, :]
    col_ok = (kc >= c_start[..., None]) & (kc < c_start[..., None] + wc)
    dr_i = row_idx - r[:, None] + NA_WIN_ROWS - 1
    dc_i = np.clip(kc - qc[..., None] + NA_WIN_COLS - 1, 0, 2 * NA_WIN_COLS - 2)
    bias = rpb[:, dr_i[:, None, None, :, None], dc_i[None, :, :, None, :]]
    bias = jnp.where(col_ok[:, :, None, :], bias.astype(jnp.float32), NEG)

    scale = d ** -0.5
    qg = q.reshape(b, h, rows, ncb, qb_w, d)
    kg = k.reshape(b, h, rows, GRID_W, d)
    vg = v.reshape(b, h, rows, GRID_W, d)
    gi_r = row_idx[:, None, :, None]
    gi_c = col_idx[None, :, None, :]
    kw = kg[:, :, gi_r, gi_c, :]
    vw = vg[:, :, gi_r, gi_c, :]
    s_loc = jnp.einsum('bhrjqd,bhrjwkd->bhrjqwk', qg, kw).astype(jnp.float32) * scale + bias[None]
    s_ctx = jnp.einsum('bhrjqd,bhld->bhrjql', qg, ctx_k).astype(jnp.float32) * scale
    n_loc = wr * kb_w
    p = jax.nn.softmax(jnp.concatenate([s_loc.reshape(b, h, rows, ncb, qb_w, n_loc), s_ctx], axis=-1), axis=-1)
    p = p.astype(v.dtype)
    p_loc = p[..., :n_loc].reshape(b, h, rows, ncb, qb_w, wr, kb_w)
    out = jnp.einsum('bhrjqwk,bhrjwkd->bhrjqd', p_loc, vw) + jnp.einsum('bhrjql,bhld->bhrjqd', p[..., n_loc:], ctx_v)
    return out.reshape(b, h, s, d)


def _complex_affine_combine(e1, e2):
    a1r, a1i, b1r, b1i = e1
    a2r, a2i, b2r, b2i = e2
    return (a2r * a1r - a2i * a1i, a2r * a1i + a2i * a1r,
            a2r * b1r - a2i * b1i + b2r, a2r * b1i + a2i * b1r + b2i)


def s5_scan(u, lam_re, lam_im, log_dt, b_re, b_im, c_re, c_im, h0_re, h0_im, reverse):
    f32 = jnp.float32
    lr, li = lam_re.astype(f32), lam_im.astype(f32)
    dt = jnp.exp(log_dt.astype(f32))[:, None]
    mag = jnp.exp(lr * dt)
    ab_re, ab_im = mag * jnp.cos(li * dt), mag * jnp.sin(li * dt)
    den = lr * lr + li * li
    nr, ni = ab_re - 1.0, ab_im
    co_re = (nr * lr + ni * li) / den
    co_im = (ni * lr - nr * li) / den
    br, bi = b_re.astype(f32), b_im.astype(f32)
    bb_re = co_re[..., None] * br - co_im[..., None] * bi
    bb_im = co_re[..., None] * bi + co_im[..., None] * br
    bu_re = jnp.einsum('bsgc,gpc->bsgp', u, bb_re)
    bu_im = jnp.einsum('bsgc,gpc->bsgp', u, bb_im)
    hr, hi = h0_re.astype(f32), h0_im.astype(f32)
    first = u.shape[1] - 1 if reverse else 0
    bu_re = bu_re.at[:, first].add(ab_re * hr - ab_im * hi)
    bu_im = bu_im.at[:, first].add(ab_re * hi + ab_im * hr)
    a_re = jnp.broadcast_to(ab_re, bu_re.shape)
    a_im = jnp.broadcast_to(ab_im, bu_im.shape)
    _, _, xr, xi = lax.associative_scan(_complex_affine_combine, (a_re, a_im, bu_re, bu_im), reverse=reverse, axis=1)
    y = jnp.einsum('bsgp,gcp->bsgc', xr, c_re.astype(f32)) - jnp.einsum('bsgp,gcp->bsgc', xi, c_im.astype(f32))
    last = 0 if reverse else u.shape[1] - 1
    return y, xr[:, last], xi[:, last]


def s5_mixer(u, lp, h0):
    b, s, _ = u.shape
    uf = u.astype(jnp.float32)
    ug = uf.reshape(b, s, SSM_GROUPS, SSM_GROUP)
    y = uf * lp['ssm_d']
    finals = []
    for d in range(2):
        yd, hr, hi = s5_scan(ug, lp['ssm_lam_re'][d], lp['ssm_lam_im'][d], lp['ssm_log_dt'][d],
                             lp['ssm_b_re'][d], lp['ssm_b_im'][d], lp['ssm_c_re'][d], lp['ssm_c_im'][d],
                             h0[:, d, 0], h0[:, d, 1], reverse=(d == 1))
        y = y + yd.reshape(b, s, SSM_WIDTH)
        finals.append(jnp.stack([hr, hi], axis=1))
    y = jax.nn.gelu(y)
    y = y * jax.nn.sigmoid(y @ lp['ssm_w_glu'].astype(jnp.float32))
    return y.astype(u.dtype), jnp.stack(finals, axis=1).astype(u.dtype)


def token_shift(z, mu):
    zp = jnp.pad(z, ((0, 0), (1, 1), (0, 0)))
    nbr = 0.5 * (zp[:, :-2] + zp[:, 2:])
    return z + mu * (nbr - z)


def rwkv_scan(r, decay, kk, kka, k, v, s0, reverse):
    def step(st, inp):
        r_t, w_t, kk_t, kka_t, k_t, v_t = inp
        sa = jnp.einsum('bhvk,bhk->bhv', st, kk_t)
        st = st * w_t[:, :, None, :] - sa[..., None] * kka_t[:, :, None, :] + v_t[..., None] * k_t[:, :, None, :]
        return st, jnp.einsum('bhvk,bhk->bhv', st, r_t)
    xs = tuple(jnp.moveaxis(a, 1, 0) for a in (r, decay, kk, kka, k, v))
    sf, ys = lax.scan(step, s0, xs, reverse=reverse)
    return jnp.moveaxis(ys, 0, 1), sf


def rwkv_mixer(z, lp, s0):
    b, s, _ = z.shape
    f32 = jnp.float32
    r, k, v, wl, al, gl = jnp.split(z.astype(f32), RWKV_SPLITS, axis=-1)

    def heads(t):
        return t.reshape(b, s, RWKV_HEADS, RWKV_HEAD)

    kk = heads(k * lp['rwkv_k_k'])
    kk = kk * lax.rsqrt(jnp.sum(kk * kk, axis=-1, keepdims=True) + 1e-12)
    g = jax.nn.sigmoid(gl) @ lp['rwkv_g2'].astype(f32)
    rh, vh = heads(r), heads(v)
    y_scan = jnp.zeros_like(rh)
    bonus = jnp.zeros_like(rh)
    finals = []
    for d in range(2):
        w_log = -jax.nn.softplus(-(lp['rwkv_w0'][d] + jnp.tanh(wl) @ lp['rwkv_w2'][d].astype(f32))) - 0.5
        decay = jnp.exp(-jnp.exp(w_log))
        a = jax.nn.sigmoid(lp['rwkv_a0'][d] + al @ lp['rwkv_a2'][d].astype(f32))
        kd = heads(k * (1.0 + (a - 1.0) * lp['rwkv_k_a']))
        yd, sf = rwkv_scan(rh, heads(decay), kk, kk * heads(a), kd, vh, s0[:, d].astype(f32), reverse=(d == 1))
        y_scan = y_scan + yd
        bonus = bonus + jnp.sum(rh * kd * lp['rwkv_r_k'], axis=-1, keepdims=True) * vh
        finals.append(sf)
    mu = jnp.mean(y_scan, axis=-1, keepdims=True)
    var = jnp.mean(jnp.square(y_scan - mu), axis=-1, keepdims=True)
    yn = ((y_scan - mu) * lax.rsqrt(var + RWKV_GN_EPS)).reshape(b, s, RWKV_WIDTH)
    y = yn * lp['rwkv_ln_g'] + lp['rwkv_ln_b'] + bonus.reshape(b, s, RWKV_WIDTH)
    return (y * g).astype(z.dtype), jnp.stack(finals, axis=1).astype(z.dtype)


def token_mixers(h, lp, cached):
    b, s, _ = h.shape
    proj = h @ lp['w_in']
    na_qkv, gqa_qkv, ssm_u, rwkv_in, gate_in = jnp.split(proj, IN_OFFSETS, axis=-1)

    def heads(t, n):
        return t.reshape(b, s, n, HEAD_DIM).transpose(0, 2, 1, 3)

    nq, nk, nv = [heads(t, NA_HEADS) for t in jnp.split(na_qkv, 3, axis=-1)]
    gq, gk, gv = jnp.split(gqa_qkv, (GQA_Q_W, GQA_Q_W + GQA_KV_W), axis=-1)
    gq = rms_norm(heads(gq, GQA_HEADS), lp['gqa_q_norm'])
    gk = rms_norm(heads(gk, GQA_KV_HEADS), lp['gqa_k_norm'])
    gv = heads(gv, GQA_KV_HEADS)
    if cached is None:
        y_na = block_attention(nq[:, :, None], nk, nv)[:, :, 0]
        y_gqa = block_attention(gq.reshape(b, GQA_KV_HEADS, GQA_GROUP, s, HEAD_DIM), gk, gv)
        ssm_h0 = jnp.zeros((b, 2, 2, SSM_GROUPS, SSM_STATE), jnp.float32)
        rwkv_s0 = jnp.zeros((b, 2, RWKV_HEADS, RWKV_HEAD, RWKV_HEAD), jnp.float32)
    else:
        c_nk, c_nv, c_gk, c_gv, ssm_h0, rwkv_s0 = cached
        y_na = neighbourhood_attention(nq, nk, nv, c_nk, c_nv, lp['na_rpb'])
        gk_all = jnp.concatenate([c_gk, axial_rope(gk)], axis=2)
        gv_all = jnp.concatenate([c_gv, gv], axis=2)
        y_gqa = block_attention(axial_rope(gq).reshape(b, GQA_KV_HEADS, GQA_GROUP, s, HEAD_DIM), gk_all, gv_all)
    y_ssm, ssm_state = s5_mixer(ssm_u, lp, ssm_h0)
    y_rwkv, rwkv_state = rwkv_mixer(token_shift(rwkv_in, lp['rwkv_mu']), lp, rwkv_s0)

    def flat(t):
        return t.transpose(0, 2, 1, 3).reshape(b, s, -1)

    ys = jnp.stack([flat(y_na), flat(y_gqa.reshape(b, GQA_HEADS, s, HEAD_DIM)), y_ssm, y_rwkv], axis=0)
    br = jnp.einsum('nbsc,ncd->bsnd', ys, lp['w_branch'])
    gates = jax.nn.sigmoid(gate_in.reshape(b, s, N_BRANCH, D_MODEL))
    out = jnp.sum(gates * br, axis=2) @ lp['w_out']
    ctx = (nk, nv, gk, gv, ssm_state, rwkv_state) if cached is None else None
    return out, ctx


def moe_ffn(h, lp):
    b, s, d = h.shape
    x = h.reshape(-1, d)
    t = x.shape[0]
    f32 = jnp.float32
    scores = jax.nn.sigmoid((x @ lp['w_router']).astype(f32))
    sel = scores + lp['router_bias'].astype(f32)
    grp_score = lax.top_k(sel.reshape(t, N_EXPERT_GROUPS, EXPERTS_PER_GROUP), 2)[0].sum(-1)
    best = jnp.argmax(grp_score, axis=-1)
    in_grp = jnp.arange(N_EXPERT_GROUPS)[None, :] == best[:, None]
    masked = jnp.where(jnp.repeat(in_grp, EXPERTS_PER_GROUP, axis=-1), sel, -jnp.inf)
    _, idx = lax.top_k(masked, EXPERT_TOPK)
    wts = jnp.take_along_axis(scores, idx, axis=-1)
    wts = wts / jnp.sum(wts, axis=-1, keepdims=True)
    n_assign = t * EXPERT_TOPK
    flat_e = idx.reshape(-1)
    flat_w = wts.reshape(-1)
    flat_tok = jnp.repeat(jnp.arange(t, dtype=jnp.int32), EXPERT_TOPK)
    order = jnp.argsort(flat_e)
    e_sorted = flat_e[order]
    counts = jnp.bincount(flat_e, length=N_EXPERTS)
    padded = ((counts + MOE_BLOCK - 1) // MOE_BLOCK) * MOE_BLOCK
    pad_end = jnp.cumsum(padded)
    pad_start = pad_end - padded
    start = jnp.cumsum(counts) - counts
    dest = pad_start[e_sorted] + jnp.arange(n_assign) - start[e_sorted]
    n_blocks = -(-n_assign // MOE_BLOCK) + N_EXPERTS
    n_rows = n_blocks * MOE_BLOCK
    row_tok = jnp.full((n_rows,), t, jnp.int32).at[dest].set(flat_tok[order])
    row_w = jnp.zeros((n_rows,), f32).at[dest].set(flat_w[order])
    block_e = jnp.minimum(jnp.searchsorted(pad_end, jnp.arange(n_blocks) * MOE_BLOCK, side='right'), N_EXPERTS - 1)
    xp = jnp.concatenate([x, jnp.zeros((1, d), x.dtype)], axis=0)
    xb = xp[row_tok].reshape(n_blocks, MOE_BLOCK, d)
    w1, w3, w2 = lp['w_exp_gate'], lp['w_exp_up'], lp['w_exp_down']

    def expert_block(args):
        xi, e = args
        return (jax.nn.silu(xi @ w1[e]) * (xi @ w3[e])) @ w2[e]

    yb = lax.map(expert_block, (xb, block_e)).reshape(n_rows, d)
    y = jax.ops.segment_sum(yb * row_w[:, None].astype(yb.dtype), row_tok, num_segments=t + 1)[:t]
    return y.reshape(b, s, d)


def trunk_layer(x, cond, lp, cached):
    sh1, sc1, g1, sh2, sc2, g2 = jnp.split(jax.nn.silu(cond) @ lp['w_ada'] + lp['b_ada'], 6, axis=-1)
    h = rms_norm(x, lp['norm_mix']) * (1.0 + sc1) + sh1
    mix, ctx = token_mixers(h, lp, cached)
    x = x + g1 * mix
    h = rms_norm(x, lp['norm_ffn']) * (1.0 + sc2) + sh2
    x = x + g2 * moe_ffn(h, lp)
    return x, ctx


def setup_inputs(seed: int = 0) -> dict:
    key = jax.random.key(seed)
    ks = iter(jax.random.split(key, 64))
    f32 = jnp.float32
    L, D = DEPTH, D_MODEL
    G, P = SSM_GROUPS, SSM_STATE

    def nrm(shape, scale=1.0):
        return jax.random.normal(next(ks), shape, f32) * scale

    def unif(shape, lo, hi):
        return jax.random.uniform(next(ks), shape, f32, lo, hi)

    def gain(shape):
        return 1.0 + nrm(shape, 0.01)

    lam_im0 = jnp.pi * jnp.arange(P, dtype=f32)
    return {
        'x_prompt': nrm((BATCH, SEQ, D)),
        'x_sample': nrm((DEC_BATCH, DEC_SEQ, D)),
        'cache_na_k': nrm((DEC_BATCH, L, NA_HEADS, PAST_LEN, HEAD_DIM)),
        'cache_na_v': nrm((DEC_BATCH, L, NA_HEADS, PAST_LEN, HEAD_DIM)),
        'cache_gqa_k': nrm((DEC_BATCH, L, GQA_KV_HEADS, PAST_LEN, HEAD_DIM)),
        'cache_gqa_v': nrm((DEC_BATCH, L, GQA_KV_HEADS, PAST_LEN, HEAD_DIM)),
        'state_ssm': nrm((DEC_BATCH, L, 2, 2, G, P)),
        'state_rwkv': nrm((DEC_BATCH, L, 2, RWKV_HEADS, RWKV_HEAD, RWKV_HEAD), 0.5),
        'c': nrm((DEC_BATCH, D)),
        'c_ctx': nrm((D,)),
        'w_ada': nrm((L, D, 6 * D), 0.5 * D ** -0.5),
        'b_ada': nrm((L, 6 * D), 0.01),
        'norm_mix': gain((L, D)),
        'norm_ffn': gain((L, D)),
        'w_in': nrm((L, D, IN_W), D ** -0.5),
        'na_rpb': nrm((L, NA_HEADS, 2 * NA_WIN_ROWS - 1, 2 * NA_WIN_COLS - 1), 0.1),
        'gqa_q_norm': gain((L, HEAD_DIM)),
        'gqa_k_norm': gain((L, HEAD_DIM)),
        'ssm_lam_re': -0.5 + nrm((L, 2, G, P), 0.01),
        'ssm_lam_im': lam_im0 + nrm((L, 2, G, P), 0.01),
        'ssm_log_dt': unif((L, 2, G), float(np.log(1e-3)), float(np.log(1e-1))),
        'ssm_b_re': nrm((L, 2, G, P, SSM_GROUP), (2 * SSM_GROUP) ** -0.5),
        'ssm_b_im': nrm((L, 2, G, P, SSM_GROUP), (2 * SSM_GROUP) ** -0.5),
        'ssm_c_re': nrm((L, 2, G, SSM_GROUP, P), (2 * P) ** -0.5),
        'ssm_c_im': nrm((L, 2, G, SSM_GROUP, P), (2 * P) ** -0.5),
        'ssm_d': nrm((L, SSM_WIDTH)),
        'ssm_w_glu': nrm((L, SSM_WIDTH, SSM_WIDTH), SSM_WIDTH ** -0.5),
        'rwkv_mu': unif((L, RWKV_IN_W), 0.0, 1.0),
        'rwkv_w0': -0.5 + nrm((L, 2, RWKV_WIDTH), 0.5),
        'rwkv_w2': nrm((L, 2, DECAY_RANK, RWKV_WIDTH), 0.1 * DECAY_RANK ** -0.5),
        'rwkv_a0': nrm((L, 2, RWKV_WIDTH), 0.5),
        'rwkv_a2': nrm((L, 2, ICLR_RANK, RWKV_WIDTH), 0.1 * ICLR_RANK ** -0.5),
        'rwkv_k_k': 0.85 + nrm((L, RWKV_WIDTH), 0.1),
        'rwkv_k_a': 1.0 + nrm((L, RWKV_WIDTH), 0.1),
        'rwkv_r_k': nrm((L, RWKV_HEADS, RWKV_HEAD), 0.1),
        'rwkv_g2': nrm((L, GATE_RANK, RWKV_WIDTH), GATE_RANK ** -0.5),
        'rwkv_ln_g': gain((L, RWKV_WIDTH)),
        'rwkv_ln_b': nrm((L, RWKV_WIDTH), 0.01),
        'w_branch': nrm((L, N_BRANCH, BRANCH_W, D), BRANCH_W ** -0.5),
        'w_out': nrm((L, D, D), D ** -0.5),
        'w_router': nrm((D, N_EXPERTS), D ** -0.5),
        'router_bias': nrm((N_EXPERTS,), 0.01),
        'w_exp_gate': nrm((L, N_EXPERTS, D, EXPERT_FF), D ** -0.5),
        'w_exp_up': nrm((L, N_EXPERTS, D, EXPERT_FF), D ** -0.5),
        'w_exp_down': nrm((L, N_EXPERTS, EXPERT_FF, D), EXPERT_FF ** -0.5),
        'norm_final': gain((D,)),
    }


def reference(x_prompt, x_sample, cache_na_k, cache_na_v, cache_gqa_k, cache_gqa_v, state_ssm, state_rwkv,
              c, c_ctx, w_ada, b_ada, norm_mix, norm_ffn, w_in, na_rpb, gqa_q_norm, gqa_k_norm,
              ssm_lam_re, ssm_lam_im, ssm_log_dt, ssm_b_re, ssm_b_im, ssm_c_re, ssm_c_im, ssm_d, ssm_w_glu,
              rwkv_mu, rwkv_w0, rwkv_w2, rwkv_a0, rwkv_a2, rwkv_k_k, rwkv_k_a, rwkv_r_k, rwkv_g2,
              rwkv_ln_g, rwkv_ln_b, w_branch, w_out, w_router, router_bias, w_exp_gate, w_exp_up,
              w_exp_down, norm_final):
    xp, xs = x_prompt, x_sample
    cond_ctx = c_ctx[None, None, :]
    cond_lat = c[:, None, :]
    ctx_lists = [[] for _ in range(6)]
    for l in range(DEPTH):
        lp = dict(
            w_ada=w_ada[l], b_ada=b_ada[l], norm_mix=norm_mix[l], norm_ffn=norm_ffn[l], w_in=w_in[l],
            na_rpb=na_rpb[l], gqa_q_norm=gqa_q_norm[l], gqa_k_norm=gqa_k_norm[l],
            ssm_lam_re=ssm_lam_re[l], ssm_lam_im=ssm_lam_im[l], ssm_log_dt=ssm_log_dt[l],
            ssm_b_re=ssm_b_re[l], ssm_b_im=ssm_b_im[l], ssm_c_re=ssm_c_re[l], ssm_c_im=ssm_c_im[l],
            ssm_d=ssm_d[l], ssm_w_glu=ssm_w_glu[l],
            rwkv_mu=rwkv_mu[l], rwkv_w0=rwkv_w0[l], rwkv_w2=rwkv_w2[l], rwkv_a0=rwkv_a0[l], rwkv_a2=rwkv_a2[l],
            rwkv_k_k=rwkv_k_k[l], rwkv_k_a=rwkv_k_a[l], rwkv_r_k=rwkv_r_k[l], rwkv_g2=rwkv_g2[l],
            rwkv_ln_g=rwkv_ln_g[l], rwkv_ln_b=rwkv_ln_b[l],
            w_branch=w_branch[l], w_out=w_out[l], w_router=w_router, router_bias=router_bias,
            w_exp_gate=w_exp_gate[l], w_exp_up=w_exp_up[l], w_exp_down=w_exp_down[l])
        xp, ctx_t = trunk_layer(xp, cond_ctx, lp, None)
        for lst, tns in zip(ctx_lists, ctx_t):
            lst.append(tns)
        cached = (cache_na_k[:, l], cache_na_v[:, l], cache_gqa_k[:, l], cache_gqa_v[:, l],
                  state_ssm[:, l], state_rwkv[:, l])
        xs, _ = trunk_layer(xs, cond_lat, lp, cached)
    y_prompt = rms_norm(xp, norm_final)
    y_sample = rms_norm(xs, norm_final)
    new_na_k = jnp.stack(ctx_lists[0], axis=1)
    new_na_v = jnp.stack(ctx_lists[1], axis=1)
    new_gqa_k = jnp.stack(ctx_lists[2], axis=1)
    new_gqa_v = jnp.stack(ctx_lists[3], axis=1)
    new_ssm = jnp.stack(ctx_lists[4], axis=1)
    new_rwkv = jnp.stack(ctx_lists[5], axis=1)
    return (y_prompt, y_sample, new_na_k, new_na_v, new_gqa_k, new_gqa_v, new_ssm, new_rwkv)
```

```python
import functools

import numpy as np
import jax
import jax.numpy as jnp
from jax import lax
from jax.experimental import pallas as pl
from jax.experimental.pallas import tpu as pltpu

F32 = jnp.float32
BF16 = jnp.bfloat16

D_MODEL = 4096
BATCH = 16
SEQ = 256
DEPTH = 4
DEC_BATCH = 2
DEC_SEQ = 1024
PAST_LEN = 512
GRID_W = 64
HEAD_DIM = 128
EPS = 1e-6
NEG = -1e30
NA_HEADS = 8
NA_WIN_ROWS = 8
NA_WIN_COLS = 16
NA_W = NA_HEADS * HEAD_DIM
GQA_HEADS = 8
GQA_KV_HEADS = 2
GQA_GROUP = GQA_HEADS // GQA_KV_HEADS
GQA_Q_W = GQA_HEADS * HEAD_DIM
GQA_KV_W = GQA_KV_HEADS * HEAD_DIM
GQA_QKV_W = GQA_Q_W + 2 * GQA_KV_W
ROPE_THETA = 10000.0
SSM_WIDTH = 1024
SSM_GROUP = 16
SSM_GROUPS = SSM_WIDTH // SSM_GROUP
SSM_STATE = 64
RWKV_WIDTH = 1024
RWKV_HEAD = 64
RWKV_HEADS = RWKV_WIDTH // RWKV_HEAD
DECAY_RANK = 64
ICLR_RANK = 64
GATE_RANK = 128
RWKV_IN_W = 3 * RWKV_WIDTH + DECAY_RANK + ICLR_RANK + GATE_RANK
RWKV_GN_EPS = 64e-5
N_BRANCH = 4
OFF_NA = 0
OFF_GQA = 3 * NA_W
OFF_SSM = OFF_GQA + GQA_QKV_W
OFF_RWKV = OFF_SSM + SSM_WIDTH
OFF_GATE = OFF_RWKV + RWKV_IN_W
IN_W = OFF_GATE + N_BRANCH * D_MODEL
N_EXPERTS = 16
N_EXPERT_GROUPS = 4
EXPERTS_PER_GROUP = N_EXPERTS // N_EXPERT_GROUPS
EXPERT_TOPK = 2
EXPERT_FF = 1024

T_PROMPT = BATCH * SEQ
T_SAMPLE = DEC_BATCH * DEC_SEQ
T_ALL = T_PROMPT + T_SAMPLE
N_COND = 1 + DEC_BATCH
COND_PAD = 8
ATT_SCALE = HEAD_DIM ** -0.5

VMEM_LIMIT = 56 * 1024 * 1024


def _params(sem):
    return pltpu.CompilerParams(dimension_semantics=sem, vmem_limit_bytes=VMEM_LIMIT)


def _cond_group(i, rows_per_block):
    n_prompt_blocks = T_PROMPT // rows_per_block
    blocks_per_seq = DEC_SEQ // rows_per_block
    return jnp.where(i < n_prompt_blocks, 0, 1 + (i - n_prompt_blocks) // blocks_per_seq)


def _dot(a, b):
    return jnp.dot(a, b, preferred_element_type=F32)


def _dot_nt(a, b):
    return lax.dot_general(a, b, (((1,), (1,)), ((), ())), preferred_element_type=F32)


def _dot_tn(a, b):
    return lax.dot_general(a, b, (((0,), (0,)), ((), ())), preferred_element_type=F32)


def _split3(x):
    hi = x.astype(BF16)
    r1 = x - hi.astype(F32)
    mid = r1.astype(BF16)
    lo = (r1 - mid.astype(F32)).astype(BF16)
    return hi, mid, lo


def _dot_exact_rhs(a_bf16, x):
    hi, mid, lo = _split3(x)
    return _dot(a_bf16, hi) + _dot(a_bf16, mid) + _dot(a_bf16, lo)


def _dot_exact_lhs(x, b_bf16):
    hi, mid, lo = _split3(x)
    return _dot(hi, b_bf16) + _dot(mid, b_bf16) + _dot(lo, b_bf16)


def _rms(x, g):
    return x * lax.rsqrt(jnp.mean(x * x, axis=-1, keepdims=True) + EPS) * g


def _softmax_rows(s):
    m = jnp.max(s, axis=-1, keepdims=True)
    e = jnp.exp(s - m)
    return e / jnp.sum(e, axis=-1, keepdims=True)


def _ada_kernel(c_ref, w_ref, b_ref, o_ref):
    c = c_ref[...]
    s = (c * jax.nn.sigmoid(c)).astype(BF16)
    o_ref[0] = _dot(s, w_ref[0].astype(BF16)) + b_ref[0]


def _ada_all(cond, w_ada, b_ada):
    tn = 512
    n6 = 6 * D_MODEL
    return pl.pallas_call(
        _ada_kernel,
        grid=(DEPTH, n6 // tn),
        in_specs=[
            pl.BlockSpec((COND_PAD, D_MODEL), lambda l, j: (0, 0)),
            pl.BlockSpec((1, D_MODEL, tn), lambda l, j: (l, 0, j)),
            pl.BlockSpec((1, 1, tn), lambda l, j: (l, 0, j)),
        ],
        out_specs=pl.BlockSpec((1, COND_PAD, tn), lambda l, j: (l, 0, j)),
        out_shape=jax.ShapeDtypeStruct((DEPTH, COND_PAD, n6), F32),
        compiler_params=_params(("parallel", "parallel")),
        name="ada",
    )(cond, w_ada, b_ada.reshape(DEPTH, 1, n6))


NORM_ROWS = 256


def _norm_mod_kernel(x_ref, g_ref, mod_ref, o_ref, *, shift_idx):
    y = _rms(x_ref[...], g_ref[...])
    scale = mod_ref[0, shift_idx + 1:shift_idx + 2, :]
    shift = mod_ref[0, shift_idx:shift_idx + 1, :]
    o_ref[...] = (y * (1.0 + scale) + shift).astype(o_ref.dtype)


def _norm_mod(x, g, mod, shift_idx, out_dtype=BF16):
    t = x.shape[0]
    return pl.pallas_call(
        functools.partial(_norm_mod_kernel, shift_idx=shift_idx),
        grid=(t // NORM_ROWS,),
        in_specs=[
            pl.BlockSpec((NORM_ROWS, D_MODEL), lambda i: (i, 0)),
            pl.BlockSpec((1, D_MODEL), lambda i: (0, 0)),
            pl.BlockSpec((1, 6, D_MODEL), lambda i: (_cond_group(i, NORM_ROWS), 0, 0)),
        ],
        out_specs=pl.BlockSpec((NORM_ROWS, D_MODEL), lambda i: (i, 0)),
        out_shape=jax.ShapeDtypeStruct((t, D_MODEL), out_dtype),
        compiler_params=_params(("parallel",)),
        name="norm_mod",
    )(x, g.reshape(1, D_MODEL), mod)


def _final_norm_kernel(x_ref, g_ref, o_ref):
    o_ref[...] = _rms(x_ref[...], g_ref[...])


def _final_norm(x, g):
    t = x.shape[0]
    return pl.pallas_call(
        _final_norm_kernel,
        grid=(t // NORM_ROWS,),
        in_specs=[pl.BlockSpec((NORM_ROWS, D_MODEL), lambda i: (i, 0)),
                  pl.BlockSpec((1, D_MODEL), lambda i: (0, 0))],
        out_specs=pl.BlockSpec((NORM_ROWS, D_MODEL), lambda i: (i, 0)),
        out_shape=jax.ShapeDtypeStruct((t, D_MODEL), F32),
        compiler_params=_params(("parallel",)),
        name="final_norm",
    )(x, g.reshape(1, D_MODEL))


MM_TM = 1024
MM_TN = 256


def _proj_kernel(x_ref, w_ref, o_ref):
    o_ref[...] = _dot(x_ref[...], w_ref[0].astype(BF16))


def _in_proj(h, w_in, layer, col0, n_cols):
    t = h.shape[0]
    blk0 = col0 // MM_TN
    return pl.pallas_call(
        _proj_kernel,
        grid=(t // MM_TM, n_cols // MM_TN),
        in_specs=[
            pl.BlockSpec((MM_TM, D_MODEL), lambda i, j: (i, 0)),
            pl.BlockSpec((1, D_MODEL, MM_TN), lambda i, j: (layer, 0, blk0 + j)),
        ],
        out_specs=pl.BlockSpec((MM_TM, MM_TN), lambda i, j: (i, j)),
        out_shape=jax.ShapeDtypeStruct((t, n_cols), F32),
        compiler_params=_params(("parallel", "parallel")),
        name="in_proj",
    )(h, w_in)


def _merge_kernel(y0, y1, y2, y3, wb_ref, g0, g1, g2, g3, o_ref):
    acc = None
    for n, (y, g) in enumerate(((y0, g0), (y1, g1), (y2, g2), (y3, g3))):
        br = _dot(y[...], wb_ref[0, n].astype(BF16))
        term = jax.nn.sigmoid(g[...]) * br
        acc = term if acc is None else acc + term
    o_ref[...] = acc.astype(BF16)


def _branch_merge(ys, proj, w_branch, layer):
    t = proj.shape[0]
    bw = ys[0].shape[1]
    per_branch = D_MODEL // MM_TN
    y_spec = pl.BlockSpec((MM_TM, bw), lambda i, j: (i, 0))

    def gate_spec(n):
        return pl.BlockSpec((MM_TM, MM_TN), lambda i, j: (i, n * per_branch + j))

    return pl.pallas_call(
        _merge_kernel,
        grid=(t // MM_TM, D_MODEL // MM_TN),
        in_specs=[y_spec] * 4
        + [pl.BlockSpec((1, N_BRANCH, bw, MM_TN), lambda i, j: (layer, 0, 0, j))]
        + [gate_spec(n) for n in range(N_BRANCH)],
        out_specs=pl.BlockSpec((MM_TM, MM_TN), lambda i, j: (i, j)),
        out_shape=jax.ShapeDtypeStruct((t, D_MODEL), BF16),
        compiler_params=_params(("parallel", "parallel")),
        name="branch_merge",
    )(*ys, w_branch, proj, proj, proj, proj)


def _out_proj_kernel(m_ref, w_ref, x_ref, mod_ref, o_ref, *, gate_idx):
    out = _dot(m_ref[...], w_ref[0].astype(BF16))
    o_ref[...] = x_ref[...] + mod_ref[0, gate_idx:gate_idx + 1, :] * out


def _out_proj_residual(merged, w_out, layer, x, mod, gate_idx):
    t = x.shape[0]
    return pl.pallas_call(
        functools.partial(_out_proj_kernel, gate_idx=gate_idx),
        grid=(t // MM_TM, D_MODEL // MM_TN),
        in_specs=[
            pl.BlockSpec((MM_TM, D_MODEL), lambda i, j: (i, 0)),
            pl.BlockSpec((1, D_MODEL, MM_TN), lambda i, j: (layer, 0, j)),
            pl.BlockSpec((MM_TM, MM_TN), lambda i, j: (i, j)),
            pl.BlockSpec((1, 6, MM_TN), lambda i, j: (_cond_group(i, MM_TM), 0, j)),
        ],
        out_specs=pl.BlockSpec((MM_TM, MM_TN), lambda i, j: (i, j)),
        out_shape=jax.ShapeDtypeStruct((t, D_MODEL), F32),
        compiler_params=_params(("parallel", "parallel")),
        name="out_proj",
    )(merged, w_out, x, mod)


def _attend(q, k, v):
    p = _softmax_rows(_dot_nt(q, k) * ATT_SCALE)
    return _dot(p.astype(BF16), v)


def _ctx_na_kernel(q_ref, k_ref, v_ref, y_ref, ko_ref, vo_ref):
    for h in range(NA_HEADS):
        sl = slice(HEAD_DIM * h, HEAD_DIM * (h + 1))
        kf = k_ref[:, sl]
        vf = v_ref[:, sl]
        y = _attend(q_ref[:, sl].astype(BF16), kf.astype(BF16), vf.astype(BF16))
        y_ref[:, sl] = y.astype(BF16)
        ko_ref[0, h] = kf
        vo_ref[0, h] = vf


def _ctx_na(proj):
    cache = jax.ShapeDtypeStruct((BATCH, NA_HEADS, SEQ, HEAD_DIM), F32)
    cache_spec = pl.BlockSpec((1, NA_HEADS, SEQ, HEAD_DIM), lambda b: (b, 0, 0, 0))
    return pl.pallas_call(
        _ctx_na_kernel,
        grid=(BATCH,),
        in_specs=[pl.BlockSpec((SEQ, NA_W), lambda b, c=c: (b, c)) for c in range(3)],
        out_specs=[pl.BlockSpec((SEQ, NA_W), lambda b: (b, 0)), cache_spec, cache_spec],
        out_shape=[jax.ShapeDtypeStruct((T_PROMPT, NA_W), BF16), cache, cache],
        compiler_params=_params(("parallel",)),
        name="ctx_na",
    )(proj, proj, proj)


def _ctx_gqa_kernel(q_ref, k_ref, v_ref, gq_ref, gk_ref, y_ref, ko_ref, vo_ref):
    for kv in range(GQA_KV_HEADS):
        sl = slice(HEAD_DIM * kv, HEAD_DIM * (kv + 1))
        kn = _rms(k_ref[:, sl], gk_ref[...])
        vf = v_ref[:, sl]
        ko_ref[0, kv] = kn
        vo_ref[0, kv] = vf
        kb = kn.astype(BF16)
        vb = vf.astype(BF16)
        for g in range(GQA_GROUP):
            h = kv * GQA_GROUP + g
            hs = slice(HEAD_DIM * h, HEAD_DIM * (h + 1))
            qn = _rms(q_ref[:, hs], gq_ref[...])
            y_ref[:, hs] = _attend(qn.astype(BF16), kb, vb).astype(BF16)


def _ctx_gqa(proj, gq, gk):
    cache = jax.ShapeDtypeStruct((BATCH, GQA_KV_HEADS, SEQ, HEAD_DIM), F32)
    cache_spec = pl.BlockSpec((1, GQA_KV_HEADS, SEQ, HEAD_DIM), lambda b: (b, 0, 0, 0))
    norm_spec = pl.BlockSpec((1, HEAD_DIM), lambda b: (0, 0))
    return pl.pallas_call(
        _ctx_gqa_kernel,
        grid=(BATCH,),
        in_specs=[
            pl.BlockSpec((SEQ, GQA_Q_W), lambda b: (b, OFF_GQA // GQA_Q_W)),
            pl.BlockSpec((SEQ, GQA_KV_W), lambda b: (b, (OFF_GQA + GQA_Q_W) // GQA_KV_W)),
            pl.BlockSpec((SEQ, GQA_KV_W), lambda b: (b, (OFF_GQA + GQA_Q_W) // GQA_KV_W + 1)),
            norm_spec, norm_spec,
        ],
        out_specs=[pl.BlockSpec((SEQ, GQA_Q_W), lambda b: (b, 0)), cache_spec, cache_spec],
        out_shape=[jax.ShapeDtypeStruct((T_PROMPT, GQA_Q_W), BF16), cache, cache],
        compiler_params=_params(("parallel",)),
        name="ctx_gqa",
    )(proj, proj, proj, gq.reshape(1, HEAD_DIM), gk.reshape(1, HEAD_DIM))


LAT_Q_ROWS = 256
LAT_ROW0 = T_PROMPT // DEC_SEQ


def _rope_tables():
    t = np.arange(DEC_SEQ)
    half = HEAD_DIM // 2
    inv = jnp.asarray(ROPE_THETA, F32) ** (-jnp.arange(0, half, 2, dtype=F32) / half)
    rows = jnp.asarray(t // GRID_W, F32)
    cols = jnp.asarray(t % GRID_W, F32)
    ang_r = rows[:, None] * inv[None, :]
    ang_c = cols[:, None] * inv[None, :]
    cos = jnp.concatenate([jnp.cos(ang_r)] * 2 + [jnp.cos(ang_c)] * 2, axis=-1)
    sin = jnp.concatenate([-jnp.sin(ang_r), jnp.sin(ang_r), -jnp.sin(ang_c), jnp.sin(ang_c)], axis=-1)
    return cos, sin


def _rope(x, cos, sin):
    q = HEAD_DIM // 4
    lane = lax.broadcasted_iota(jnp.int32, x.shape, 1)
    partner = jnp.where((lane % (2 * q)) < q, pltpu.roll(x, HEAD_DIM - q, 1), pltpu.roll(x, q, 1))
    return x * cos + partner * sin


def _lat_gqa_kernel(q_ref, k_ref, v_ref, ck_ref, cv_ref, gq_ref, gk_ref, cos_ref, sin_ref, y_ref, kall, vall):
    cos = cos_ref[...]
    sin = sin_ref[...]
    kall[0:PAST_LEN, :] = ck_ref[0, 0, 0].astype(BF16)
    vall[0:PAST_LEN, :] = cv_ref[0, 0, 0].astype(BF16)
    kall[PAST_LEN:, :] = _rope(_rms(k_ref[...], gk_ref[...]), cos, sin).astype(BF16)
    vall[PAST_LEN:, :] = v_ref[...].astype(BF16)
    for g in range(GQA_GROUP):
        hs = slice(HEAD_DIM * g, HEAD_DIM * (g + 1))
        for qb in range(DEC_SEQ // LAT_Q_ROWS):
            rs = slice(LAT_Q_ROWS * qb, LAT_Q_ROWS * (qb + 1))
            qn = _rope(_rms(q_ref[rs, hs], gq_ref[...]), cos[rs], sin[rs])
            y_ref[rs, hs] = _attend(qn.astype(BF16), kall[...], vall[...]).astype(BF16)


def _lat_gqa(proj, cache_k, cache_v, layer, gq, gk, cos, sin):
    group_w = GQA_GROUP * HEAD_DIM
    kcol = (OFF_GQA + GQA_Q_W) // HEAD_DIM
    cache_spec = pl.BlockSpec((1, 1, 1, PAST_LEN, HEAD_DIM), lambda b, kv: (b, layer, kv, 0, 0))
    norm_spec = pl.BlockSpec((1, HEAD_DIM), lambda b, kv: (0, 0))
    tab_spec = pl.BlockSpec((DEC_SEQ, HEAD_DIM), lambda b, kv: (0, 0))
    return pl.pallas_call(
        _lat_gqa_kernel,
        grid=(DEC_BATCH, GQA_KV_HEADS),
        in_specs=[
            pl.BlockSpec((DEC_SEQ, group_w), lambda b, kv: (LAT_ROW0 + b, OFF_GQA // group_w + kv)),
            pl.BlockSpec((DEC_SEQ, HEAD_DIM), lambda b, kv: (LAT_ROW0 + b, kcol + kv)),
            pl.BlockSpec((DEC_SEQ, HEAD_DIM), lambda b, kv: (LAT_ROW0 + b, kcol + GQA_KV_HEADS + kv)),
            cache_spec, cache_spec, norm_spec, norm_spec, tab_spec, tab_spec,
        ],
        out_specs=pl.BlockSpec((DEC_SEQ, group_w), lambda b, kv: (b, kv)),
        out_shape=jax.ShapeDtypeStruct((T_SAMPLE, GQA_Q_W), BF16),
        scratch_shapes=[pltpu.VMEM((PAST_LEN + DEC_SEQ, HEAD_DIM), BF16)] * 2,
        compiler_params=_params(("parallel", "parallel")),
        name="lat_gqa",
    )(proj, proj, proj, cache_k, cache_v, gq.reshape(1, HEAD_DIM), gk.reshape(1, HEAD_DIM), cos, sin)


def _na_bias(rpb):
    rows = DEC_SEQ // GRID_W
    wr = min(NA_WIN_ROWS, rows)
    t = np.arange(DEC_SEQ)
    r, c = t // GRID_W, t % GRID_W
    r0 = np.clip(r - wr // 2, 0, rows - wr)
    c0 = np.clip(c - NA_WIN_COLS // 2, 0, GRID_W - NA_WIN_COLS)
    ok = ((r[None, :] >= r0[:, None]) & (r[None, :] < r0[:, None] + wr)
          & (c[None, :] >= c0[:, None]) & (c[None, :] < c0[:, None] + NA_WIN_COLS))
    dr = np.clip(r[None, :] - r[:, None] + NA_WIN_ROWS - 1, 0, 2 * NA_WIN_ROWS - 2)
    dc = np.clip(c[None, :] - c[:, None] + NA_WIN_COLS - 1, 0, 2 * NA_WIN_COLS - 2)
    return jnp.where(ok[None], rpb[:, dr, dc].astype(F32), NEG)


def _lat_na_kernel(q_ref, k_ref, v_ref, ck_ref, cv_ref, bias_ref, y_ref):
    kb = k_ref[...].astype(BF16)
    vb = v_ref[...].astype(BF16)
    ck = ck_ref[0, 0, 0].astype(BF16)
    cv = cv_ref[0, 0, 0].astype(BF16)
    for qb in range(DEC_SEQ // LAT_Q_ROWS):
        rs = slice(LAT_Q_ROWS * qb, LAT_Q_ROWS * (qb + 1))
        q = q_ref[rs, :].astype(BF16)
        s_loc = _dot_nt(q, kb) * ATT_SCALE + bias_ref[0, rs, :]
        s_ctx = _dot_nt(q, ck) * ATT_SCALE
        m = jnp.maximum(jnp.max(s_loc, axis=-1, keepdims=True), jnp.max(s_ctx, axis=-1, keepdims=True))
        e_loc = jnp.exp(s_loc - m)
        e_ctx = jnp.exp(s_ctx - m)
        den = jnp.sum(e_loc, axis=-1, keepdims=True) + jnp.sum(e_ctx, axis=-1, keepdims=True)
        out = _dot((e_loc / den).astype(BF16), vb) + _dot((e_ctx / den).astype(BF16), cv)
        y_ref[rs, :] = out.astype(BF16)


def _lat_na(proj, cache_k, cache_v, layer, bias):
    rows_per_seq = DEC_SEQ
    cache_spec = pl.BlockSpec((1, 1, 1, PAST_LEN, HEAD_DIM), lambda h, b: (b, layer, h, 0, 0))

    def qkv_spec(c):
        return pl.BlockSpec((rows_per_seq, HEAD_DIM), lambda h, b: (LAT_ROW0 + b, c * NA_HEADS + h))

    return pl.pallas_call(
        _lat_na_kernel,
        grid=(NA_HEADS, DEC_BATCH),
        in_specs=[qkv_spec(0), qkv_spec(1), qkv_spec(2), cache_spec, cache_spec,
                  pl.BlockSpec((1, DEC_SEQ, DEC_SEQ), lambda h, b: (h, 0, 0))],
        out_specs=pl.BlockSpec((rows_per_seq, HEAD_DIM), lambda h, b: (b, h)),
        out_shape=jax.ShapeDtypeStruct((T_SAMPLE, NA_W), BF16),
        compiler_params=_params(("parallel", "parallel")),
        name="lat_na",
    )(proj, proj, proj, cache_k, cache_v, bias)


S5_TC = 256
S5_PITCH = S5_TC + 8
S5_SLABS = SSM_GROUPS * SSM_STATE // 128
S5_QB = 4
S5_SLABS_Q = S5_SLABS // S5_QB


def _s5_disc_kernel(lr_ref, li_ref, dt_ref, br_ref, bi_ref, ar_ref, ai_ref, bbr_ref, bbi_ref):
    lr, li, dt = lr_ref[...], li_ref[...], dt_ref[...]
    mag = jnp.exp(lr * dt)
    ab_re = mag * jnp.cos(li * dt)
    ab_im = mag * jnp.sin(li * dt)
    den = lr * lr + li * li
    nr, ni = ab_re - 1.0, ab_im
    co_re = (nr * lr + ni * li) / den
    co_im = (ni * lr - nr * li) / den
    br, bi = br_ref[...], bi_ref[...]
    ar_ref[...] = ab_re
    ai_ref[...] = ab_im
    bbr_ref[...] = co_re * br - co_im * bi
    bbi_ref[...] = co_re * bi + co_im * br


def _s5_params(lam_re, lam_im, log_dt, b_re, b_im, c_re, c_im):
    g, p, c = SSM_GROUPS, SSM_STATE, SSM_GROUP
    shape = (2, g, c, p)
    n = 2 * g * c
    bc = lambda a: jnp.broadcast_to(a, shape).reshape(n, p)
    dt = jnp.exp(log_dt)
    args = (bc(lam_re[:, :, None, :]), bc(lam_im[:, :, None, :]), bc(dt[:, :, None, None]),
            jnp.transpose(b_re, (0, 1, 3, 2)).reshape(n, p), jnp.transpose(b_im, (0, 1, 3, 2)).reshape(n, p))
    spec = pl.BlockSpec((n, p), lambda: (0, 0))
    out = jax.ShapeDtypeStruct((n, p), F32)
    ab_re, ab_im, bb_re, bb_im = pl.pallas_call(
        _s5_disc_kernel, in_specs=[spec] * 5, out_specs=[spec] * 4, out_shape=[out] * 4, name="s5_disc")(*args)
    a = jnp.stack([ab_re.reshape(shape)[:, :, 0, :], ab_im.reshape(shape)[:, :, 0, :]], axis=1)
    a = a.reshape(2, 2, S5_SLABS, 128)
    eye = jnp.eye(16, dtype=F32)

    def blockdiag_in(bb):
        return jnp.einsum('dqgcp,gh->dqgchp', bb.reshape(2, S5_QB, 16, c, p), eye).reshape(2, S5_QB, 16 * c, 16 * p)

    def blockdiag_out(cc):
        return jnp.einsum('dqgcp,gh->dqgphc', cc.reshape(2, S5_QB, 16, c, p), eye).reshape(2, S5_QB, 16 * p, 16 * c)

    bb = jnp.concatenate([blockdiag_in(bb_re.reshape(shape)), blockdiag_in(bb_im.reshape(shape))], axis=-1)
    cc = jnp.stack([blockdiag_out(c_re), blockdiag_out(-c_im)], axis=2)
    return a, bb.astype(BF16), cc.astype(BF16)


def _s5_scan_kernel(u0, u1, u2, u3, bb_ref, cc_ref, a_ref, h0_ref, y_ref, hout_ref,
                    bur, bui, xsr, xsi, st, *, n_chunks):
    d = pl.program_id(1)
    c = pl.program_id(2)

    @pl.when(c == 0)
    def _():
        st[0] = h0_ref[0, 0, 0]
        st[1] = h0_ref[0, 0, 1]

    half = S5_SLABS_Q * 128
    for q, u in enumerate((u0, u1, u2, u3)):
        buq = _dot(u[...].astype(BF16), bb_ref[0, q])
        for j in range(S5_SLABS_Q):
            row0 = (q * S5_SLABS_Q + j) * S5_PITCH
            bur[row0:row0 + S5_TC, :] = buq[:, 128 * j:128 * (j + 1)]
            bui[row0:row0 + S5_TC, :] = buq[:, half + 128 * j:half + 128 * (j + 1)]

    ar = a_ref[0, 0]
    ai = a_ref[0, 1]

    def step(i, carry):
        xr, xi = carry
        t = jnp.where(d == 0, i, S5_TC - 1 - i)
        rows = pl.ds(t, S5_SLABS, stride=S5_PITCH)
        nxr = ar * xr - ai * xi + bur[rows, :]
        nxi = ar * xi + ai * xr + bui[rows, :]
        xsr[rows, :] = nxr
        xsi[rows, :] = nxi
        return nxr, nxi

    xr, xi = lax.fori_loop(0, S5_TC, step, (st[0], st[1]))
    st[0] = xr
    st[1] = xi

    @pl.when(c == n_chunks - 1)
    def _():
        hout_ref[0, 0, 0] = xr
        hout_ref[0, 0, 1] = xi

    for q in range(S5_QB):
        acc = jnp.zeros((S5_TC, 16 * SSM_GROUP), F32)
        for j in range(S5_SLABS_Q):
            row0 = (q * S5_SLABS_Q + j) * S5_PITCH
            ks = slice(128 * j, 128 * (j + 1))
            acc += _dot(xsr[row0:row0 + S5_TC, :].astype(BF16), cc_ref[0, q, 0, ks, :])
            acc += _dot(xsi[row0:row0 + S5_TC, :].astype(BF16), cc_ref[0, q, 1, ks, :])
        y_ref[0, :, 256 * q:256 * (q + 1)] = acc


def _s5_scan(proj, row_block0, n_seq, n_chunks, a, bb, cc, h0):
    ucol = OFF_SSM // 256

    def chunk(d, c):
        return jnp.where(d == 0, c, n_chunks - 1 - c)

    def u_spec(q):
        return pl.BlockSpec((S5_TC, 256), lambda s, d, c: (row_block0 + s * n_chunks + chunk(d, c), ucol + q))

    slab = pltpu.VMEM((S5_SLABS * S5_PITCH, 128), F32)
    rows = n_seq * n_chunks * S5_TC
    return pl.pallas_call(
        functools.partial(_s5_scan_kernel, n_chunks=n_chunks),
        grid=(n_seq, 2, n_chunks),
        in_specs=[u_spec(q) for q in range(S5_QB)] + [
            pl.BlockSpec((1, S5_QB, 256, 2 * S5_SLABS_Q * 128), lambda s, d, c: (d, 0, 0, 0)),
            pl.BlockSpec((1, S5_QB, 2, S5_SLABS_Q * 128, 256), lambda s, d, c: (d, 0, 0, 0, 0)),
            pl.BlockSpec((1, 2, S5_SLABS, 128), lambda s, d, c: (d, 0, 0, 0)),
            pl.BlockSpec((1, 1, 2, S5_SLABS, 128), lambda s, d, c: (s, d, 0, 0, 0)),
        ],
        out_specs=[
            pl.BlockSpec((1, S5_TC, SSM_WIDTH), lambda s, d, c: (d, s * n_chunks + chunk(d, c), 0)),
            pl.BlockSpec((1, 1, 2, S5_SLABS, 128), lambda s, d, c: (s, d, 0, 0, 0)),
        ],
        out_shape=[jax.ShapeDtypeStruct((2, rows, SSM_WIDTH), F32),
                   jax.ShapeDtypeStruct((n_seq, 2, 2, S5_SLABS, 128), F32)],
        scratch_shapes=[slab, slab, slab, slab, pltpu.VMEM((2, S5_SLABS, 128), F32)],
        compiler_params=_params(("parallel", "parallel", "arbitrary")),
        name="s5_scan",
    )(proj, proj, proj, proj, bb, cc, a, h0)


def _s5_glu_kernel(ua_ref, ub_ref, y0_ref, y1_ref, d_ref, w_ref, o_ref):
    u = jnp.concatenate([ua_ref[...], ub_ref[...]], axis=-1)
    y = jax.nn.gelu(u * d_ref[0] + y0_ref[0] + y1_ref[0])
    o_ref[...] = (y * jax.nn.sigmoid(_dot(y.astype(BF16), w_ref[0].astype(BF16)))).astype(BF16)


S5_GLU_ROWS = 512


def _s5_glu(proj, row0, y, ssm_d, w_glu, layer):
    rows = y.shape[1]
    tr = S5_GLU_ROWS
    blk0 = row0 // tr
    ucol = OFF_SSM // 512
    return pl.pallas_call(
        _s5_glu_kernel,
        grid=(rows // tr,),
        in_specs=[
            pl.BlockSpec((tr, 512), lambda i: (blk0 + i, ucol)),
            pl.BlockSpec((tr, 512), lambda i: (blk0 + i, ucol + 1)),
            pl.BlockSpec((1, tr, SSM_WIDTH), lambda i: (0, i, 0)),
            pl.BlockSpec((1, tr, SSM_WIDTH), lambda i: (1, i, 0)),
            pl.BlockSpec((1, 1, SSM_WIDTH), lambda i: (layer, 0, 0)),
            pl.BlockSpec((1, SSM_WIDTH, SSM_WIDTH), lambda i: (layer, 0, 0)),
        ],
        out_specs=pl.BlockSpec((tr, SSM_WIDTH), lambda i: (i, 0)),
        out_shape=jax.ShapeDtypeStruct((rows, SSM_WIDTH), BF16),
        compiler_params=_params(("parallel",)),
        name="s5_glu",
    )(proj, proj, y, y, ssm_d.reshape(DEPTH, 1, SSM_WIDTH), w_glu)


RW_ROWS = 256
RW_C = 64
RW_PAIRS = RWKV_HEADS // 2
RW_LOWRANK0 = 3 * RWKV_WIDTH


def _head_indicator():
    ind = (np.arange(RWKV_WIDTH)[:, None] // RWKV_HEAD == np.arange(128)[None, :]).astype(np.float32)
    return jnp.asarray(ind, BF16), jnp.asarray(ind.T, BF16)


def _head_sum(x, ind, ind_t):
    return _dot_exact_lhs(_dot_exact_lhs(x, ind), ind_t)


def _rwkv_prep_kernel(x_ref, prev_ref, next_ref, mu_ref, kk_ref, ka_ref, rk_ref, w0_ref, a0_ref, w2_ref, a2_ref,
                      g2_ref, ind_ref, indt_ref,
                      r_out, kkn_out, v_out, g_out, bonus_out, lw_out, kka_out, kd_out, *, seq_blocks):
    i = pl.program_id(0)
    n_prompt_blocks = T_PROMPT // RW_ROWS
    j = i - n_prompt_blocks
    in_seq = jnp.where(i < n_prompt_blocks, i % seq_blocks[0], j % seq_blocks[1])
    n_in_seq = jnp.where(i < n_prompt_blocks, seq_blocks[0], seq_blocks[1])
    first = in_seq == 0
    last = in_seq == n_in_seq - 1
    x = x_ref[...]
    row = lax.broadcasted_iota(jnp.int32, x.shape, 0)
    prev_row = jnp.where(first, 0.0, prev_ref[7:8, :])
    next_row = jnp.where(last, 0.0, next_ref[0:1, :])
    xp = jnp.where(row == 0, prev_row, pltpu.roll(x, 1, 0))
    xn = jnp.where(row == RW_ROWS - 1, next_row, pltpu.roll(x, RW_ROWS - 1, 0))
    z = x + mu_ref[...] * (0.5 * (xp + xn) - x)

    w = RWKV_WIDTH
    r, k, v = z[:, 0:w], z[:, w:2 * w], z[:, 2 * w:3 * w]
    low = z[:, RW_LOWRANK0:RW_LOWRANK0 + 128]
    gl = z[:, RW_LOWRANK0 + 128:RW_LOWRANK0 + 256]
    ind, ind_t = ind_ref[...], indt_ref[...]

    kk = k * kk_ref[...]
    kk = kk * lax.rsqrt(_head_sum(kk * kk, ind, ind_t) + 1e-12)
    r_out[...] = r
    kkn_out[...] = kk
    v_out[...] = v
    g_out[...] = _dot(jax.nn.sigmoid(gl).astype(BF16), g2_ref[...].astype(BF16))
    tanh_low = jnp.tanh(low).astype(BF16)
    low_b = low.astype(BF16)
    bonus = jnp.zeros_like(r)
    for d in range(2):
        w_log = -jax.nn.softplus(-(w0_ref[d:d + 1, :] + _dot(tanh_low, w2_ref[d].astype(BF16)))) - 0.5
        a = jax.nn.sigmoid(a0_ref[d:d + 1, :] + _dot(low_b, a2_ref[d].astype(BF16)))
        kd = k * (1.0 + (a - 1.0) * ka_ref[...])
        lw_out[d] = -jnp.exp(w_log)
        kka_out[d] = kk * a
        kd_out[d] = kd
        bonus = bonus + _head_sum(r * kd * rk_ref[...], ind, ind_t) * v
    bonus_out[...] = bonus


def _rwkv_prep(proj_r, lp):
    t = proj_r.shape[0]
    nb = t // RW_ROWS
    halo = RW_ROWS // 8
    w = RWKV_WIDTH
    zpad = jnp.zeros((2, 64, w), F32)
    w2 = jnp.concatenate([lp['rwkv_w2'], zpad], axis=1)
    a2 = jnp.concatenate([zpad, lp['rwkv_a2']], axis=1)
    ind, ind_t = _head_indicator()
    row = lambda a: a.reshape(1, -1)
    full = lambda shape: pl.BlockSpec(shape, lambda i: (0,) * len(shape))
    tok = pl.BlockSpec((RW_ROWS, w), lambda i: (i, 0))
    tok2 = pl.BlockSpec((2, RW_ROWS, w), lambda i: (0, i, 0))
    f1 = jax.ShapeDtypeStruct((t, w), F32)
    f2 = jax.ShapeDtypeStruct((2, t, w), F32)
    return pl.pallas_call(
        functools.partial(_rwkv_prep_kernel, seq_blocks=(SEQ // RW_ROWS, DEC_SEQ // RW_ROWS)),
        grid=(nb,),
        in_specs=[
            pl.BlockSpec((RW_ROWS, RWKV_IN_W), lambda i: (i, 0)),
            pl.BlockSpec((8, RWKV_IN_W), lambda i: (jnp.maximum(i * halo - 1, 0), 0)),
            pl.BlockSpec((8, RWKV_IN_W), lambda i: (jnp.minimum((i + 1) * halo, nb * halo - 1), 0)),
            full((1, RWKV_IN_W)), full((1, w)), full((1, w)), full((1, w)), full((2, w)), full((2, w)),
            full((2, 128, w)), full((2, 128, w)), full((GATE_RANK, w)), full((w, 128)), full((128, w)),
        ],
        out_specs=[tok, tok, tok, tok, tok, tok2, tok2, tok2],
        out_shape=[f1, f1, f1, f1, f1, f2, f2, f2],
        compiler_params=_params(("parallel",)),
        name="rwkv_prep",
    )(proj_r, proj_r, proj_r, row(lp['rwkv_mu']), row(lp['rwkv_k_k']), row(lp['rwkv_k_a']), row(lp['rwkv_r_k']),
      lp['rwkv_w0'], lp['rwkv_a0'], w2, a2, lp['rwkv_g2'], ind, ind_t)


def _rwkv_chunk_kernel(r_ref, kk_ref, v_ref, lw_ref, kka_ref, kd_ref, r2_out, y0_out, mneg_out, sadd_out, gc_out):
    d = pl.program_id(0)
    c = RW_C
    ri = lax.broadcasted_iota(jnp.int32, (2 * c, 2 * c), 0)
    ci = lax.broadcasted_iota(jnp.int32, (2 * c, 2 * c), 1)
    lower_half_rows = ri >= c
    same_head = lower_half_rows == (ci >= c)
    ti, tj = ri % c, ci % c
    before = (tj - ti) * (1 - 2 * d) < 0
    strict = same_head & before
    incl = same_head & (before | (ti == tj))
    eye = (ri == ci).astype(F32)
    tri_c = (incl[0:c, 0:c]).astype(BF16)
    lane = lax.broadcasted_iota(jnp.int32, (1, 2 * c), 1)
    m_a = (lane < RWKV_HEAD).astype(F32)
    m_b = 1.0 - m_a
    bf = lambda a: a.astype(BF16)

    def pair(p, carry):
        ls = pl.ds(pl.multiple_of(p * 128, 128), 128)
        r, kk, v = r_ref[:, ls], kk_ref[:, ls], v_ref[:, ls]
        lw, kka, kd = lw_ref[0, :, ls], kka_ref[0, :, ls], kd_ref[0, :, ls]
        cum = _dot_exact_rhs(tri_c, lw)
        tot = jnp.where(d == 0, cum[c - 1:c, :], cum[0:1, :])
        g_rem = jnp.exp(tot - cum)
        g_inv = jnp.exp(-cum)
        qk = kk * jnp.exp(cum - lw)
        rt = r * jnp.exp(cum)
        kh = kd * g_rem
        ah = kka * g_rem
        lhs = jnp.concatenate([qk * m_a, qk * m_b, rt * m_a, rt * m_b], axis=0)
        rhs = jnp.concatenate([kd * g_inv, kka * g_inv], axis=0)
        sc = _dot_nt(bf(lhs), bf(rhs))
        x, z = sc[0:2 * c], sc[2 * c:4 * c]
        xr, zr = pltpu.roll(x, c, 1), pltpu.roll(z, c, 1)
        qk_bd = jnp.where(strict, jnp.where(lower_half_rows, xr, x), 0.0)
        qa_bd = jnp.where(strict, jnp.where(lower_half_rows, x, xr), 0.0)
        rk_bd = jnp.where(incl, jnp.where(lower_half_rows, zr, z), 0.0)
        ra_bd = jnp.where(incl, jnp.where(lower_half_rows, z, zr), 0.0)
        inv = eye - qa_bd
        power = qa_bd
        for _ in range(5):
            pb = bf(power)
            power = _dot(pb, pb)
            inv = inv + _dot(bf(inv), bf(power))
        inv_b = bf(inv)
        v_st = bf(jnp.concatenate([v * m_a, v * m_b], axis=0))
        w_st = _dot(inv_b, bf(lhs[0:2 * c]))
        u0_st = _dot(inv_b, bf(_dot(bf(qk_bd), v_st)))
        ra_b = bf(ra_bd)
        r2_st = lhs[2 * c:4 * c] - _dot(ra_b, bf(w_st))
        y0_st = _dot(bf(rk_bd), v_st) - _dot(ra_b, bf(u0_st))
        w_f = w_st[0:c] + w_st[c:2 * c]
        u0_f = u0_st[0:c] + u0_st[c:2 * c]
        r2_out[0, :, ls] = r2_st[0:c] + r2_st[c:2 * c]
        y0_out[0, :, ls] = y0_st[0:c] + y0_st[c:2 * c]
        ah_b = bf(ah)
        mneg_out[0, 0, p] = jnp.where(same_head, _dot_tn(bf(w_f), ah_b), 0.0)
        sadd_out[0, 0, p] = jnp.where(same_head, _dot_tn(bf(v), bf(kh)) - _dot_tn(bf(u0_f), ah_b), 0.0)
        gc_out[0, 0, :, ls] = jnp.exp(tot)
        return carry

    lax.fori_loop(0, RW_PAIRS, pair, 0)


def _rwkv_chunks(r, kk, v, lw, kka, kd):
    t = r.shape[0]
    nck = t // RW_C
    w = RWKV_WIDTH
    tok = pl.BlockSpec((RW_C, w), lambda d, i: (i, 0))
    tok2 = pl.BlockSpec((1, RW_C, w), lambda d, i: (d, i, 0))
    mat = pl.BlockSpec((1, 1, RW_PAIRS, 128, 128), lambda d, i: (d, i, 0, 0, 0))
    mat_shape = jax.ShapeDtypeStruct((2, nck, RW_PAIRS, 128, 128), F32)
    return pl.pallas_call(
        _rwkv_chunk_kernel,
        grid=(2, nck),
        in_specs=[tok, tok, tok, tok2, tok2, tok2],
        out_specs=[tok2, tok2, mat, mat, pl.BlockSpec((1, 1, 1, w), lambda d, i: (d, i, 0, 0))],
        out_shape=[jax.ShapeDtypeStruct((2, t, w), F32), jax.ShapeDtypeStruct((2, t, w), F32), mat_shape, mat_shape,
                   jax.ShapeDtypeStruct((2, nck, 1, w), F32)],
        compiler_params=_params(("parallel", "parallel")),
        name="rwkv_chunks",
    )(r, kk, v, lw, kka, kd)


RW_BLOCK_CHUNKS = 4


def _rwkv_state_kernel(r2_ref, y0_ref, mneg_ref, sadd_ref, gc_ref, s0_ref, y_out, s_out, st, *, n_blocks):
    d = pl.program_id(1)
    b = pl.program_id(2)

    @pl.when(b == 0)
    def _():
        st[...] = s0_ref[0, 0]

    def chunk(i, carry):
        ck = jnp.where(d == 0, i, RW_BLOCK_CHUNKS - 1 - i)
        rows = pl.ds(pl.multiple_of(ck * RW_C, RW_C), RW_C)
        for p in range(RW_PAIRS):
            ls = slice(128 * p, 128 * (p + 1))
            s = st[p]
            sb = s.astype(BF16)
            y_out[0, rows, ls] = _dot_nt(r2_ref[0, rows, ls].astype(BF16), sb) + y0_ref[0, rows, ls]
            st[p] = s * gc_ref[0, ck, :, ls] - _dot(sb, mneg_ref[0, ck, p].astype(BF16)) + sadd_ref[0, ck, p]
        return carry

    lax.fori_loop(0, RW_BLOCK_CHUNKS, chunk, 0)

    @pl.when(b == n_blocks - 1)
    def _():
        s_out[0, 0] = st[...]


def _rwkv_state(r2, y0, mneg, sadd, gc, s0, row0, n_seq, seq_len):
    rows = RW_BLOCK_CHUNKS * RW_C
    n_blocks = seq_len // rows
    blk0 = row0 // rows
    w = RWKV_WIDTH

    def blk(s, d, b):
        return blk0 + s * n_blocks + jnp.where(d == 0, b, n_blocks - 1 - b)

    tok = pl.BlockSpec((1, rows, w), lambda s, d, b: (d, blk(s, d, b), 0))
    mat = pl.BlockSpec((1, RW_BLOCK_CHUNKS, RW_PAIRS, 128, 128), lambda s, d, b: (d, blk(s, d, b), 0, 0, 0))
    state = pl.BlockSpec((1, 1, RW_PAIRS, 128, 128), lambda s, d, b: (s, d, 0, 0, 0))
    return pl.pallas_call(
        functools.partial(_rwkv_state_kernel, n_blocks=n_blocks),
        grid=(n_seq, 2, n_blocks),
        in_specs=[tok, tok, mat, mat,
                  pl.BlockSpec((1, RW_BLOCK_CHUNKS, 1, w), lambda s, d, b: (d, blk(s, d, b), 0, 0)), state],
        out_specs=[pl.BlockSpec((1, rows, w), lambda s, d, b: (d, blk(s, d, b) - blk0, 0)), state],
        out_shape=[jax.ShapeDtypeStruct((2, n_seq * seq_len, w), F32),
                   jax.ShapeDtypeStruct((n_seq, 2, RW_PAIRS, 128, 128), F32)],
        scratch_shapes=[pltpu.VMEM((RW_PAIRS, 128, 128), F32)],
        compiler_params=_params(("parallel", "parallel", "arbitrary")),
        name="rwkv_state",
    )(r2, y0, mneg, sadd, gc, s0)


def _pack_state(s):
    n = s.shape[0]
    sp = s.reshape(n, 2, RW_PAIRS, 2, RWKV_HEAD, RWKV_HEAD)
    z = jnp.zeros_like(sp[:, :, :, 0])
    top = jnp.concatenate([sp[:, :, :, 0], z], axis=-1)
    bot = jnp.concatenate([z, sp[:, :, :, 1]], axis=-1)
    return jnp.concatenate([top, bot], axis=-2)


def _unpack_state(sp):
    n = sp.shape[0]
    h = RWKV_HEAD
    return jnp.stack([sp[..., 0:h, 0:h], sp[..., h:, h:]], axis=3).reshape(n, 2, RWKV_HEADS, h, h)


def _rwkv_post_kernel(y_ref, bonus_ref, g_ref, lng_ref, lnb_ref, ind_ref, indt_ref, o_ref):
    ind, ind_t = ind_ref[...], indt_ref[...]
    y = y_ref[0] + y_ref[1]
    mu = _head_sum(y, ind, ind_t) * (1.0 / RWKV_HEAD)
    yc = y - mu
    var = _head_sum(yc * yc, ind, ind_t) * (1.0 / RWKV_HEAD)
    yn = yc * lax.rsqrt(var + RWKV_GN_EPS)
    o_ref[...] = ((yn * lng_ref[...] + lnb_ref[...] + bonus_ref[...]) * g_ref[...]).astype(BF16)


def _rwkv_post(y, bonus, g, ln_g, ln_b):
    t = bonus.shape[0]
    w = RWKV_WIDTH
    ind, ind_t = _head_indicator()
    tok = pl.BlockSpec((RW_ROWS, w), lambda i: (i, 0))
    full = lambda shape: pl.BlockSpec(shape, lambda i: (0,) * len(shape))
    return pl.pallas_call(
        _rwkv_post_kernel,
        grid=(t // RW_ROWS,),
        in_specs=[pl.BlockSpec((2, RW_ROWS, w), lambda i: (0, i, 0)), tok, tok, full((1, w)), full((1, w)),
                  full((w, 128)), full((128, w))],
        out_specs=tok,
        out_shape=jax.ShapeDtypeStruct((t, w), BF16),
        compiler_params=_params(("parallel",)),
        name="rwkv_post",
    )(y, bonus, g, ln_g.reshape(1, w), ln_b.reshape(1, w), ind, ind_t)


def _rwkv_mixer(proj_r, lp, s0_sample):
    r, kk, v, g, bonus, lw, kka, kd = _rwkv_prep(proj_r, lp)
    r2, y0, mneg, sadd, gc = _rwkv_chunks(r, kk, v, lw, kka, kd)
    zero = jnp.zeros((BATCH, 2, RW_PAIRS, 128, 128), F32)
    y_p, s_p = _rwkv_state(r2, y0, mneg, sadd, gc, zero, 0, BATCH, SEQ)
    y_s, _ = _rwkv_state(r2, y0, mneg, sadd, gc, _pack_state(s0_sample), T_PROMPT, DEC_BATCH, DEC_SEQ)
    y = jnp.concatenate([y_p, y_s], axis=1)
    return _rwkv_post(y, bonus, g, lp['rwkv_ln_g'], lp['rwkv_ln_b']), _unpack_state(s_p)


MOE_ROWS = 256
MOE_BLOCKS = T_ALL * EXPERT_TOPK // MOE_ROWS + N_EXPERTS
MOE_PAD_ROWS = MOE_BLOCKS * MOE_ROWS
MOE_FF_TILE = 256
MOE_OUT_TILE = 2048


def _top2_sum(vals):
    best = None
    for a in range(len(vals)):
        for b in range(a + 1, len(vals)):
            s = vals[a] + vals[b]
            best = s if best is None else jnp.maximum(best, s)
    return best


def _first_argmax(vals):
    idx = jnp.zeros(vals[0].shape, jnp.int32)
    best = vals[0]
    for j in range(1, len(vals)):
        upd = vals[j] > best
        idx = jnp.where(upd, j, idx)
        best = jnp.where(upd, vals[j], best)
    return idx, best


def _pick(idx, vals):
    out = vals[-1]
    for j in range(len(vals) - 2, -1, -1):
        out = jnp.where(idx == j, vals[j], out)
    return out


def _ffn_norm_route_kernel(x_ref, g_ref, mod_ref, wh_ref, wl_ref, bias_ref, h_ref, idx_ref, wts_ref):
    y = _rms(x_ref[...], g_ref[...])
    h = y * (1.0 + mod_ref[0, 4:5, :]) + mod_ref[0, 3:4, :]
    h_ref[...] = h
    hh = h.astype(BF16)
    hl = (h - hh.astype(F32)).astype(BF16)
    wh, wl = wh_ref[...], wl_ref[...]
    logits = _dot_nt(wh, hh) + _dot_nt(wh, hl) + _dot_nt(wl, hh)
    scores = jax.nn.sigmoid(logits)
    sel = scores + bias_ref[...]
    gs = EXPERTS_PER_GROUP
    sel_rows = [sel[e:e + 1, :] for e in range(N_EXPERTS)]
    sc_rows = [scores[e:e + 1, :] for e in range(N_EXPERTS)]
    grp, _ = _first_argmax([_top2_sum(sel_rows[gs * g:gs * (g + 1)]) for g in range(N_EXPERT_GROUPS)])
    v = [_pick(grp, [sel_rows[gs * g + j] for g in range(N_EXPERT_GROUPS)]) for j in range(gs)]
    s = [_pick(grp, [sc_rows[gs * g + j] for g in range(N_EXPERT_GROUPS)]) for j in range(gs)]
    i1, _ = _first_argmax(v)
    i2, _ = _first_argmax([jnp.where(i1 == j, -jnp.inf, v[j]) for j in range(gs)])
    w1, w2 = _pick(i1, s), _pick(i2, s)
    tot = w1 + w2
    idx_ref[0:1, :] = grp * gs + i1
    idx_ref[1:2, :] = grp * gs + i2
    wts_ref[0:1, :] = w1 / tot
    wts_ref[1:2, :] = w2 / tot


def _ffn_norm_route(x, g, mod, w_router, router_bias):
    t = x.shape[0]
    wt = w_router.T
    wh = wt.astype(BF16)
    wl = (wt - wh.astype(F32)).astype(BF16)
    full = lambda shape: pl.BlockSpec(shape, lambda i: (0,) * len(shape))
    return pl.pallas_call(
        _ffn_norm_route_kernel,
        grid=(t // NORM_ROWS,),
        in_specs=[
            pl.BlockSpec((NORM_ROWS, D_MODEL), lambda i: (i, 0)),
            full((1, D_MODEL)),
            pl.BlockSpec((1, 6, D_MODEL), lambda i: (_cond_group(i, NORM_ROWS), 0, 0)),
            full((N_EXPERTS, D_MODEL)), full((N_EXPERTS, D_MODEL)), full((N_EXPERTS, 1)),
        ],
        out_specs=[pl.BlockSpec((NORM_ROWS, D_MODEL), lambda i: (i, 0)),
                   pl.BlockSpec((EXPERT_TOPK, NORM_ROWS), lambda i: (0, i)),
                   pl.BlockSpec((EXPERT_TOPK, NORM_ROWS), lambda i: (0, i))],
        out_shape=[jax.ShapeDtypeStruct((t, D_MODEL), F32),
                   jax.ShapeDtypeStruct((EXPERT_TOPK, t), jnp.int32),
                   jax.ShapeDtypeStruct((EXPERT_TOPK, t), F32)],
        compiler_params=_params(("parallel",)),
        name="ffn_norm_route",
    )(x, g.reshape(1, D_MODEL), mod, wh, wl, router_bias.reshape(N_EXPERTS, 1))


def _dispatch_plan(idx, wts):
    t = idx.shape[1]
    flat_e = idx.T.reshape(-1)
    flat_w = wts.T.reshape(-1)
    onehot = (flat_e[:, None] == jnp.arange(N_EXPERTS, dtype=jnp.int32)[None, :]).astype(jnp.int32)
    csum = jnp.cumsum(onehot, axis=0)
    rank = jnp.take_along_axis(csum, flat_e[:, None], axis=1)[:, 0] - 1
    counts = csum[-1]
    padded = ((counts + MOE_ROWS - 1) // MOE_ROWS) * MOE_ROWS
    pad_end = jnp.cumsum(padded)
    pad_start = pad_end - padded
    dest = (pad_start[flat_e] + rank).astype(jnp.int32)
    tok = jnp.arange(t * EXPERT_TOPK, dtype=jnp.int32) // EXPERT_TOPK
    row_tok = jnp.zeros((MOE_PAD_ROWS,), jnp.int32).at[dest].set(tok)
    row_w = jnp.zeros((MOE_PAD_ROWS,), F32).at[dest].set(flat_w)
    block_e = jnp.minimum(
        jnp.searchsorted(pad_end, jnp.arange(MOE_BLOCKS, dtype=jnp.int32) * MOE_ROWS, side='right'),
        N_EXPERTS - 1).astype(jnp.int32)
    n_used = (pad_end[-1] // MOE_ROWS).astype(jnp.int32).reshape(1)
    return row_tok, row_w.reshape(MOE_PAD_ROWS, 1), block_e, n_used, dest


def _row_copy(src_hbm, row, dst, slot, sem):
    return pltpu.make_async_copy(src_hbm.at[pl.ds(row, 1), :], dst.at[pl.ds(slot, 1), :], sem)


def _gather_rows_kernel(tok_ref, h_hbm, o_ref, buf, sem):
    base = pl.program_id(0) * MOE_ROWS

    def issue(r, c):
        _row_copy(h_hbm, tok_ref[base + r], buf, r, sem).start()
        return c

    def wait(r, c):
        _row_copy(h_hbm, 0, buf, r, sem).wait()
        return c

    lax.fori_loop(0, MOE_ROWS, issue, 0)
    lax.fori_loop(0, MOE_ROWS, wait, 0)
    o_ref[...] = buf[...].astype(BF16)


def _gather_rows(h, row_tok):
    return pl.pallas_call(
        _gather_rows_kernel,
        grid_spec=pltpu.PrefetchScalarGridSpec(
            num_scalar_prefetch=1,
            grid=(MOE_BLOCKS,),
            in_specs=[pl.BlockSpec(memory_space=pl.ANY)],
            out_specs=pl.BlockSpec((MOE_ROWS, D_MODEL), lambda i, tok: (i, 0)),
            scratch_shapes=[pltpu.VMEM((MOE_ROWS, D_MODEL), F32), pltpu.SemaphoreType.DMA(())],
        ),
        out_shape=jax.ShapeDtypeStruct((MOE_PAD_ROWS, D_MODEL), BF16),
        compiler_params=_params(("arbitrary",)),
        name="moe_gather",
    )(row_tok, h)


def _expert_up_kernel(be_ref, nu_ref, x_ref, w1_ref, w3_ref, o_ref):
    i = pl.program_id(1)

    @pl.when(i < nu_ref[0])
    def _():
        x = x_ref[...]
        a = _dot(x, w1_ref[0, 0].astype(BF16))
        b = _dot(x, w3_ref[0, 0].astype(BF16))
        o_ref[...] = (a * jax.nn.sigmoid(a) * b).astype(BF16)

    @pl.when(i >= nu_ref[0])
    def _():
        o_ref[...] = jnp.zeros_like(o_ref)


def _expert_up(xb, w_gate, w_up, layer, block_e, n_used):
    wspec = pl.BlockSpec((1, 1, D_MODEL, MOE_FF_TILE), lambda j, i, be, nu: (layer, be[i], 0, j))
    return pl.pallas_call(
        _expert_up_kernel,
        grid_spec=pltpu.PrefetchScalarGridSpec(
            num_scalar_prefetch=2,
            grid=(EXPERT_FF // MOE_FF_TILE, MOE_BLOCKS),
            in_specs=[pl.BlockSpec((MOE_ROWS, D_MODEL), lambda j, i, be, nu: (i, 0)), wspec, wspec],
            out_specs=pl.BlockSpec((MOE_ROWS, MOE_FF_TILE), lambda j, i, be, nu: (i, j)),
        ),
        out_shape=jax.ShapeDtypeStruct((MOE_PAD_ROWS, EXPERT_FF), BF16),
        compiler_params=_params(("parallel", "arbitrary")),
        name="moe_up",
    )(block_e, n_used, xb, w_gate, w_up)


def _expert_down_kernel(be_ref, nu_ref, h_ref, w2_ref, rw_ref, o_ref):
    i = pl.program_id(1)

    @pl.when(i < nu_ref[0])
    def _():
        o_ref[...] = _dot(h_ref[...], w2_ref[0, 0].astype(BF16)) * rw_ref[...]

    @pl.when(i >= nu_ref[0])
    def _():
        o_ref[...] = jnp.zeros_like(o_ref)


def _expert_down(hmid, w_down, layer, block_e, n_used, row_w):
    return pl.pallas_call(
        _expert_down_kernel,
        grid_spec=pltpu.PrefetchScalarGridSpec(
            num_scalar_prefetch=2,
            grid=(D_MODEL // MOE_OUT_TILE, MOE_BLOCKS),
            in_specs=[
                pl.BlockSpec((MOE_ROWS, EXPERT_FF), lambda j, i, be, nu: (i, 0)),
                pl.BlockSpec((1, 1, EXPERT_FF, MOE_OUT_TILE), lambda j, i, be, nu: (layer, be[i], 0, j)),
                pl.BlockSpec((MOE_ROWS, 1), lambda j, i, be, nu: (i, 0)),
            ],
            out_specs=pl.BlockSpec((MOE_ROWS, MOE_OUT_TILE), lambda j, i, be, nu: (i, j)),
        ),
        out_shape=jax.ShapeDtypeStruct((MOE_PAD_ROWS, D_MODEL), F32),
        compiler_params=_params(("parallel", "arbitrary")),
        name="moe_down",
    )(block_e, n_used, hmid, w_down, row_w)


COMBINE_ROWS = 256


def _combine_kernel(pos_ref, yb_hbm, x_ref, mod_ref, o_ref, buf, sem):
    base = pl.program_id(0) * COMBINE_ROWS * EXPERT_TOPK

    def issue(r, c):
        for k in range(EXPERT_TOPK):
            _row_copy(yb_hbm, pos_ref[base + EXPERT_TOPK * r + k], buf.at[k], r, sem).start()
        return c

    def wait(r, c):
        for k in range(EXPERT_TOPK):
            _row_copy(yb_hbm, 0, buf.at[k], r, sem).wait()
        return c

    lax.fori_loop(0, COMBINE_ROWS, issue, 0)
    lax.fori_loop(0, COMBINE_ROWS, wait, 0)
    o_ref[...] = x_ref[...] + mod_ref[0, 5:6, :] * (buf[0] + buf[1])


def _combine(yb, dest, x, mod):
    t = x.shape[0]
    return pl.pallas_call(
        _combine_kernel,
        grid_spec=pltpu.PrefetchScalarGridSpec(
            num_scalar_prefetch=1,
            grid=(t // COMBINE_ROWS,),
            in_specs=[
                pl.BlockSpec(memory_space=pl.ANY),
                pl.BlockSpec((COMBINE_ROWS, D_MODEL), lambda i, pos: (i, 0)),
                pl.BlockSpec((1, 6, D_MODEL), lambda i, pos: (_cond_group(i, COMBINE_ROWS), 0, 0)),
            ],
            out_specs=pl.BlockSpec((COMBINE_ROWS, D_MODEL), lambda i, pos: (i, 0)),
            scratch_shapes=[pltpu.VMEM((EXPERT_TOPK, COMBINE_ROWS, D_MODEL), F32), pltpu.SemaphoreType.DMA(())],
        ),
        out_shape=jax.ShapeDtypeStruct((t, D_MODEL), F32),
        compiler_params=_params(("arbitrary",)),
        name="moe_combine",
    )(dest, yb, x, mod)


def _moe_residual(x, g, mod, lp_router, w_gate, w_up, w_down, layer):
    h, idx, wts = _ffn_norm_route(x, g, mod, *lp_router)
    row_tok, row_w, block_e, n_used, dest = _dispatch_plan(idx, wts)
    xb = _gather_rows(h, row_tok)
    hmid = _expert_up(xb, w_gate, w_up, layer, block_e, n_used)
    yb = _expert_down(hmid, w_down, layer, block_e, n_used, row_w)
    return _combine(yb, dest, x, mod)


def _layer_params(args, layer):
    return {k: v[layer] for k, v in args.items()}


def kernel(x_prompt, x_sample, cache_na_k, cache_na_v, cache_gqa_k, cache_gqa_v, state_ssm, state_rwkv, c, c_ctx,
           w_ada, b_ada, norm_mix, norm_ffn, w_in, na_rpb, gqa_q_norm, gqa_k_norm, ssm_lam_re, ssm_lam_im,
           ssm_log_dt, ssm_b_re, ssm_b_im, ssm_c_re, ssm_c_im, ssm_d, ssm_w_glu, rwkv_mu, rwkv_w0, rwkv_w2, rwkv_a0,
           rwkv_a2, rwkv_k_k, rwkv_k_a, rwkv_r_k, rwkv_g2, rwkv_ln_g, rwkv_ln_b, w_branch, w_out, w_router,
           router_bias, w_exp_gate, w_exp_up, w_exp_down, norm_final):
    x = jnp.concatenate([x_prompt.reshape(T_PROMPT, D_MODEL), x_sample.reshape(T_SAMPLE, D_MODEL)], axis=0)
    cond = jnp.concatenate([c_ctx[None, :], c, jnp.zeros((COND_PAD - N_COND, D_MODEL), F32)], axis=0)
    mod_all = _ada_all(cond, w_ada, b_ada).reshape(DEPTH, COND_PAD, 6, D_MODEL)
    cos, sin = _rope_tables()
    rwkv_args = dict(rwkv_mu=rwkv_mu, rwkv_w0=rwkv_w0, rwkv_w2=rwkv_w2, rwkv_a0=rwkv_a0, rwkv_a2=rwkv_a2,
                     rwkv_k_k=rwkv_k_k, rwkv_k_a=rwkv_k_a, rwkv_r_k=rwkv_r_k.reshape(DEPTH, RWKV_WIDTH),
                     rwkv_g2=rwkv_g2, rwkv_ln_g=rwkv_ln_g, rwkv_ln_b=rwkv_ln_b)
    caches = [[] for _ in range(6)]
    for l in range(DEPTH):
        mod = mod_all[l]
        h = _norm_mod(x, norm_mix[l], mod, 0)
        proj = _in_proj(h, w_in, l, 0, OFF_RWKV)
        proj_r = _in_proj(h, w_in, l, OFF_RWKV, RWKV_IN_W)
        proj_g = _in_proj(h, w_in, l, OFF_GATE, N_BRANCH * D_MODEL)

        y_na_p, nk, nv = _ctx_na(proj)
        y_gqa_p, gk, gv = _ctx_gqa(proj, gqa_q_norm[l], gqa_k_norm[l])
        y_na_s = _lat_na(proj, cache_na_k, cache_na_v, l, _na_bias(na_rpb[l]))
        y_gqa_s = _lat_gqa(proj, cache_gqa_k, cache_gqa_v, l, gqa_q_norm[l], gqa_k_norm[l], cos, sin)

        a, bb, cc = _s5_params(ssm_lam_re[l], ssm_lam_im[l], ssm_log_dt[l], ssm_b_re[l], ssm_b_im[l],
                               ssm_c_re[l], ssm_c_im[l])
        zero_h = jnp.zeros((BATCH, 2, 2, S5_SLABS, 128), F32)
        ys_p, ssm_state = _s5_scan(proj, 0, BATCH, SEQ // S5_TC, a, bb, cc, zero_h)
        ys_s, _ = _s5_scan(proj, T_PROMPT // S5_TC, DEC_BATCH, DEC_SEQ // S5_TC, a, bb, cc,
                           state_ssm[:, l].reshape(DEC_BATCH, 2, 2, S5_SLABS, 128))
        y_ssm = jnp.concatenate([_s5_glu(proj, 0, ys_p, ssm_d, ssm_w_glu, l),
                                 _s5_glu(proj, T_PROMPT, ys_s, ssm_d, ssm_w_glu, l)], axis=0)

        y_rwkv, rwkv_state = _rwkv_mixer(proj_r, _layer_params(rwkv_args, l), state_rwkv[:, l])

        ys = (jnp.concatenate([y_na_p, y_na_s], axis=0), jnp.concatenate([y_gqa_p, y_gqa_s], axis=0), y_ssm, y_rwkv)
        merged = _branch_merge(ys, proj_g, w_branch, l)
        x = _out_proj_residual(merged, w_out, l, x, mod, 2)
        x = _moe_residual(x, norm_ffn[l], mod, (w_router, router_bias), w_exp_gate, w_exp_up, w_exp_down, l)

        for lst, val in zip(caches, (nk, nv, gk, gv, ssm_state.reshape(BATCH, 2, 2, SSM_GROUPS, SSM_STATE),
                                     rwkv_state)):
            lst.append(val)

    y = _final_norm(x, norm_final)
    outs = [jnp.stack(lst, axis=1) for lst in caches]
    return (y[:T_PROMPT].reshape(BATCH, SEQ, D_MODEL), y[T_PROMPT:].reshape(DEC_BATCH, DEC_SEQ, D_MODEL), *outs)
```

```python
import functools

import numpy as np
import jax
import jax.numpy as jnp
from jax import lax
from jax.experimental import pallas as pl
from jax.experimental.pallas import tpu as pltpu

F32 = jnp.float32
BF16 = jnp.bfloat16

D_MODEL = 4096
BATCH = 16
SEQ = 256
DEPTH = 4
DEC_BATCH = 2
DEC_SEQ = 1024
PAST_LEN = 512
GRID_W = 64
HEAD_DIM = 128
EPS = 1e-6
NEG = -1e30
NA_HEADS = 8
NA_WIN_ROWS = 8
NA_WIN_COLS = 16
NA_W = NA_HEADS * HEAD_DIM
GQA_HEADS = 8
GQA_KV_HEADS = 2
GQA_GROUP = GQA_HEADS // GQA_KV_HEADS
GQA_Q_W = GQA_HEADS * HEAD_DIM
GQA_KV_W = GQA_KV_HEADS * HEAD_DIM
GQA_QKV_W = GQA_Q_W + 2 * GQA_KV_W
ROPE_THETA = 10000.0
SSM_WIDTH = 1024
SSM_GROUP = 16
SSM_GROUPS = SSM_WIDTH // SSM_GROUP
SSM_STATE = 64
RWKV_WIDTH = 1024
RWKV_HEAD = 64
RWKV_HEADS = RWKV_WIDTH // RWKV_HEAD
DECAY_RANK = 64
ICLR_RANK = 64
GATE_RANK = 128
RWKV_IN_W = 3 * RWKV_WIDTH + DECAY_RANK + ICLR_RANK + GATE_RANK
RWKV_GN_EPS = 64e-5
N_BRANCH = 4
OFF_NA = 0
OFF_GQA = 3 * NA_W
OFF_SSM = OFF_GQA + GQA_QKV_W
OFF_RWKV = OFF_SSM + SSM_WIDTH
OFF_GATE = OFF_RWKV + RWKV_IN_W
IN_W = OFF_GATE + N_BRANCH * D_MODEL
N_EXPERTS = 16
N_EXPERT_GROUPS = 4
EXPERTS_PER_GROUP = N_EXPERTS // N_EXPERT_GROUPS
EXPERT_TOPK = 2
EXPERT_FF = 1024

T_PROMPT = BATCH * SEQ
T_SAMPLE = DEC_BATCH * DEC_SEQ
T_ALL = T_PROMPT + T_SAMPLE
N_COND = 1 + DEC_BATCH
COND_PAD = 8
ATT_SCALE = HEAD_DIM ** -0.5

VMEM_LIMIT = 56 * 1024 * 1024


def _params(sem):
    return pltpu.CompilerParams(dimension_semantics=sem, vmem_limit_bytes=VMEM_LIMIT)


def _cond_group(i, rows_per_block):
    n_prompt_blocks = T_PROMPT // rows_per_block
    blocks_per_seq = DEC_SEQ // rows_per_block
    return jnp.where(i < n_prompt_blocks, 0, 1 + (i - n_prompt_blocks) // blocks_per_seq)


def _dot(a, b):
    return jnp.dot(a, b, preferred_element_type=F32)


def _dot_nt(a, b):
    return lax.dot_general(a, b, (((1,), (1,)), ((), ())), preferred_element_type=F32)


def _dot_tn(a, b):
    return lax.dot_general(a, b, (((0,), (0,)), ((), ())), preferred_element_type=F32)


def _split3(x):
    hi = x.astype(BF16)
    r1 = x - hi.astype(F32)
    mid = r1.astype(BF16)
    lo = (r1 - mid.astype(F32)).astype(BF16)
    return hi, mid, lo


def _dot_exact_rhs(a_bf16, x):
    hi, mid, lo = _split3(x)
    return _dot(a_bf16, hi) + _dot(a_bf16, mid) + _dot(a_bf16, lo)


def _dot_exact_lhs(x, b_bf16):
    hi, mid, lo = _split3(x)
    return _dot(hi, b_bf16) + _dot(mid, b_bf16) + _dot(lo, b_bf16)


def _rms(x, g):
    return x * lax.rsqrt(jnp.mean(x * x, axis=-1, keepdims=True) + EPS) * g


def _softmax_rows(s):
    m = jnp.max(s, axis=-1, keepdims=True)
    e = jnp.exp(s - m)
    return e / jnp.sum(e, axis=-1, keepdims=True)


def _ada_kernel(c_ref, w_ref, b_ref, o_ref):
    c = c_ref[...]
    s = (c * jax.nn.sigmoid(c)).astype(BF16)
    o_ref[0] = _dot(s, w_ref[0].astype(BF16)) + b_ref[0]


def _ada_all(cond, w_ada, b_ada):
    tn = 512
    n6 = 6 * D_MODEL
    return pl.pallas_call(
        _ada_kernel,
        grid=(DEPTH, n6 // tn),
        in_specs=[
            pl.BlockSpec((COND_PAD, D_MODEL), lambda l, j: (0, 0)),
            pl.BlockSpec((1, D_MODEL, tn), lambda l, j: (l, 0, j)),
            pl.BlockSpec((1, 1, tn), lambda l, j: (l, 0, j)),
        ],
        out_specs=pl.BlockSpec((1, COND_PAD, tn), lambda l, j: (l, 0, j)),
        out_shape=jax.ShapeDtypeStruct((DEPTH, COND_PAD, n6), F32),
        compiler_params=_params(("parallel", "parallel")),
        name="ada",
    )(cond, w_ada, b_ada.reshape(DEPTH, 1, n6))


NORM_ROWS = 256


def _norm_mod_kernel(x_ref, g_ref, mod_ref, o_ref, *, shift_idx):
    y = _rms(x_ref[...], g_ref[...])
    scale = mod_ref[0, shift_idx + 1:shift_idx + 2, :]
    shift = mod_ref[0, shift_idx:shift_idx + 1, :]
    o_ref[...] = (y * (1.0 + scale) + shift).astype(o_ref.dtype)


def _norm_mod(x, g, mod, shift_idx, out_dtype=BF16):
    t = x.shape[0]
    return pl.pallas_call(
        functools.partial(_norm_mod_kernel, shift_idx=shift_idx),
        grid=(t // NORM_ROWS,),
        in_specs=[
            pl.BlockSpec((NORM_ROWS, D_MODEL), lambda i: (i, 0)),
            pl.BlockSpec((1, D_MODEL), lambda i: (0, 0)),
            pl.BlockSpec((1, 6, D_MODEL), lambda i: (_cond_group(i, NORM_ROWS), 0, 0)),
        ],
        out_specs=pl.BlockSpec((NORM_ROWS, D_MODEL), lambda i: (i, 0)),
        out_shape=jax.ShapeDtypeStruct((t, D_MODEL), out_dtype),
        compiler_params=_params(("parallel",)),
        name="norm_mod",
    )(x, g.reshape(1, D_MODEL), mod)


def _final_norm_kernel(x_ref, g_ref, o_ref):
    o_ref[...] = _rms(x_ref[...], g_ref[...])


def _final_norm(x, g):
    t = x.shape[0]
    return pl.pallas_call(
        _final_norm_kernel,
        grid=(t // NORM_ROWS,),
        in_specs=[pl.BlockSpec((NORM_ROWS, D_MODEL), lambda i: (i, 0)),
                  pl.BlockSpec((1, D_MODEL), lambda i: (0, 0))],
        out_specs=pl.BlockSpec((NORM_ROWS, D_MODEL), lambda i: (i, 0)),
        out_shape=jax.ShapeDtypeStruct((t, D_MODEL), F32),
        compiler_params=_params(("parallel",)),
        name="final_norm",
    )(x, g.reshape(1, D_MODEL))


MM_TM = 1024
MM_TN = 256
IN_TM = 2048


def _proj_kernel(x_ref, w_ref, o_ref):
    o_ref[...] = _dot(x_ref[...], w_ref[0].astype(BF16))


def _in_proj(h, w_in, layer, col0, n_cols):
    t = h.shape[0]
    blk0 = col0 // MM_TN
    return pl.pallas_call(
        _proj_kernel,
        grid=(t // IN_TM, n_cols // MM_TN),
        in_specs=[
            pl.BlockSpec((IN_TM, D_MODEL), lambda i, j: (i, 0)),
            pl.BlockSpec((1, D_MODEL, MM_TN), lambda i, j: (layer, 0, blk0 + j)),
        ],
        out_specs=pl.BlockSpec((IN_TM, MM_TN), lambda i, j: (i, j)),
        out_shape=jax.ShapeDtypeStruct((t, n_cols), F32),
        compiler_params=_params(("parallel", "parallel")),
        name="in_proj",
    )(h, w_in)


def _merge_kernel(y0, y1, y2, y3, wb_ref, g0, g1, g2, g3, o_ref):
    acc = None
    for n, (y, g) in enumerate(((y0, g0), (y1, g1), (y2, g2), (y3, g3))):
        br = _dot(y[...], wb_ref[0, n].astype(BF16))
        term = jax.nn.sigmoid(g[...]) * br
        acc = term if acc is None else acc + term
    o_ref[...] = acc.astype(BF16)


def _branch_merge(ys, proj, w_branch, layer):
    t = proj.shape[0]
    bw = ys[0].shape[1]
    per_branch = D_MODEL // MM_TN
    y_spec = pl.BlockSpec((MM_TM, bw), lambda i, j: (i, 0))

    def gate_spec(n):
        return pl.BlockSpec((MM_TM, MM_TN), lambda i, j: (i, n * per_branch + j))

    return pl.pallas_call(
        _merge_kernel,
        grid=(t // MM_TM, D_MODEL // MM_TN),
        in_specs=[y_spec] * 4
        + [pl.BlockSpec((1, N_BRANCH, bw, MM_TN), lambda i, j: (layer, 0, 0, j))]
        + [gate_spec(n) for n in range(N_BRANCH)],
        out_specs=pl.BlockSpec((MM_TM, MM_TN), lambda i, j: (i, j)),
        out_shape=jax.ShapeDtypeStruct((t, D_MODEL), BF16),
        compiler_params=_params(("parallel", "parallel")),
        name="branch_merge",
    )(*ys, w_branch, proj, proj, proj, proj)


def _out_proj_kernel(m_ref, w_ref, x_ref, mod_ref, o_ref, *, gate_idx):
    out = _dot(m_ref[...], w_ref[0].astype(BF16))
    o_ref[...] = x_ref[...] + mod_ref[0, gate_idx:gate_idx + 1, :] * out


def _out_proj_residual(merged, w_out, layer, x, mod, gate_idx):
    t = x.shape[0]
    return pl.pallas_call(
        functools.partial(_out_proj_kernel, gate_idx=gate_idx),
        grid=(t // MM_TM, D_MODEL // MM_TN),
        in_specs=[
            pl.BlockSpec((MM_TM, D_MODEL), lambda i, j: (i, 0)),
            pl.BlockSpec((1, D_MODEL, MM_TN), lambda i, j: (layer, 0, j)),
            pl.BlockSpec((MM_TM, MM_TN), lambda i, j: (i, j)),
            pl.BlockSpec((1, 6, MM_TN), lambda i, j: (_cond_group(i, MM_TM), 0, j)),
        ],
        out_specs=pl.BlockSpec((MM_TM, MM_TN), lambda i, j: (i, j)),
        out_shape=jax.ShapeDtypeStruct((t, D_MODEL), F32),
        compiler_params=_params(("parallel", "parallel")),
        name="out_proj",
    )(merged, w_out, x, mod)


def _attend(q, k, v):
    p = _softmax_rows(_dot_nt(q, k) * ATT_SCALE)
    return _dot(p.astype(BF16), v)


def _ctx_na_kernel(q_ref, k_ref, v_ref, y_ref, ko_ref, vo_ref):
    for h in range(NA_HEADS):
        sl = slice(HEAD_DIM * h, HEAD_DIM * (h + 1))
        kf = k_ref[:, sl]
        vf = v_ref[:, sl]
        y = _attend(q_ref[:, sl].astype(BF16), kf.astype(BF16), vf.astype(BF16))
        y_ref[:, sl] = y.astype(BF16)
        ko_ref[0, h] = kf
        vo_ref[0, h] = vf


def _ctx_na(proj):
    cache = jax.ShapeDtypeStruct((BATCH, NA_HEADS, SEQ, HEAD_DIM), F32)
    cache_spec = pl.BlockSpec((1, NA_HEADS, SEQ, HEAD_DIM), lambda b: (b, 0, 0, 0))
    return pl.pallas_call(
        _ctx_na_kernel,
        grid=(BATCH,),
        in_specs=[pl.BlockSpec((SEQ, NA_W), lambda b, c=c: (b, c)) for c in range(3)],
        out_specs=[pl.BlockSpec((SEQ, NA_W), lambda b: (b, 0)), cache_spec, cache_spec],
        out_shape=[jax.ShapeDtypeStruct((T_PROMPT, NA_W), BF16), cache, cache],
        compiler_params=_params(("parallel",)),
        name="ctx_na",
    )(proj, proj, proj)


def _ctx_gqa_kernel(q_ref, k_ref, v_ref, gq_ref, gk_ref, y_ref, ko_ref, vo_ref):
    for kv in range(GQA_KV_HEADS):
        sl = slice(HEAD_DIM * kv, HEAD_DIM * (kv + 1))
        kn = _rms(k_ref[:, sl], gk_ref[...])
        vf = v_ref[:, sl]
        ko_ref[0, kv] = kn
        vo_ref[0, kv] = vf
        kb = kn.astype(BF16)
        vb = vf.astype(BF16)
        for g in range(GQA_GROUP):
            h = kv * GQA_GROUP + g
            hs = slice(HEAD_DIM * h, HEAD_DIM * (h + 1))
            qn = _rms(q_ref[:, hs], gq_ref[...])
            y_ref[:, hs] = _attend(qn.astype(BF16), kb, vb).astype(BF16)


def _ctx_gqa(proj, gq, gk):
    cache = jax.ShapeDtypeStruct((BATCH, GQA_KV_HEADS, SEQ, HEAD_DIM), F32)
    cache_spec = pl.BlockSpec((1, GQA_KV_HEADS, SEQ, HEAD_DIM), lambda b: (b, 0, 0, 0))
    norm_spec = pl.BlockSpec((1, HEAD_DIM), lambda b: (0, 0))
    return pl.pallas_call(
        _ctx_gqa_kernel,
        grid=(BATCH,),
        in_specs=[
            pl.BlockSpec((SEQ, GQA_Q_W), lambda b: (b, OFF_GQA // GQA_Q_W)),
            pl.BlockSpec((SEQ, GQA_KV_W), lambda b: (b, (OFF_GQA + GQA_Q_W) // GQA_KV_W)),
            pl.BlockSpec((SEQ, GQA_KV_W), lambda b: (b, (OFF_GQA + GQA_Q_W) // GQA_KV_W + 1)),
            norm_spec, norm_spec,
        ],
        out_specs=[pl.BlockSpec((SEQ, GQA_Q_W), lambda b: (b, 0)), cache_spec, cache_spec],
        out_shape=[jax.ShapeDtypeStruct((T_PROMPT, GQA_Q_W), BF16), cache, cache],
        compiler_params=_params(("parallel",)),
        name="ctx_gqa",
    )(proj, proj, proj, gq.reshape(1, HEAD_DIM), gk.reshape(1, HEAD_DIM))


LAT_Q_ROWS = 256
LAT_ROW0 = T_PROMPT // DEC_SEQ


def _rope_tables():
    t = np.arange(DEC_SEQ)
    half = HEAD_DIM // 2
    inv = jnp.asarray(ROPE_THETA, F32) ** (-jnp.arange(0, half, 2, dtype=F32) / half)
    rows = jnp.asarray(t // GRID_W, F32)
    cols = jnp.asarray(t % GRID_W, F32)
    ang_r = rows[:, None] * inv[None, :]
    ang_c = cols[:, None] * inv[None, :]
    cos = jnp.concatenate([jnp.cos(ang_r)] * 2 + [jnp.cos(ang_c)] * 2, axis=-1)
    sin = jnp.concatenate([-jnp.sin(ang_r), jnp.sin(ang_r), -jnp.sin(ang_c), jnp.sin(ang_c)], axis=-1)
    return cos, sin


def _rope(x, cos, sin):
    q = HEAD_DIM // 4
    lane = lax.broadcasted_iota(jnp.int32, x.shape, 1)
    partner = jnp.where((lane % (2 * q)) < q, pltpu.roll(x, HEAD_DIM - q, 1), pltpu.roll(x, q, 1))
    return x * cos + partner * sin


def _lat_gqa_kernel(q_ref, k_ref, v_ref, ck_ref, cv_ref, gq_ref, gk_ref, cos_ref, sin_ref, y_ref, kall, vall):
    cos = cos_ref[...]
    sin = sin_ref[...]
    kall[0:PAST_LEN, :] = ck_ref[0, 0, 0].astype(BF16)
    vall[0:PAST_LEN, :] = cv_ref[0, 0, 0].astype(BF16)
    kall[PAST_LEN:, :] = _rope(_rms(k_ref[...], gk_ref[...]), cos, sin).astype(BF16)
    vall[PAST_LEN:, :] = v_ref[...].astype(BF16)
    for g in range(GQA_GROUP):
        hs = slice(HEAD_DIM * g, HEAD_DIM * (g + 1))
        for qb in range(DEC_SEQ // LAT_Q_ROWS):
            rs = slice(LAT_Q_ROWS * qb, LAT_Q_ROWS * (qb + 1))
            qn = _rope(_rms(q_ref[rs, hs], gq_ref[...]), cos[rs], sin[rs])
            y_ref[rs, hs] = _attend(qn.astype(BF16), kall[...], vall[...]).astype(BF16)


def _lat_gqa(proj, cache_k, cache_v, layer, gq, gk, cos, sin):
    group_w = GQA_GROUP * HEAD_DIM
    kcol = (OFF_GQA + GQA_Q_W) // HEAD_DIM
    cache_spec = pl.BlockSpec((1, 1, 1, PAST_LEN, HEAD_DIM), lambda b, kv: (b, layer, kv, 0, 0))
    norm_spec = pl.BlockSpec((1, HEAD_DIM), lambda b, kv: (0, 0))
    tab_spec = pl.BlockSpec((DEC_SEQ, HEAD_DIM), lambda b, kv: (0, 0))
    return pl.pallas_call(
        _lat_gqa_kernel,
        grid=(DEC_BATCH, GQA_KV_HEADS),
        in_specs=[
            pl.BlockSpec((DEC_SEQ, group_w), lambda b, kv: (LAT_ROW0 + b, OFF_GQA // group_w + kv)),
            pl.BlockSpec((DEC_SEQ, HEAD_DIM), lambda b, kv: (LAT_ROW0 + b, kcol + kv)),
            pl.BlockSpec((DEC_SEQ, HEAD_DIM), lambda b, kv: (LAT_ROW0 + b, kcol + GQA_KV_HEADS + kv)),
            cache_spec, cache_spec, norm_spec, norm_spec, tab_spec, tab_spec,
        ],
        out_specs=pl.BlockSpec((DEC_SEQ, group_w), lambda b, kv: (b, kv)),
        out_shape=jax.ShapeDtypeStruct((T_SAMPLE, GQA_Q_W), BF16),
        scratch_shapes=[pltpu.VMEM((PAST_LEN + DEC_SEQ, HEAD_DIM), BF16)] * 2,
        compiler_params=_params(("parallel", "parallel")),
        name="lat_gqa",
    )(proj, proj, proj, cache_k, cache_v, gq.reshape(1, HEAD_DIM), gk.reshape(1, HEAD_DIM), cos, sin)


NA_ROWS = DEC_SEQ // GRID_W
NA_WR = min(NA_WIN_ROWS, NA_ROWS)
NA_BAND = NA_WR * GRID_W
NA_DR = 2 * NA_WIN_ROWS - 1
NA_DC = 2 * NA_WIN_COLS - 1


def _na_row_window(r):
    r0 = min(max(r - NA_WR // 2, 0), NA_ROWS - NA_WR)
    return r0, r0 - r + NA_WIN_ROWS - 1


def _na_table_kernel(rpb_ref, sel_ref, ok_ref, o_ref):
    picked = _dot_exact_lhs(rpb_ref[...], sel_ref[...])
    o_ref[...] = jnp.where(ok_ref[...] > 0.5, picked, NEG)


def _na_band_bias(rpb):
    c = np.arange(GRID_W)
    dc = c[None, :] - c[:, None] + NA_WIN_COLS - 1
    c0 = np.clip(c - NA_WIN_COLS // 2, 0, GRID_W - NA_WIN_COLS)
    ok = (c[None, :] >= c0[:, None]) & (c[None, :] < c0[:, None] + NA_WIN_COLS)
    sel = np.zeros((128, GRID_W * GRID_W), np.float32)
    flat_dc = np.clip(dc, 0, NA_DC - 1).reshape(-1)
    sel[flat_dc, np.arange(GRID_W * GRID_W)] = 1.0
    n = NA_HEADS * NA_DR
    rpb2 = jnp.pad(rpb.reshape(n, NA_DC).astype(F32), ((0, 0), (0, 128 - NA_DC)))
    full = lambda shape: pl.BlockSpec(shape, lambda: (0,) * len(shape))
    table = pl.pallas_call(
        _na_table_kernel,
        in_specs=[full((n, 128)), full((128, GRID_W * GRID_W)), full((1, GRID_W * GRID_W))],
        out_specs=full((n, GRID_W * GRID_W)),
        out_shape=jax.ShapeDtypeStruct((n, GRID_W * GRID_W), F32),
        name="na_table",
    )(rpb2, jnp.asarray(sel, BF16), jnp.asarray(ok.reshape(1, -1), F32))
    table = table.reshape(NA_HEADS, NA_DR, GRID_W, GRID_W)
    bands = [table[:, a0:a0 + NA_WR].transpose(0, 2, 1, 3).reshape(NA_HEADS, GRID_W, NA_BAND)
             for a0 in range(NA_DR - NA_WR + 1)]
    return jnp.stack(bands, axis=1)


def _lat_na_kernel(q_ref, k_ref, v_ref, ck_ref, cv_ref, bias_ref, y_ref):
    kb = k_ref[...].astype(BF16)
    vb = v_ref[...].astype(BF16)
    ck = ck_ref[0, 0, 0].astype(BF16)
    cv = cv_ref[0, 0, 0].astype(BF16)
    for r in range(NA_ROWS):
        r0, a0 = _na_row_window(r)
        rs = slice(GRID_W * r, GRID_W * (r + 1))
        ks = slice(GRID_W * r0, GRID_W * r0 + NA_BAND)
        q = q_ref[rs, :].astype(BF16)
        s_loc = _dot_nt(q, kb[ks]) * ATT_SCALE + bias_ref[0, a0]
        s_ctx = _dot_nt(q, ck) * ATT_SCALE
        m = jnp.maximum(jnp.max(s_loc, axis=-1, keepdims=True), jnp.max(s_ctx, axis=-1, keepdims=True))
        e_loc = jnp.exp(s_loc - m)
        e_ctx = jnp.exp(s_ctx - m)
        den = jnp.sum(e_loc, axis=-1, keepdims=True) + jnp.sum(e_ctx, axis=-1, keepdims=True)
        out = _dot((e_loc / den).astype(BF16), vb[ks]) + _dot((e_ctx / den).astype(BF16), cv)
        y_ref[rs, :] = out.astype(BF16)


def _lat_na(proj, cache_k, cache_v, layer, bias):
    cache_spec = pl.BlockSpec((1, 1, 1, PAST_LEN, HEAD_DIM), lambda h, b: (b, layer, h, 0, 0))

    def qkv_spec(c):
        return pl.BlockSpec((DEC_SEQ, HEAD_DIM), lambda h, b: (LAT_ROW0 + b, c * NA_HEADS + h))

    return pl.pallas_call(
        _lat_na_kernel,
        grid=(NA_HEADS, DEC_BATCH),
        in_specs=[qkv_spec(0), qkv_spec(1), qkv_spec(2), cache_spec, cache_spec,
                  pl.BlockSpec((1, NA_DR - NA_WR + 1, GRID_W, NA_BAND), lambda h, b: (h, 0, 0, 0))],
        out_specs=pl.BlockSpec((DEC_SEQ, HEAD_DIM), lambda h, b: (b, h)),
        out_shape=jax.ShapeDtypeStruct((T_SAMPLE, NA_W), BF16),
        compiler_params=_params(("parallel", "parallel")),
        name="lat_na",
    )(proj, proj, proj, cache_k, cache_v, bias)


S5_TC = 256
S5_PITCH = S5_TC + 8
S5_SLABS = SSM_GROUPS * SSM_STATE // 128
S5_QB = 4
S5_SLABS_Q = S5_SLABS // S5_QB


def _s5_disc_kernel(lr_ref, li_ref, dt_ref, br_ref, bi_ref, ar_ref, ai_ref, bbr_ref, bbi_ref):
    lr, li, dt = lr_ref[...], li_ref[...], dt_ref[...]
    mag = jnp.exp(lr * dt)
    ab_re = mag * jnp.cos(li * dt)
    ab_im = mag * jnp.sin(li * dt)
    den = lr * lr + li * li
    nr, ni = ab_re - 1.0, ab_im
    co_re = (nr * lr + ni * li) / den
    co_im = (ni * lr - nr * li) / den
    br, bi = br_ref[...], bi_ref[...]
    ar_ref[...] = ab_re
    ai_ref[...] = ab_im
    bbr_ref[...] = co_re * br - co_im * bi
    bbi_ref[...] = co_re * bi + co_im * br


def _s5_params(lam_re, lam_im, log_dt, b_re, b_im, c_re, c_im):
    g, p, c = SSM_GROUPS, SSM_STATE, SSM_GROUP
    shape = (2, g, c, p)
    n = 2 * g * c
    bc = lambda a: jnp.broadcast_to(a, shape).reshape(n, p)
    dt = jnp.exp(log_dt)
    args = (bc(lam_re[:, :, None, :]), bc(lam_im[:, :, None, :]), bc(dt[:, :, None, None]),
            jnp.transpose(b_re, (0, 1, 3, 2)).reshape(n, p), jnp.transpose(b_im, (0, 1, 3, 2)).reshape(n, p))
    spec = pl.BlockSpec((n, p), lambda: (0, 0))
    out = jax.ShapeDtypeStruct((n, p), F32)
    ab_re, ab_im, bb_re, bb_im = pl.pallas_call(
        _s5_disc_kernel, in_specs=[spec] * 5, out_specs=[spec] * 4, out_shape=[out] * 4, name="s5_disc")(*args)
    a = jnp.stack([ab_re.reshape(shape)[:, :, 0, :], ab_im.reshape(shape)[:, :, 0, :]], axis=1)
    a = a.reshape(2, 2, S5_SLABS, 128)
    eye = jnp.eye(16, dtype=F32)

    def blockdiag_in(bb):
        return jnp.einsum('dqgcp,gh->dqgchp', bb.reshape(2, S5_QB, 16, c, p), eye).reshape(2, S5_QB, 16 * c, 16 * p)

    def blockdiag_out(cc):
        return jnp.einsum('dqgcp,gh->dqgphc', cc.reshape(2, S5_QB, 16, c, p), eye).reshape(2, S5_QB, 16 * p, 16 * c)

    bb = jnp.concatenate([blockdiag_in(bb_re.reshape(shape)), blockdiag_in(bb_im.reshape(shape))], axis=-1)
    cc = jnp.stack([blockdiag_out(c_re), blockdiag_out(-c_im)], axis=2)
    return a, bb.astype(BF16), cc.astype(BF16)


def _s5_scan_kernel(u0, u1, u2, u3, bb_ref, cc_ref, a_ref, h0_ref, y_ref, hout_ref,
                    bur, bui, xsr, xsi, st, *, n_chunks):
    d = pl.program_id(1)
    c = pl.program_id(2)

    @pl.when(c == 0)
    def _():
        st[0] = h0_ref[0, 0, 0]
        st[1] = h0_ref[0, 0, 1]

    half = S5_SLABS_Q * 128
    for q, u in enumerate((u0, u1, u2, u3)):
        buq = _dot(u[...].astype(BF16), bb_ref[0, q])
        for j in range(S5_SLABS_Q):
            row0 = (q * S5_SLABS_Q + j) * S5_PITCH
            bur[row0:row0 + S5_TC, :] = buq[:, 128 * j:128 * (j + 1)]
            bui[row0:row0 + S5_TC, :] = buq[:, half + 128 * j:half + 128 * (j + 1)]

    ar = a_ref[0, 0]
    ai = a_ref[0, 1]

    def step(i, carry):
        xr, xi = carry
        t = jnp.where(d == 0, i, S5_TC - 1 - i)
        rows = pl.ds(t, S5_SLABS, stride=S5_PITCH)
        nxr = ar * xr - ai * xi + bur[rows, :]
        nxi = ar * xi + ai * xr + bui[rows, :]
        xsr[rows, :] = nxr
        xsi[rows, :] = nxi
        return nxr, nxi

    xr, xi = lax.fori_loop(0, S5_TC, step, (st[0], st[1]))
    st[0] = xr
    st[1] = xi

    @pl.when(c == n_chunks - 1)
    def _():
        hout_ref[0, 0, 0] = xr
        hout_ref[0, 0, 1] = xi

    for q in range(S5_QB):
        acc = jnp.zeros((S5_TC, 16 * SSM_GROUP), F32)
        for j in range(S5_SLABS_Q):
            row0 = (q * S5_SLABS_Q + j) * S5_PITCH
            ks = slice(128 * j, 128 * (j + 1))
            acc += _dot(xsr[row0:row0 + S5_TC, :].astype(BF16), cc_ref[0, q, 0, ks, :])
            acc += _dot(xsi[row0:row0 + S5_TC, :].astype(BF16), cc_ref[0, q, 1, ks, :])
        y_ref[0, :, 256 * q:256 * (q + 1)] = acc


def _s5_scan(proj, row_block0, n_seq, n_chunks, a, bb, cc, h0):
    ucol = OFF_SSM // 256

    def chunk(d, c):
        return jnp.where(d == 0, c, n_chunks - 1 - c)

    def u_spec(q):
        return pl.BlockSpec((S5_TC, 256), lambda s, d, c: (row_block0 + s * n_chunks + chunk(d, c), ucol + q))

    slab = pltpu.VMEM((S5_SLABS * S5_PITCH, 128), F32)
    rows = n_seq * n_chunks * S5_TC
    return pl.pallas_call(
        functools.partial(_s5_scan_kernel, n_chunks=n_chunks),
        grid=(n_seq, 2, n_chunks),
        in_specs=[u_spec(q) for q in range(S5_QB)] + [
            pl.BlockSpec((1, S5_QB, 256, 2 * S5_SLABS_Q * 128), lambda s, d, c: (d, 0, 0, 0)),
            pl.BlockSpec((1, S5_QB, 2, S5_SLABS_Q * 128, 256), lambda s, d, c: (d, 0, 0, 0, 0)),
            pl.BlockSpec((1, 2, S5_SLABS, 128), lambda s, d, c: (d, 0, 0, 0)),
            pl.BlockSpec((1, 1, 2, S5_SLABS, 128), lambda s, d, c: (s, d, 0, 0, 0)),
        ],
        out_specs=[
            pl.BlockSpec((1, S5_TC, SSM_WIDTH), lambda s, d, c: (d, s * n_chunks + chunk(d, c), 0)),
            pl.BlockSpec((1, 1, 2, S5_SLABS, 128), lambda s, d, c: (s, d, 0, 0, 0)),
        ],
        out_shape=[jax.ShapeDtypeStruct((2, rows, SSM_WIDTH), F32),
                   jax.ShapeDtypeStruct((n_seq, 2, 2, S5_SLABS, 128), F32)],
        scratch_shapes=[slab, slab, slab, slab, pltpu.VMEM((2, S5_SLABS, 128), F32)],
        compiler_params=_params(("parallel", "parallel", "arbitrary")),
        name="s5_scan",
    )(proj, proj, proj, proj, bb, cc, a, h0)


def _s5_glu_kernel(ua_ref, ub_ref, y0_ref, y1_ref, d_ref, w_ref, o_ref):
    u = jnp.concatenate([ua_ref[...], ub_ref[...]], axis=-1)
    y = jax.nn.gelu(u * d_ref[0] + y0_ref[0] + y1_ref[0])
    o_ref[...] = (y * jax.nn.sigmoid(_dot(y.astype(BF16), w_ref[0].astype(BF16)))).astype(BF16)


S5_GLU_ROWS = 512


def _s5_glu(proj, row0, y, ssm_d, w_glu, layer):
    rows = y.shape[1]
    tr = S5_GLU_ROWS
    blk0 = row0 // tr
    ucol = OFF_SSM // 512
    return pl.pallas_call(
        _s5_glu_kernel,
        grid=(rows // tr,),
        in_specs=[
            pl.BlockSpec((tr, 512), lambda i: (blk0 + i, ucol)),
            pl.BlockSpec((tr, 512), lambda i: (blk0 + i, ucol + 1)),
            pl.BlockSpec((1, tr, SSM_WIDTH), lambda i: (0, i, 0)),
            pl.BlockSpec((1, tr, SSM_WIDTH), lambda i: (1, i, 0)),
            pl.BlockSpec((1, 1, SSM_WIDTH), lambda i: (layer, 0, 0)),
            pl.BlockSpec((1, SSM_WIDTH, SSM_WIDTH), lambda i: (layer, 0, 0)),
        ],
        out_specs=pl.BlockSpec((tr, SSM_WIDTH), lambda i: (i, 0)),
        out_shape=jax.ShapeDtypeStruct((rows, SSM_WIDTH), BF16),
        compiler_params=_params(("parallel",)),
        name="s5_glu",
    )(proj, proj, y, y, ssm_d.reshape(DEPTH, 1, SSM_WIDTH), w_glu)


RW_ROWS = 256
RW_C = 64
RW_PAIRS = RWKV_HEADS // 2
RW_LOWRANK0 = 3 * RWKV_WIDTH
RW_PAIR_GROUP = 8


def _head_indicator():
    ind = (np.arange(RWKV_WIDTH)[:, None] // RWKV_HEAD == np.arange(128)[None, :]).astype(np.float32)
    return jnp.asarray(ind, BF16), jnp.asarray(ind.T, BF16)


def _head_sum(x, ind, ind_t):
    return _dot_exact_lhs(_dot_exact_lhs(x, ind), ind_t)


def _rwkv_prep_kernel(x_ref, prev_ref, next_ref, mu_ref, kk_ref, ka_ref, rk_ref, w0_ref, a0_ref, w2_ref, a2_ref,
                      g2_ref, ind_ref, indt_ref,
                      r_out, kkn_out, v_out, g_out, bonus_out, lw_out, kka_out, kd_out, *, seq_blocks):
    i = pl.program_id(0)
    n_prompt_blocks = T_PROMPT // RW_ROWS
    j = i - n_prompt_blocks
    in_seq = jnp.where(i < n_prompt_blocks, i % seq_blocks[0], j % seq_blocks[1])
    n_in_seq = jnp.where(i < n_prompt_blocks, seq_blocks[0], seq_blocks[1])
    first = in_seq == 0
    last = in_seq == n_in_seq - 1
    x = x_ref[...]
    row = lax.broadcasted_iota(jnp.int32, x.shape, 0)
    prev_row = jnp.where(first, 0.0, prev_ref[7:8, :])
    next_row = jnp.where(last, 0.0, next_ref[0:1, :])
    xp = jnp.where(row == 0, prev_row, pltpu.roll(x, 1, 0))
    xn = jnp.where(row == RW_ROWS - 1, next_row, pltpu.roll(x, RW_ROWS - 1, 0))
    z = x + mu_ref[...] * (0.5 * (xp + xn) - x)

    w = RWKV_WIDTH
    r, k, v = z[:, 0:w], z[:, w:2 * w], z[:, 2 * w:3 * w]
    low = z[:, RW_LOWRANK0:RW_LOWRANK0 + 128]
    gl = z[:, RW_LOWRANK0 + 128:RW_LOWRANK0 + 256]
    ind, ind_t = ind_ref[...], indt_ref[...]

    kk = k * kk_ref[...]
    kk = kk * lax.rsqrt(_head_sum(kk * kk, ind, ind_t) + 1e-12)
    r_out[...] = r
    kkn_out[...] = kk
    v_out[...] = v
    g_out[...] = _dot(jax.nn.sigmoid(gl).astype(BF16), g2_ref[...].astype(BF16))
    tanh_low = jnp.tanh(low).astype(BF16)
    low_b = low.astype(BF16)
    bonus = jnp.zeros_like(r)
    for d in range(2):
        w_log = -jax.nn.softplus(-(w0_ref[d:d + 1, :] + _dot(tanh_low, w2_ref[d].astype(BF16)))) - 0.5
        a = jax.nn.sigmoid(a0_ref[d:d + 1, :] + _dot(low_b, a2_ref[d].astype(BF16)))
        kd = k * (1.0 + (a - 1.0) * ka_ref[...])
        lw_out[d] = -jnp.exp(w_log)
        kka_out[d] = kk * a
        kd_out[d] = kd
        bonus = bonus + _head_sum(r * kd * rk_ref[...], ind, ind_t) * v
    bonus_out[...] = bonus


def _rwkv_prep(proj_r, lp):
    t = proj_r.shape[0]
    nb = t // RW_ROWS
    halo = RW_ROWS // 8
    w = RWKV_WIDTH
    zpad = jnp.zeros((2, 64, w), F32)
    w2 = jnp.concatenate([lp['rwkv_w2'], zpad], axis=1)
    a2 = jnp.concatenate([zpad, lp['rwkv_a2']], axis=1)
    ind, ind_t = _head_indicator()
    row = lambda a: a.reshape(1, -1)
    full = lambda shape: pl.BlockSpec(shape, lambda i: (0,) * len(shape))
    tok = pl.BlockSpec((RW_ROWS, w), lambda i: (i, 0))
    tok2 = pl.BlockSpec((2, RW_ROWS, w), lambda i: (0, i, 0))
    f1 = jax.ShapeDtypeStruct((t, w), F32)
    f2 = jax.ShapeDtypeStruct((2, t, w), F32)
    return pl.pallas_call(
        functools.partial(_rwkv_prep_kernel, seq_blocks=(SEQ // RW_ROWS, DEC_SEQ // RW_ROWS)),
        grid=(nb,),
        in_specs=[
            pl.BlockSpec((RW_ROWS, RWKV_IN_W), lambda i: (i, 0)),
            pl.BlockSpec((8, RWKV_IN_W), lambda i: (jnp.maximum(i * halo - 1, 0), 0)),
            pl.BlockSpec((8, RWKV_IN_W), lambda i: (jnp.minimum((i + 1) * halo, nb * halo - 1), 0)),
            full((1, RWKV_IN_W)), full((1, w)), full((1, w)), full((1, w)), full((2, w)), full((2, w)),
            full((2, 128, w)), full((2, 128, w)), full((GATE_RANK, w)), full((w, 128)), full((128, w)),
        ],
        out_specs=[tok, tok, tok, tok, tok, tok2, tok2, tok2],
        out_shape=[f1, f1, f1, f1, f1, f2, f2, f2],
        compiler_params=_params(("parallel",)),
        name="rwkv_prep",
    )(proj_r, proj_r, proj_r, row(lp['rwkv_mu']), row(lp['rwkv_k_k']), row(lp['rwkv_k_a']), row(lp['rwkv_r_k']),
      lp['rwkv_w0'], lp['rwkv_a0'], w2, a2, lp['rwkv_g2'], ind, ind_t)


def _rwkv_chunk_kernel(r_ref, kk_ref, v_ref, lw_ref, kka_ref, kd_ref, r2_out, y0_out, mneg_out, sadd_out, gc_out):
    d = pl.program_id(0)
    c = RW_C
    ri = lax.broadcasted_iota(jnp.int32, (2 * c, 2 * c), 0)
    ci = lax.broadcasted_iota(jnp.int32, (2 * c, 2 * c), 1)
    lower_half_rows = ri >= c
    same_head = lower_half_rows == (ci >= c)
    ti, tj = ri % c, ci % c
    before = (tj - ti) * (1 - 2 * d) < 0
    strict = same_head & before
    incl = same_head & (before | (ti == tj))
    eye = (ri == ci).astype(F32)
    tri_c = (incl[0:c, 0:c]).astype(BF16)
    lane = lax.broadcasted_iota(jnp.int32, (1, 2 * c), 1)
    m_a = (lane < RWKV_HEAD).astype(F32)
    m_b = 1.0 - m_a
    bf = lambda a: a.astype(BF16)

    def group(ps):
        each = lambda f, *xs: [f(*a) for a in zip(*xs)]
        sl = [slice(128 * p, 128 * (p + 1)) for p in ps]
        r, kk, v = ([ref[:, s] for s in sl] for ref in (r_ref, kk_ref, v_ref))
        lw, kka, kd = ([ref[0, :, s] for s in sl] for ref in (lw_ref, kka_ref, kd_ref))
        cum = each(lambda x: _dot_exact_rhs(tri_c, x), lw)
        tot = each(lambda x: jnp.where(d == 0, x[c - 1:c, :], x[0:1, :]), cum)
        g_rem = each(lambda t, x: jnp.exp(t - x), tot, cum)
        g_inv = each(lambda x: jnp.exp(-x), cum)
        qk = each(lambda k_, x, l: k_ * jnp.exp(x - l), kk, cum, lw)
        rt = each(lambda r_, x: r_ * jnp.exp(x), r, cum)
        lhs_q = each(lambda q: jnp.concatenate([q * m_a, q * m_b], axis=0), qk)
        lhs_r = each(lambda q: jnp.concatenate([q * m_a, q * m_b], axis=0), rt)
        rhs = each(lambda k_, a_, g: bf(jnp.concatenate([k_ * g, a_ * g], axis=0)), kd, kka, g_inv)
        x = each(lambda l, rr: _dot_nt(bf(l), rr), lhs_q, rhs)
        z = each(lambda l, rr: _dot_nt(bf(l), rr), lhs_r, rhs)
        xr = each(lambda t: pltpu.roll(t, c, 1), x)
        zr = each(lambda t: pltpu.roll(t, c, 1), z)
        qk_bd = each(lambda t, tr: bf(jnp.where(strict, jnp.where(lower_half_rows, tr, t), 0.0)), x, xr)
        qa_bd = each(lambda t, tr: jnp.where(strict, jnp.where(lower_half_rows, t, tr), 0.0), x, xr)
        rk_bd = each(lambda t, tr: bf(jnp.where(incl, jnp.where(lower_half_rows, tr, t), 0.0)), z, zr)
        ra_bd = each(lambda t, tr: bf(jnp.where(incl, jnp.where(lower_half_rows, t, tr), 0.0)), z, zr)
        inv = each(lambda n_: eye - n_, qa_bd)
        power = qa_bd
        for _ in range(5):
            pb = each(bf, power)
            power = each(_dot, pb, pb)
            inv = each(lambda i_, p_: i_ + _dot(bf(i_), bf(p_)), inv, power)
        v_st = each(lambda t: bf(jnp.concatenate([t * m_a, t * m_b], axis=0)), v)
        av_st = each(_dot, qk_bd, v_st)
        sol = each(lambda i_, q, av: _dot(bf(i_), jnp.concatenate([bf(q), bf(av)], axis=1)), inv, lhs_q, av_st)
        ra_sol = each(lambda a_, s_: _dot(a_, bf(s_)), ra_bd, sol)
        rk_v = each(_dot, rk_bd, v_st)
        fold = lambda t: t[0:c] + t[c:2 * c]
        for i, p in enumerate(ps):
            w_f = bf(fold(sol[i][:, 0:128]))
            u0_f = bf(fold(sol[i][:, 128:256]))
            ah_b = bf(kka[i] * g_rem[i])
            r2_out[0, :, sl[i]] = fold(lhs_r[i] - ra_sol[i][:, 0:128])
            y0_out[0, :, sl[i]] = fold(rk_v[i] - ra_sol[i][:, 128:256])
            mneg_out[0, 0, p] = jnp.where(same_head, _dot_tn(w_f, ah_b), 0.0)
            sadd_out[0, 0, p] = jnp.where(
                same_head, _dot_tn(bf(v[i]), bf(kd[i] * g_rem[i])) - _dot_tn(u0_f, ah_b), 0.0)
            gc_out[0, 0, :, sl[i]] = jnp.exp(tot[i])

    for p0 in range(0, RW_PAIRS, RW_PAIR_GROUP):
        group(list(range(p0, p0 + RW_PAIR_GROUP)))


def _rwkv_chunks(r, kk, v, lw, kka, kd):
    t = r.shape[0]
    nck = t // RW_C
    w = RWKV_WIDTH
    tok = pl.BlockSpec((RW_C, w), lambda d, i: (i, 0))
    tok2 = pl.BlockSpec((1, RW_C, w), lambda d, i: (d, i, 0))
    mat = pl.BlockSpec((1, 1, RW_PAIRS, 128, 128), lambda d, i: (d, i, 0, 0, 0))
    mat_shape = jax.ShapeDtypeStruct((2, nck, RW_PAIRS, 128, 128), F32)
    return pl.pallas_call(
        _rwkv_chunk_kernel,
        grid=(2, nck),
        in_specs=[tok, tok, tok, tok2, tok2, tok2],
        out_specs=[tok2, tok2, mat, mat, pl.BlockSpec((1, 1, 1, w), lambda d, i: (d, i, 0, 0))],
        out_shape=[jax.ShapeDtypeStruct((2, t, w), F32), jax.ShapeDtypeStruct((2, t, w), F32), mat_shape, mat_shape,
                   jax.ShapeDtypeStruct((2, nck, 1, w), F32)],
        compiler_params=_params(("parallel", "parallel")),
        name="rwkv_chunks",
    )(r, kk, v, lw, kka, kd)


RW_BLOCK_CHUNKS = 4


def _rwkv_state_kernel(r2_ref, y0_ref, mneg_ref, sadd_ref, gc_ref, s0_ref, y_out, s_out, st, *, n_blocks):
    d = pl.program_id(1)
    b = pl.program_id(2)

    @pl.when(b == 0)
    def _():
        st[...] = s0_ref[0, 0]

    def chunk(i, carry):
        ck = jnp.where(d == 0, i, RW_BLOCK_CHUNKS - 1 - i)
        rows = pl.ds(pl.multiple_of(ck * RW_C, RW_C), RW_C)
        for p in range(RW_PAIRS):
            ls = slice(128 * p, 128 * (p + 1))
            s = st[p]
            sb = s.astype(BF16)
            y_out[0, rows, ls] = _dot_nt(r2_ref[0, rows, ls].astype(BF16), sb) + y0_ref[0, rows, ls]
            st[p] = s * gc_ref[0, ck, :, ls] - _dot(sb, mneg_ref[0, ck, p].astype(BF16)) + sadd_ref[0, ck, p]
        return carry

    lax.fori_loop(0, RW_BLOCK_CHUNKS, chunk, 0)

    @pl.when(b == n_blocks - 1)
    def _():
        s_out[0, 0] = st[...]


def _rwkv_state(r2, y0, mneg, sadd, gc, s0, row0, n_seq, seq_len):
    rows = RW_BLOCK_CHUNKS * RW_C
    n_blocks = seq_len // rows
    blk0 = row0 // rows
    w = RWKV_WIDTH

    def blk(s, d, b):
        return blk0 + s * n_blocks + jnp.where(d == 0, b, n_blocks - 1 - b)

    tok = pl.BlockSpec((1, rows, w), lambda s, d, b: (d, blk(s, d, b), 0))
    mat = pl.BlockSpec((1, RW_BLOCK_CHUNKS, RW_PAIRS, 128, 128), lambda s, d, b: (d, blk(s, d, b), 0, 0, 0))
    state = pl.BlockSpec((1, 1, RW_PAIRS, 128, 128), lambda s, d, b: (s, d, 0, 0, 0))
    return pl.pallas_call(
        functools.partial(_rwkv_state_kernel, n_blocks=n_blocks),
        grid=(n_seq, 2, n_blocks),
        in_specs=[tok, tok, mat, mat,
                  pl.BlockSpec((1, RW_BLOCK_CHUNKS, 1, w), lambda s, d, b: (d, blk(s, d, b), 0, 0)), state],
        out_specs=[pl.BlockSpec((1, rows, w), lambda s, d, b: (d, blk(s, d, b) - blk0, 0)), state],
        out_shape=[jax.ShapeDtypeStruct((2, n_seq * seq_len, w), F32),
                   jax.ShapeDtypeStruct((n_seq, 2, RW_PAIRS, 128, 128), F32)],
        scratch_shapes=[pltpu.VMEM((RW_PAIRS, 128, 128), F32)],
        compiler_params=_params(("parallel", "parallel", "arbitrary")),
        name="rwkv_state",
    )(r2, y0, mneg, sadd, gc, s0)


def _pack_state(s):
    n = s.shape[0]
    sp = s.reshape(n, 2, RW_PAIRS, 2, RWKV_HEAD, RWKV_HEAD)
    z = jnp.zeros_like(sp[:, :, :, 0])
    top = jnp.concatenate([sp[:, :, :, 0], z], axis=-1)
    bot = jnp.concatenate([z, sp[:, :, :, 1]], axis=-1)
    return jnp.concatenate([top, bot], axis=-2)


def _unpack_state(sp):
    n = sp.shape[0]
    h = RWKV_HEAD
    return jnp.stack([sp[..., 0:h, 0:h], sp[..., h:, h:]], axis=3).reshape(n, 2, RWKV_HEADS, h, h)


def _rwkv_post_kernel(y_ref, bonus_ref, g_ref, lng_ref, lnb_ref, ind_ref, indt_ref, o_ref):
    ind, ind_t = ind_ref[...], indt_ref[...]
    y = y_ref[0] + y_ref[1]
    mu = _head_sum(y, ind, ind_t) * (1.0 / RWKV_HEAD)
    yc = y - mu
    var = _head_sum(yc * yc, ind, ind_t) * (1.0 / RWKV_HEAD)
    yn = yc * lax.rsqrt(var + RWKV_GN_EPS)
    o_ref[...] = ((yn * lng_ref[...] + lnb_ref[...] + bonus_ref[...]) * g_ref[...]).astype(BF16)


def _rwkv_post(y, bonus, g, ln_g, ln_b):
    t = bonus.shape[0]
    w = RWKV_WIDTH
    ind, ind_t = _head_indicator()
    tok = pl.BlockSpec((RW_ROWS, w), lambda i: (i, 0))
    full = lambda shape: pl.BlockSpec(shape, lambda i: (0,) * len(shape))
    return pl.pallas_call(
        _rwkv_post_kernel,
        grid=(t // RW_ROWS,),
        in_specs=[pl.BlockSpec((2, RW_ROWS, w), lambda i: (0, i, 0)), tok, tok, full((1, w)), full((1, w)),
                  full((w, 128)), full((128, w))],
        out_specs=tok,
        out_shape=jax.ShapeDtypeStruct((t, w), BF16),
        compiler_params=_params(("parallel",)),
        name="rwkv_post",
    )(y, bonus, g, ln_g.reshape(1, w), ln_b.reshape(1, w), ind, ind_t)


def _rwkv_mixer(proj_r, lp, s0_sample):
    r, kk, v, g, bonus, lw, kka, kd = _rwkv_prep(proj_r, lp)
    r2, y0, mneg, sadd, gc = _rwkv_chunks(r, kk, v, lw, kka, kd)
    zero = jnp.zeros((BATCH, 2, RW_PAIRS, 128, 128), F32)
    y_p, s_p = _rwkv_state(r2, y0, mneg, sadd, gc, zero, 0, BATCH, SEQ)
    y_s, _ = _rwkv_state(r2, y0, mneg, sadd, gc, _pack_state(s0_sample), T_PROMPT, DEC_BATCH, DEC_SEQ)
    y = jnp.concatenate([y_p, y_s], axis=1)
    return _rwkv_post(y, bonus, g, lp['rwkv_ln_g'], lp['rwkv_ln_b']), _unpack_state(s_p)


MOE_ROWS = 256
MOE_BLOCKS = T_ALL * EXPERT_TOPK // MOE_ROWS + N_EXPERTS
MOE_PAD_ROWS = MOE_BLOCKS * MOE_ROWS
MOE_FF_TILE = 512
MOE_OUT_TILE = 2048


def _top2_sum(vals):
    best = None
    for a in range(len(vals)):
        for b in range(a + 1, len(vals)):
            s = vals[a] + vals[b]
            best = s if best is None else jnp.maximum(best, s)
    return best


def _first_argmax(vals):
    idx = jnp.zeros(vals[0].shape, jnp.int32)
    best = vals[0]
    for j in range(1, len(vals)):
        upd = vals[j] > best
        idx = jnp.where(upd, j, idx)
        best = jnp.where(upd, vals[j], best)
    return idx, best


def _pick(idx, vals):
    out = vals[-1]
    for j in range(len(vals) - 2, -1, -1):
        out = jnp.where(idx == j, vals[j], out)
    return out


def _ffn_norm_route_kernel(x_ref, g_ref, mod_ref, wh_ref, wl_ref, bias_ref, h_ref, idx_ref, wts_ref):
    y = _rms(x_ref[...], g_ref[...])
    h = y * (1.0 + mod_ref[0, 4:5, :]) + mod_ref[0, 3:4, :]
    h_ref[...] = h
    hh = h.astype(BF16)
    hl = (h - hh.astype(F32)).astype(BF16)
    wh, wl = wh_ref[...], wl_ref[...]
    logits = _dot_nt(wh, hh) + _dot_nt(wh, hl) + _dot_nt(wl, hh)
    scores = jax.nn.sigmoid(logits)
    sel = scores + bias_ref[...]
    gs = EXPERTS_PER_GROUP
    sel_rows = [sel[e:e + 1, :] for e in range(N_EXPERTS)]
    sc_rows = [scores[e:e + 1, :] for e in range(N_EXPERTS)]
    grp, _ = _first_argmax([_top2_sum(sel_rows[gs * g:gs * (g + 1)]) for g in range(N_EXPERT_GROUPS)])
    v = [_pick(grp, [sel_rows[gs * g + j] for g in range(N_EXPERT_GROUPS)]) for j in range(gs)]
    s = [_pick(grp, [sc_rows[gs * g + j] for g in range(N_EXPERT_GROUPS)]) for j in range(gs)]
    i1, _ = _first_argmax(v)
    i2, _ = _first_argmax([jnp.where(i1 == j, -jnp.inf, v[j]) for j in range(gs)])
    w1, w2 = _pick(i1, s), _pick(i2, s)
    tot = w1 + w2
    idx_ref[0:1, :] = grp * gs + i1
    idx_ref[1:2, :] = grp * gs + i2
    wts_ref[0:1, :] = w1 / tot
    wts_ref[1:2, :] = w2 / tot


def _ffn_norm_route(x, g, mod, w_router, router_bias):
    t = x.shape[0]
    wt = w_router.T
    wh = wt.astype(BF16)
    wl = (wt - wh.astype(F32)).astype(BF16)
    full = lambda shape: pl.BlockSpec(shape, lambda i: (0,) * len(shape))
    return pl.pallas_call(
        _ffn_norm_route_kernel,
        grid=(t // NORM_ROWS,),
        in_specs=[
            pl.BlockSpec((NORM_ROWS, D_MODEL), lambda i: (i, 0)),
            full((1, D_MODEL)),
            pl.BlockSpec((1, 6, D_MODEL), lambda i: (_cond_group(i, NORM_ROWS), 0, 0)),
            full((N_EXPERTS, D_MODEL)), full((N_EXPERTS, D_MODEL)), full((N_EXPERTS, 1)),
        ],
        out_specs=[pl.BlockSpec((NORM_ROWS, D_MODEL), lambda i: (i, 0)),
                   pl.BlockSpec((EXPERT_TOPK, NORM_ROWS), lambda i: (0, i)),
                   pl.BlockSpec((EXPERT_TOPK, NORM_ROWS), lambda i: (0, i))],
        out_shape=[jax.ShapeDtypeStruct((t, D_MODEL), F32),
                   jax.ShapeDtypeStruct((EXPERT_TOPK, t), jnp.int32),
                   jax.ShapeDtypeStruct((EXPERT_TOPK, t), F32)],
        compiler_params=_params(("parallel",)),
        name="ffn_norm_route",
    )(x, g.reshape(1, D_MODEL), mod, wh, wl, router_bias.reshape(N_EXPERTS, 1))


def _dispatch_plan(idx):
    t = idx.shape[1]
    flat_e = idx.T.reshape(-1)
    onehot = (flat_e[:, None] == jnp.arange(N_EXPERTS, dtype=jnp.int32)[None, :]).astype(jnp.int32)
    csum = jnp.cumsum(onehot, axis=0)
    rank = jnp.take_along_axis(csum, flat_e[:, None], axis=1)[:, 0] - 1
    counts = csum[-1]
    padded = ((counts + MOE_ROWS - 1) // MOE_ROWS) * MOE_ROWS
    pad_end = jnp.cumsum(padded)
    pad_start = pad_end - padded
    dest = (pad_start[flat_e] + rank).astype(jnp.int32)
    tok = jnp.arange(t * EXPERT_TOPK, dtype=jnp.int32) // EXPERT_TOPK
    row_tok = jnp.zeros((MOE_PAD_ROWS,), jnp.int32).at[dest].set(tok)
    block_row0 = jnp.arange(MOE_BLOCKS, dtype=jnp.int32) * MOE_ROWS
    block_e = jnp.minimum(jnp.sum((pad_end[None, :] <= block_row0[:, None]).astype(jnp.int32), axis=1),
                          N_EXPERTS - 1).astype(jnp.int32)
    n_used = (pad_end[-1] // MOE_ROWS).astype(jnp.int32).reshape(1)
    return row_tok, block_e, n_used, dest


DMA_UNROLL = 8


def _row_copy(src_hbm, row, dst, slot, sem):
    return pltpu.make_async_copy(src_hbm.at[pl.ds(row, 1), :], dst.at[pl.ds(slot, 1), :], sem)


def _start_rows(src_hbm, idx_ref, idx0, stride, n, dst, sem):
    def body(r, c):
        _row_copy(src_hbm, idx_ref[idx0 + stride * r], dst, r, sem).start()
        return c
    lax.fori_loop(0, n, body, 0, unroll=DMA_UNROLL)


def _wait_rows(src_hbm, n, dst, sem):
    def body(r, c):
        _row_copy(src_hbm, 0, dst, r, sem).wait()
        return c
    lax.fori_loop(0, n, body, 0, unroll=DMA_UNROLL)


def _gather_rows_kernel(tok_ref, nu_ref, h_hbm, o_ref, buf, sem):
    i = pl.program_id(0)
    n = nu_ref[0]

    @pl.when((i == 0) & (n > 0))
    def _():
        _start_rows(h_hbm, tok_ref, 0, 1, MOE_ROWS, buf.at[0], sem.at[0])

    @pl.when(i + 1 < n)
    def _():
        nxt = (i + 1) % 2
        _start_rows(h_hbm, tok_ref, (i + 1) * MOE_ROWS, 1, MOE_ROWS, buf.at[nxt], sem.at[nxt])

    @pl.when(i < n)
    def _():
        cur = i % 2
        _wait_rows(h_hbm, MOE_ROWS, buf.at[cur], sem.at[cur])
        o_ref[...] = buf[cur].astype(BF16)

    @pl.when(i >= n)
    def _():
        o_ref[...] = jnp.zeros_like(o_ref)


def _gather_rows(h, row_tok, n_used):
    return pl.pallas_call(
        _gather_rows_kernel,
        grid_spec=pltpu.PrefetchScalarGridSpec(
            num_scalar_prefetch=2,
            grid=(MOE_BLOCKS,),
            in_specs=[pl.BlockSpec(memory_space=pl.ANY)],
            out_specs=pl.BlockSpec((MOE_ROWS, D_MODEL), lambda i, tok, nu: (i, 0)),
            scratch_shapes=[pltpu.VMEM((2, MOE_ROWS, D_MODEL), F32), pltpu.SemaphoreType.DMA((2,))],
        ),
        out_shape=jax.ShapeDtypeStruct((MOE_PAD_ROWS, D_MODEL), BF16),
        compiler_params=_params(("arbitrary",)),
        name="moe_gather",
    )(row_tok, n_used, h)


def _expert_up_kernel(be_ref, nu_ref, x_ref, w1_ref, w3_ref, o_ref):
    i = pl.program_id(1)

    @pl.when(i < nu_ref[0])
    def _():
        x = x_ref[...]
        a = _dot(x, w1_ref[0, 0].astype(BF16))
        b = _dot(x, w3_ref[0, 0].astype(BF16))
        o_ref[...] = (a * jax.nn.sigmoid(a) * b).astype(BF16)

    @pl.when(i >= nu_ref[0])
    def _():
        o_ref[...] = jnp.zeros_like(o_ref)


def _expert_up(xb, w_gate, w_up, layer, block_e, n_used):
    wspec = pl.BlockSpec((1, 1, D_MODEL, MOE_FF_TILE), lambda j, i, be, nu: (layer, be[i], 0, j))
    return pl.pallas_call(
        _expert_up_kernel,
        grid_spec=pltpu.PrefetchScalarGridSpec(
            num_scalar_prefetch=2,
            grid=(EXPERT_FF // MOE_FF_TILE, MOE_BLOCKS),
            in_specs=[pl.BlockSpec((MOE_ROWS, D_MODEL), lambda j, i, be, nu: (i, 0)), wspec, wspec],
            out_specs=pl.BlockSpec((MOE_ROWS, MOE_FF_TILE), lambda j, i, be, nu: (i, j)),
        ),
        out_shape=jax.ShapeDtypeStruct((MOE_PAD_ROWS, EXPERT_FF), BF16),
        compiler_params=_params(("parallel", "arbitrary")),
        name="moe_up",
    )(block_e, n_used, xb, w_gate, w_up)


def _expert_down_kernel(be_ref, nu_ref, h_ref, w2_ref, o_ref):
    i = pl.program_id(1)

    @pl.when(i < nu_ref[0])
    def _():
        o_ref[...] = _dot(h_ref[...], w2_ref[0, 0].astype(BF16))

    @pl.when(i >= nu_ref[0])
    def _():
        o_ref[...] = jnp.zeros_like(o_ref)


def _expert_down(hmid, w_down, layer, block_e, n_used):
    return pl.pallas_call(
        _expert_down_kernel,
        grid_spec=pltpu.PrefetchScalarGridSpec(
            num_scalar_prefetch=2,
            grid=(D_MODEL // MOE_OUT_TILE, MOE_BLOCKS),
            in_specs=[
                pl.BlockSpec((MOE_ROWS, EXPERT_FF), lambda j, i, be, nu: (i, 0)),
                pl.BlockSpec((1, 1, EXPERT_FF, MOE_OUT_TILE), lambda j, i, be, nu: (layer, be[i], 0, j)),
            ],
            out_specs=pl.BlockSpec((MOE_ROWS, MOE_OUT_TILE), lambda j, i, be, nu: (i, j)),
        ),
        out_shape=jax.ShapeDtypeStruct((MOE_PAD_ROWS, D_MODEL), F32),
        compiler_params=_params(("parallel", "arbitrary")),
        name="moe_down",
    )(block_e, n_used, hmid, w_down)


COMBINE_ROWS = 256


def _combine_kernel(pos_ref, yb_hbm, x_ref, w_ref, mod_ref, o_ref, buf, sem):
    i = pl.program_id(0)
    n = pl.num_programs(0)
    per_block = COMBINE_ROWS * EXPERT_TOPK

    def start(blk, slot):
        for k in range(EXPERT_TOPK):
            _start_rows(yb_hbm, pos_ref, blk * per_block + k, EXPERT_TOPK, COMBINE_ROWS, buf.at[slot, k],
                        sem.at[slot])

    @pl.when(i == 0)
    def _():
        start(0, 0)

    @pl.when(i + 1 < n)
    def _():
        start(i + 1, (i + 1) % 2)

    cur = i % 2
    for k in range(EXPERT_TOPK):
        _wait_rows(yb_hbm, COMBINE_ROWS, buf.at[cur, k], sem.at[cur])
    y = buf[cur, 0] * w_ref[:, 0:1] + buf[cur, 1] * w_ref[:, 1:2]
    o_ref[...] = x_ref[...] + mod_ref[0, 5:6, :] * y


def _combine(yb, dest, wts, x, mod):
    t = x.shape[0]
    return pl.pallas_call(
        _combine_kernel,
        grid_spec=pltpu.PrefetchScalarGridSpec(
            num_scalar_prefetch=1,
            grid=(t // COMBINE_ROWS,),
            in_specs=[
                pl.BlockSpec(memory_space=pl.ANY),
                pl.BlockSpec((COMBINE_ROWS, D_MODEL), lambda i, pos: (i, 0)),
                pl.BlockSpec((COMBINE_ROWS, EXPERT_TOPK), lambda i, pos: (i, 0)),
                pl.BlockSpec((1, 6, D_MODEL), lambda i, pos: (_cond_group(i, COMBINE_ROWS), 0, 0)),
            ],
            out_specs=pl.BlockSpec((COMBINE_ROWS, D_MODEL), lambda i, pos: (i, 0)),
            scratch_shapes=[pltpu.VMEM((2, EXPERT_TOPK, COMBINE_ROWS, D_MODEL), F32),
                            pltpu.SemaphoreType.DMA((2,))],
        ),
        out_shape=jax.ShapeDtypeStruct((t, D_MODEL), F32),
        compiler_params=_params(("arbitrary",)),
        name="moe_combine",
    )(dest, yb, x, wts.T, mod)


def _moe_residual(x, g, mod, lp_router, w_gate, w_up, w_down, layer):
    h, idx, wts = _ffn_norm_route(x, g, mod, *lp_router)
    row_tok, block_e, n_used, dest = _dispatch_plan(idx)
    xb = _gather_rows(h, row_tok, n_used)
    hmid = _expert_up(xb, w_gate, w_up, layer, block_e, n_used)
    yb = _expert_down(hmid, w_down, layer, block_e, n_used)
    return _combine(yb, dest, wts, x, mod)


def _layer_params(args, layer):
    return {k: v[layer] for k, v in args.items()}


def kernel(x_prompt, x_sample, cache_na_k, cache_na_v, cache_gqa_k, cache_gqa_v, state_ssm, state_rwkv, c, c_ctx,
           w_ada, b_ada, norm_mix, norm_ffn, w_in, na_rpb, gqa_q_norm, gqa_k_norm, ssm_lam_re, ssm_lam_im,
           ssm_log_dt, ssm_b_re, ssm_b_im, ssm_c_re, ssm_c_im, ssm_d, ssm_w_glu, rwkv_mu, rwkv_w0, rwkv_w2, rwkv_a0,
           rwkv_a2, rwkv_k_k, rwkv_k_a, rwkv_r_k, rwkv_g2, rwkv_ln_g, rwkv_ln_b, w_branch, w_out, w_router,
           router_bias, w_exp_gate, w_exp_up, w_exp_down, norm_final):
    x = jnp.concatenate([x_prompt.reshape(T_PROMPT, D_MODEL), x_sample.reshape(T_SAMPLE, D_MODEL)], axis=0)
    cond = jnp.concatenate([c_ctx[None, :], c, jnp.zeros((COND_PAD - N_COND, D_MODEL), F32)], axis=0)
    mod_all = _ada_all(cond, w_ada, b_ada).reshape(DEPTH, COND_PAD, 6, D_MODEL)
    cos, sin = _rope_tables()
    rwkv_args = dict(rwkv_mu=rwkv_mu, rwkv_w0=rwkv_w0, rwkv_w2=rwkv_w2, rwkv_a0=rwkv_a0, rwkv_a2=rwkv_a2,
                     rwkv_k_k=rwkv_k_k, rwkv_k_a=rwkv_k_a, rwkv_r_k=rwkv_r_k.reshape(DEPTH, RWKV_WIDTH),
                     rwkv_g2=rwkv_g2, rwkv_ln_g=rwkv_ln_g, rwkv_ln_b=rwkv_ln_b)
    caches = [[] for _ in range(6)]
    for l in range(DEPTH):
        mod = mod_all[l]
        h = _norm_mod(x, norm_mix[l], mod, 0)
        proj = _in_proj(h, w_in, l, 0, OFF_RWKV)
        proj_r = _in_proj(h, w_in, l, OFF_RWKV, RWKV_IN_W)
        proj_g = _in_proj(h, w_in, l, OFF_GATE, N_BRANCH * D_MODEL)

        y_na_p, nk, nv = _ctx_na(proj)
        y_gqa_p, gk, gv = _ctx_gqa(proj, gqa_q_norm[l], gqa_k_norm[l])
        y_na_s = _lat_na(proj, cache_na_k, cache_na_v, l, _na_band_bias(na_rpb[l]))
        y_gqa_s = _lat_gqa(proj, cache_gqa_k, cache_gqa_v, l, gqa_q_norm[l], gqa_k_norm[l], cos, sin)

        a, bb, cc = _s5_params(ssm_lam_re[l], ssm_lam_im[l], ssm_log_dt[l], ssm_b_re[l], ssm_b_im[l],
                               ssm_c_re[l], ssm_c_im[l])
        zero_h = jnp.zeros((BATCH, 2, 2, S5_SLABS, 128), F32)
        ys_p, ssm_state = _s5_scan(proj, 0, BATCH, SEQ // S5_TC, a, bb, cc, zero_h)
        ys_s, _ = _s5_scan(proj, T_PROMPT // S5_TC, DEC_BATCH, DEC_SEQ // S5_TC, a, bb, cc,
                           state_ssm[:, l].reshape(DEC_BATCH, 2, 2, S5_SLABS, 128))
        y_ssm = jnp.concatenate([_s5_glu(proj, 0, ys_p, ssm_d, ssm_w_glu, l),
                                 _s5_glu(proj, T_PROMPT, ys_s, ssm_d, ssm_w_glu, l)], axis=0)

        y_rwkv, rwkv_state = _rwkv_mixer(proj_r, _layer_params(rwkv_args, l), state_rwkv[:, l])

        ys = (jnp.concatenate([y_na_p, y_na_s], axis=0), jnp.concatenate([y_gqa_p, y_gqa_s], axis=0), y_ssm, y_rwkv)
        merged = _branch_merge(ys, proj_g, w_branch, l)
        x = _out_proj_residual(merged, w_out, l, x, mod, 2)
        x = _moe_residual(x, norm_ffn[l], mod, (w_router, router_bias), w_exp_gate, w_exp_up, w_exp_down, l)

        for lst, val in zip(caches, (nk, nv, gk, gv, ssm_state.reshape(BATCH, 2, 2, SSM_GROUPS, SSM_STATE),
                                     rwkv_state)):
            lst.append(val)

    y = _final_norm(x, norm_final)
    outs = [jnp.stack(lst, axis=1) for lst in caches]
    return (y[:T_PROMPT].reshape(BATCH, SEQ, D_MODEL), y[T_PROMPT:].reshape(DEC_BATCH, DEC_SEQ, D_MODEL), *outs)
```

```python
import functools

import numpy as np
import jax
import jax.numpy as jnp
from jax import lax
from jax.experimental import pallas as pl
from jax.experimental.pallas import tpu as pltpu

F32 = jnp.float32
BF16 = jnp.bfloat16

D_MODEL = 4096
BATCH = 16
SEQ = 256
DEPTH = 4
DEC_BATCH = 2
DEC_SEQ = 1024
PAST_LEN = 512
GRID_W = 64
HEAD_DIM = 128
EPS = 1e-6
NEG = -1e30
NA_HEADS = 8
NA_WIN_ROWS = 8
NA_WIN_COLS = 16
NA_W = NA_HEADS * HEAD_DIM
GQA_HEADS = 8
GQA_KV_HEADS = 2
GQA_GROUP = GQA_HEADS // GQA_KV_HEADS
GQA_Q_W = GQA_HEADS * HEAD_DIM
GQA_KV_W = GQA_KV_HEADS * HEAD_DIM
GQA_QKV_W = GQA_Q_W + 2 * GQA_KV_W
ROPE_THETA = 10000.0
SSM_WIDTH = 1024
SSM_GROUP = 16
SSM_GROUPS = SSM_WIDTH // SSM_GROUP
SSM_STATE = 64
RWKV_WIDTH = 1024
RWKV_HEAD = 64
RWKV_HEADS = RWKV_WIDTH // RWKV_HEAD
DECAY_RANK = 64
ICLR_RANK = 64
GATE_RANK = 128
RWKV_IN_W = 3 * RWKV_WIDTH + DECAY_RANK + ICLR_RANK + GATE_RANK
RWKV_GN_EPS = 64e-5
N_BRANCH = 4
OFF_NA = 0
OFF_GQA = 3 * NA_W
OFF_SSM = OFF_GQA + GQA_QKV_W
OFF_RWKV = OFF_SSM + SSM_WIDTH
OFF_GATE = OFF_RWKV + RWKV_IN_W
IN_W = OFF_GATE + N_BRANCH * D_MODEL
N_EXPERTS = 16
N_EXPERT_GROUPS = 4
EXPERTS_PER_GROUP = N_EXPERTS // N_EXPERT_GROUPS
EXPERT_TOPK = 2
EXPERT_FF = 1024

T_PROMPT = BATCH * SEQ
T_SAMPLE = DEC_BATCH * DEC_SEQ
T_ALL = T_PROMPT + T_SAMPLE
N_COND = 1 + DEC_BATCH
COND_PAD = 8
ATT_SCALE = HEAD_DIM ** -0.5

VMEM_LIMIT = 56 * 1024 * 1024


def _params(sem):
    return pltpu.CompilerParams(dimension_semantics=sem, vmem_limit_bytes=VMEM_LIMIT)


def _cond_group(i, rows_per_block):
    n_prompt_blocks = T_PROMPT // rows_per_block
    blocks_per_seq = DEC_SEQ // rows_per_block
    return jnp.where(i < n_prompt_blocks, 0, 1 + (i - n_prompt_blocks) // blocks_per_seq)


def _dot(a, b):
    return jnp.dot(a, b, preferred_element_type=F32)


def _dot_nt(a, b):
    return lax.dot_general(a, b, (((1,), (1,)), ((), ())), preferred_element_type=F32)


def _dot_tn(a, b):
    return lax.dot_general(a, b, (((0,), (0,)), ((), ())), preferred_element_type=F32)


def _split3(x):
    hi = x.astype(BF16)
    r1 = x - hi.astype(F32)
    mid = r1.astype(BF16)
    lo = (r1 - mid.astype(F32)).astype(BF16)
    return hi, mid, lo


def _dot_exact_rhs(a_bf16, x):
    hi, mid, lo = _split3(x)
    return _dot(a_bf16, hi) + _dot(a_bf16, mid) + _dot(a_bf16, lo)


def _dot_exact_lhs(x, b_bf16):
    hi, mid, lo = _split3(x)
    return _dot(hi, b_bf16) + _dot(mid, b_bf16) + _dot(lo, b_bf16)


def _rms(x, g):
    return x * lax.rsqrt(jnp.mean(x * x, axis=-1, keepdims=True) + EPS) * g


def _softmax_rows(s):
    m = jnp.max(s, axis=-1, keepdims=True)
    e = jnp.exp(s - m)
    return e / jnp.sum(e, axis=-1, keepdims=True)


def _ada_kernel(c_ref, w_ref, b_ref, o_ref):
    c = c_ref[...]
    s = (c * jax.nn.sigmoid(c)).astype(BF16)
    o_ref[0] = _dot(s, w_ref[0].astype(BF16)) + b_ref[0]


def _ada_all(cond, w_ada, b_ada):
    tn = 512
    n6 = 6 * D_MODEL
    return pl.pallas_call(
        _ada_kernel,
        grid=(DEPTH, n6 // tn),
        in_specs=[
            pl.BlockSpec((COND_PAD, D_MODEL), lambda l, j: (0, 0)),
            pl.BlockSpec((1, D_MODEL, tn), lambda l, j: (l, 0, j)),
            pl.BlockSpec((1, 1, tn), lambda l, j: (l, 0, j)),
        ],
        out_specs=pl.BlockSpec((1, COND_PAD, tn), lambda l, j: (l, 0, j)),
        out_shape=jax.ShapeDtypeStruct((DEPTH, COND_PAD, n6), F32),
        compiler_params=_params(("parallel", "parallel")),
        name="ada",
    )(cond, w_ada, b_ada.reshape(DEPTH, 1, n6))


NORM_ROWS = 256


def _norm_mod_kernel(x_ref, g_ref, mod_ref, o_ref, *, shift_idx):
    y = _rms(x_ref[...], g_ref[...])
    scale = mod_ref[0, shift_idx + 1:shift_idx + 2, :]
    shift = mod_ref[0, shift_idx:shift_idx + 1, :]
    o_ref[...] = (y * (1.0 + scale) + shift).astype(o_ref.dtype)


def _norm_mod(x, g, mod, shift_idx, out_dtype=BF16):
    t = x.shape[0]
    return pl.pallas_call(
        functools.partial(_norm_mod_kernel, shift_idx=shift_idx),
        grid=(t // NORM_ROWS,),
        in_specs=[
            pl.BlockSpec((NORM_ROWS, D_MODEL), lambda i: (i, 0)),
            pl.BlockSpec((1, D_MODEL), lambda i: (0, 0)),
            pl.BlockSpec((1, 6, D_MODEL), lambda i: (_cond_group(i, NORM_ROWS), 0, 0)),
        ],
        out_specs=pl.BlockSpec((NORM_ROWS, D_MODEL), lambda i: (i, 0)),
        out_shape=jax.ShapeDtypeStruct((t, D_MODEL), out_dtype),
        compiler_params=_params(("parallel",)),
        name="norm_mod",
    )(x, g.reshape(1, D_MODEL), mod)


def _final_norm_kernel(x_ref, g_ref, o_ref):
    o_ref[...] = _rms(x_ref[...], g_ref[...])


def _final_norm(x, g):
    t = x.shape[0]
    return pl.pallas_call(
        _final_norm_kernel,
        grid=(t // NORM_ROWS,),
        in_specs=[pl.BlockSpec((NORM_ROWS, D_MODEL), lambda i: (i, 0)),
                  pl.BlockSpec((1, D_MODEL), lambda i: (0, 0))],
        out_specs=pl.BlockSpec((NORM_ROWS, D_MODEL), lambda i: (i, 0)),
        out_shape=jax.ShapeDtypeStruct((t, D_MODEL), F32),
        compiler_params=_params(("parallel",)),
        name="final_norm",
    )(x, g.reshape(1, D_MODEL))


MM_TM = 1024
MM_TN = 256
IN_TM = 2048


def _proj_kernel(x_ref, w_ref, o_ref):
    o_ref[...] = _dot(x_ref[...], w_ref[0].astype(BF16))


def _in_proj(h, w_in, layer, col0, n_cols):
    t = h.shape[0]
    blk0 = col0 // MM_TN
    return pl.pallas_call(
        _proj_kernel,
        grid=(t // IN_TM, n_cols // MM_TN),
        in_specs=[
            pl.BlockSpec((IN_TM, D_MODEL), lambda i, j: (i, 0)),
            pl.BlockSpec((1, D_MODEL, MM_TN), lambda i, j: (layer, 0, blk0 + j)),
        ],
        out_specs=pl.BlockSpec((IN_TM, MM_TN), lambda i, j: (i, j)),
        out_shape=jax.ShapeDtypeStruct((t, n_cols), F32),
        compiler_params=_params(("parallel", "parallel")),
        name="in_proj",
    )(h, w_in)


def _merge_kernel(y0p, y0s, y1p, y1s, y2, y3, wb_ref, g0, g1, g2, g3, o_ref):
    in_prompt = pl.program_id(0) < T_PROMPT // MM_TM
    ys = (jnp.where(in_prompt, y0p[...], y0s[...]), jnp.where(in_prompt, y1p[...], y1s[...]), y2[...], y3[...])
    acc = None
    for n, (y, g) in enumerate(zip(ys, (g0, g1, g2, g3))):
        br = _dot(y, wb_ref[0, n].astype(BF16))
        term = jax.nn.sigmoid(g[...]) * br
        acc = term if acc is None else acc + term
    o_ref[...] = acc.astype(BF16)


def _branch_merge(y_na, y_gqa, y_ssm, y_rwkv, proj, w_branch, layer):
    t = proj.shape[0]
    bw = y_ssm.shape[1]
    per_branch = D_MODEL // MM_TN
    prompt_blocks = T_PROMPT // MM_TM
    y_spec = pl.BlockSpec((MM_TM, bw), lambda i, j: (i, 0))
    yp_spec = pl.BlockSpec((MM_TM, bw), lambda i, j: (jnp.minimum(i, prompt_blocks - 1), 0))
    ys_spec = pl.BlockSpec((MM_TM, bw), lambda i, j: (jnp.maximum(i - prompt_blocks, 0), 0))

    def gate_spec(n):
        return pl.BlockSpec((MM_TM, MM_TN), lambda i, j: (i, n * per_branch + j))

    return pl.pallas_call(
        _merge_kernel,
        grid=(t // MM_TM, D_MODEL // MM_TN),
        in_specs=[yp_spec, ys_spec, yp_spec, ys_spec, y_spec, y_spec]
        + [pl.BlockSpec((1, N_BRANCH, bw, MM_TN), lambda i, j: (layer, 0, 0, j))]
        + [gate_spec(n) for n in range(N_BRANCH)],
        out_specs=pl.BlockSpec((MM_TM, MM_TN), lambda i, j: (i, j)),
        out_shape=jax.ShapeDtypeStruct((t, D_MODEL), BF16),
        compiler_params=_params(("parallel", "parallel")),
        name="branch_merge",
    )(*y_na, *y_gqa, y_ssm, y_rwkv, w_branch, proj, proj, proj, proj)


def _out_proj_kernel(m_ref, w_ref, x_ref, mod_ref, o_ref, *, gate_idx):
    out = _dot(m_ref[...], w_ref[0].astype(BF16))
    o_ref[...] = x_ref[...] + mod_ref[0, gate_idx:gate_idx + 1, :] * out


def _out_proj_residual(merged, w_out, layer, x, mod, gate_idx):
    t = x.shape[0]
    return pl.pallas_call(
        functools.partial(_out_proj_kernel, gate_idx=gate_idx),
        grid=(t // MM_TM, D_MODEL // MM_TN),
        in_specs=[
            pl.BlockSpec((MM_TM, D_MODEL), lambda i, j: (i, 0)),
            pl.BlockSpec((1, D_MODEL, MM_TN), lambda i, j: (layer, 0, j)),
            pl.BlockSpec((MM_TM, MM_TN), lambda i, j: (i, j)),
            pl.BlockSpec((1, 6, MM_TN), lambda i, j: (_cond_group(i, MM_TM), 0, j)),
        ],
        out_specs=pl.BlockSpec((MM_TM, MM_TN), lambda i, j: (i, j)),
        out_shape=jax.ShapeDtypeStruct((t, D_MODEL), F32),
        compiler_params=_params(("parallel", "parallel")),
        name="out_proj",
    )(merged, w_out, x, mod)


def _attend(q, k, v):
    p = _softmax_rows(_dot_nt(q, k) * ATT_SCALE)
    return _dot(p.astype(BF16), v)


def _ctx_na_kernel(q_ref, k_ref, v_ref, y_ref, ko_ref, vo_ref):
    for h in range(NA_HEADS):
        sl = slice(HEAD_DIM * h, HEAD_DIM * (h + 1))
        kf = k_ref[:, sl]
        vf = v_ref[:, sl]
        y = _attend(q_ref[:, sl].astype(BF16), kf.astype(BF16), vf.astype(BF16))
        y_ref[:, sl] = y.astype(BF16)
        ko_ref[0, h] = kf
        vo_ref[0, h] = vf


def _ctx_na(proj):
    cache = jax.ShapeDtypeStruct((BATCH, NA_HEADS, SEQ, HEAD_DIM), F32)
    cache_spec = pl.BlockSpec((1, NA_HEADS, SEQ, HEAD_DIM), lambda b: (b, 0, 0, 0))
    return pl.pallas_call(
        _ctx_na_kernel,
        grid=(BATCH,),
        in_specs=[pl.BlockSpec((SEQ, NA_W), lambda b, c=c: (b, c)) for c in range(3)],
        out_specs=[pl.BlockSpec((SEQ, NA_W), lambda b: (b, 0)), cache_spec, cache_spec],
        out_shape=[jax.ShapeDtypeStruct((T_PROMPT, NA_W), BF16), cache, cache],
        compiler_params=_params(("parallel",)),
        name="ctx_na",
    )(proj, proj, proj)


def _ctx_gqa_kernel(q_ref, k_ref, v_ref, gq_ref, gk_ref, y_ref, ko_ref, vo_ref):
    for kv in range(GQA_KV_HEADS):
        sl = slice(HEAD_DIM * kv, HEAD_DIM * (kv + 1))
        kn = _rms(k_ref[:, sl], gk_ref[...])
        vf = v_ref[:, sl]
        ko_ref[0, kv] = kn
        vo_ref[0, kv] = vf
        kb = kn.astype(BF16)
        vb = vf.astype(BF16)
        for g in range(GQA_GROUP):
            h = kv * GQA_GROUP + g
            hs = slice(HEAD_DIM * h, HEAD_DIM * (h + 1))
            qn = _rms(q_ref[:, hs], gq_ref[...])
            y_ref[:, hs] = _attend(qn.astype(BF16), kb, vb).astype(BF16)


def _ctx_gqa(proj, gq, gk):
    cache = jax.ShapeDtypeStruct((BATCH, GQA_KV_HEADS, SEQ, HEAD_DIM), F32)
    cache_spec = pl.BlockSpec((1, GQA_KV_HEADS, SEQ, HEAD_DIM), lambda b: (b, 0, 0, 0))
    norm_spec = pl.BlockSpec((1, HEAD_DIM), lambda b: (0, 0))
    return pl.pallas_call(
        _ctx_gqa_kernel,
        grid=(BATCH,),
        in_specs=[
            pl.BlockSpec((SEQ, GQA_Q_W), lambda b: (b, OFF_GQA // GQA_Q_W)),
            pl.BlockSpec((SEQ, GQA_KV_W), lambda b: (b, (OFF_GQA + GQA_Q_W) // GQA_KV_W)),
            pl.BlockSpec((SEQ, GQA_KV_W), lambda b: (b, (OFF_GQA + GQA_Q_W) // GQA_KV_W + 1)),
            norm_spec, norm_spec,
        ],
        out_specs=[pl.BlockSpec((SEQ, GQA_Q_W), lambda b: (b, 0)), cache_spec, cache_spec],
        out_shape=[jax.ShapeDtypeStruct((T_PROMPT, GQA_Q_W), BF16), cache, cache],
        compiler_params=_params(("parallel",)),
        name="ctx_gqa",
    )(proj, proj, proj, gq.reshape(1, HEAD_DIM), gk.reshape(1, HEAD_DIM))


LAT_Q_ROWS = 256
LAT_ROW0 = T_PROMPT // DEC_SEQ


def _rope_tables():
    t = np.arange(DEC_SEQ)
    half = HEAD_DIM // 2
    inv = jnp.asarray(ROPE_THETA, F32) ** (-jnp.arange(0, half, 2, dtype=F32) / half)
    rows = jnp.asarray(t // GRID_W, F32)
    cols = jnp.asarray(t % GRID_W, F32)
    ang_r = rows[:, None] * inv[None, :]
    ang_c = cols[:, None] * inv[None, :]
    cos = jnp.concatenate([jnp.cos(ang_r)] * 2 + [jnp.cos(ang_c)] * 2, axis=-1)
    sin = jnp.concatenate([-jnp.sin(ang_r), jnp.sin(ang_r), -jnp.sin(ang_c), jnp.sin(ang_c)], axis=-1)
    return cos, sin


def _rope(x, cos, sin):
    q = HEAD_DIM // 4
    lane = lax.broadcasted_iota(jnp.int32, x.shape, 1)
    partner = jnp.where((lane % (2 * q)) < q, pltpu.roll(x, HEAD_DIM - q, 1), pltpu.roll(x, q, 1))
    return x * cos + partner * sin


def _lat_gqa_kernel(q_ref, k_ref, v_ref, ck_ref, cv_ref, gq_ref, gk_ref, cos_ref, sin_ref, y_ref, kall, vall):
    cos = cos_ref[...]
    sin = sin_ref[...]
    kall[0:PAST_LEN, :] = ck_ref[0, 0, 0].astype(BF16)
    vall[0:PAST_LEN, :] = cv_ref[0, 0, 0].astype(BF16)
    kall[PAST_LEN:, :] = _rope(_rms(k_ref[...], gk_ref[...]), cos, sin).astype(BF16)
    vall[PAST_LEN:, :] = v_ref[...].astype(BF16)
    for g in range(GQA_GROUP):
        hs = slice(HEAD_DIM * g, HEAD_DIM * (g + 1))
        for qb in range(DEC_SEQ // LAT_Q_ROWS):
            rs = slice(LAT_Q_ROWS * qb, LAT_Q_ROWS * (qb + 1))
            qn = _rope(_rms(q_ref[rs, hs], gq_ref[...]), cos[rs], sin[rs])
            y_ref[rs, hs] = _attend(qn.astype(BF16), kall[...], vall[...]).astype(BF16)


def _lat_gqa(proj, cache_k, cache_v, layer, gq, gk, cos, sin):
    group_w = GQA_GROUP * HEAD_DIM
    kcol = (OFF_GQA + GQA_Q_W) // HEAD_DIM
    cache_spec = pl.BlockSpec((1, 1, 1, PAST_LEN, HEAD_DIM), lambda b, kv: (b, layer, kv, 0, 0))
    norm_spec = pl.BlockSpec((1, HEAD_DIM), lambda b, kv: (0, 0))
    tab_spec = pl.BlockSpec((DEC_SEQ, HEAD_DIM), lambda b, kv: (0, 0))
    return pl.pallas_call(
        _lat_gqa_kernel,
        grid=(DEC_BATCH, GQA_KV_HEADS),
        in_specs=[
            pl.BlockSpec((DEC_SEQ, group_w), lambda b, kv: (LAT_ROW0 + b, OFF_GQA // group_w + kv)),
            pl.BlockSpec((DEC_SEQ, HEAD_DIM), lambda b, kv: (LAT_ROW0 + b, kcol + kv)),
            pl.BlockSpec((DEC_SEQ, HEAD_DIM), lambda b, kv: (LAT_ROW0 + b, kcol + GQA_KV_HEADS + kv)),
            cache_spec, cache_spec, norm_spec, norm_spec, tab_spec, tab_spec,
        ],
        out_specs=pl.BlockSpec((DEC_SEQ, group_w), lambda b, kv: (b, kv)),
        out_shape=jax.ShapeDtypeStruct((T_SAMPLE, GQA_Q_W), BF16),
        scratch_shapes=[pltpu.VMEM((PAST_LEN + DEC_SEQ, HEAD_DIM), BF16)] * 2,
        compiler_params=_params(("parallel", "parallel")),
        name="lat_gqa",
    )(proj, proj, proj, cache_k, cache_v, gq.reshape(1, HEAD_DIM), gk.reshape(1, HEAD_DIM), cos, sin)


NA_ROWS = DEC_SEQ // GRID_W
NA_WR = min(NA_WIN_ROWS, NA_ROWS)
NA_BAND = NA_WR * GRID_W
NA_DR = 2 * NA_WIN_ROWS - 1
NA_DC = 2 * NA_WIN_COLS - 1


def _na_row_window(r):
    r0 = min(max(r - NA_WR // 2, 0), NA_ROWS - NA_WR)
    return r0, r0 - r + NA_WIN_ROWS - 1


def _na_table_kernel(rpb_ref, sel_ref, ok_ref, o_ref):
    picked = _dot_exact_lhs(rpb_ref[...], sel_ref[...])
    o_ref[...] = jnp.where(ok_ref[...] > 0.5, picked, NEG)


def _na_band_bias(rpb):
    c = np.arange(GRID_W)
    dc = c[None, :] - c[:, None] + NA_WIN_COLS - 1
    c0 = np.clip(c - NA_WIN_COLS // 2, 0, GRID_W - NA_WIN_COLS)
    ok = (c[None, :] >= c0[:, None]) & (c[None, :] < c0[:, None] + NA_WIN_COLS)
    sel = np.zeros((128, GRID_W * GRID_W), np.float32)
    flat_dc = np.clip(dc, 0, NA_DC - 1).reshape(-1)
    sel[flat_dc, np.arange(GRID_W * GRID_W)] = 1.0
    n = NA_HEADS * NA_DR
    rpb2 = jnp.pad(rpb.reshape(n, NA_DC).astype(F32), ((0, 0), (0, 128 - NA_DC)))
    full = lambda shape: pl.BlockSpec(shape, lambda: (0,) * len(shape))
    table = pl.pallas_call(
        _na_table_kernel,
        in_specs=[full((n, 128)), full((128, GRID_W * GRID_W)), full((1, GRID_W * GRID_W))],
        out_specs=full((n, GRID_W * GRID_W)),
        out_shape=jax.ShapeDtypeStruct((n, GRID_W * GRID_W), F32),
        name="na_table",
    )(rpb2, jnp.asarray(sel, BF16), jnp.asarray(ok.reshape(1, -1), F32))
    table = table.reshape(NA_HEADS, NA_DR, GRID_W, GRID_W)
    bands = [table[:, a0:a0 + NA_WR].transpose(0, 2, 1, 3).reshape(NA_HEADS, GRID_W, NA_BAND)
             for a0 in range(NA_DR - NA_WR + 1)]
    return jnp.stack(bands, axis=1)


def _lat_na_kernel(q_ref, k_ref, v_ref, ck_ref, cv_ref, bias_ref, y_ref):
    kb = k_ref[...].astype(BF16)
    vb = v_ref[...].astype(BF16)
    ck = ck_ref[0, 0, 0].astype(BF16)
    cv = cv_ref[0, 0, 0].astype(BF16)
    for r in range(NA_ROWS):
        r0, a0 = _na_row_window(r)
        rs = slice(GRID_W * r, GRID_W * (r + 1))
        ks = slice(GRID_W * r0, GRID_W * r0 + NA_BAND)
        q = q_ref[rs, :].astype(BF16)
        s_loc = _dot_nt(q, kb[ks]) * ATT_SCALE + bias_ref[0, a0]
        s_ctx = _dot_nt(q, ck) * ATT_SCALE
        m = jnp.maximum(jnp.max(s_loc, axis=-1, keepdims=True), jnp.max(s_ctx, axis=-1, keepdims=True))
        e_loc = jnp.exp(s_loc - m)
        e_ctx = jnp.exp(s_ctx - m)
        den = jnp.sum(e_loc, axis=-1, keepdims=True) + jnp.sum(e_ctx, axis=-1, keepdims=True)
        out = _dot((e_loc / den).astype(BF16), vb[ks]) + _dot((e_ctx / den).astype(BF16), cv)
        y_ref[rs, :] = out.astype(BF16)


def _lat_na(proj, cache_k, cache_v, layer, bias):
    cache_spec = pl.BlockSpec((1, 1, 1, PAST_LEN, HEAD_DIM), lambda h, b: (b, layer, h, 0, 0))

    def qkv_spec(c):
        return pl.BlockSpec((DEC_SEQ, HEAD_DIM), lambda h, b: (LAT_ROW0 + b, c * NA_HEADS + h))

    return pl.pallas_call(
        _lat_na_kernel,
        grid=(NA_HEADS, DEC_BATCH),
        in_specs=[qkv_spec(0), qkv_spec(1), qkv_spec(2), cache_spec, cache_spec,
                  pl.BlockSpec((1, NA_DR - NA_WR + 1, GRID_W, NA_BAND), lambda h, b: (h, 0, 0, 0))],
        out_specs=pl.BlockSpec((DEC_SEQ, HEAD_DIM), lambda h, b: (b, h)),
        out_shape=jax.ShapeDtypeStruct((T_SAMPLE, NA_W), BF16),
        compiler_params=_params(("parallel", "parallel")),
        name="lat_na",
    )(proj, proj, proj, cache_k, cache_v, bias)


SEQ_BLOCK = 256
N_SEQ_BLOCKS = T_ALL // SEQ_BLOCK
N_SEQS = BATCH + DEC_BATCH


def _walk_block(d, i):
    return jnp.where(d == 0, i, N_SEQ_BLOCKS - 1 - i)


def _block_seq(j):
    prompt_blocks = T_PROMPT // SEQ_BLOCK
    per_prompt, per_sample = SEQ // SEQ_BLOCK, DEC_SEQ // SEQ_BLOCK
    js = j - prompt_blocks
    in_prompt = j < prompt_blocks
    return (jnp.where(in_prompt, j // per_prompt, BATCH + js // per_sample),
            jnp.where(in_prompt, j % per_prompt, js % per_sample),
            jnp.where(in_prompt, per_prompt, per_sample))


def _walk_flags(d, i):
    _, pos, n = _block_seq(_walk_block(d, i))
    return pos == jnp.where(d == 0, 0, n - 1), pos == jnp.where(d == 0, n - 1, 0)


S5_TC = SEQ_BLOCK
S5_PITCH = S5_TC + 8
S5_SLABS = SSM_GROUPS * SSM_STATE // 128
S5_QB = 4
S5_SLABS_Q = S5_SLABS // S5_QB


def _s5_disc_kernel(lr_ref, li_ref, dt_ref, br_ref, bi_ref, ar_ref, ai_ref, bbr_ref, bbi_ref):
    lr, li, dt = lr_ref[...], li_ref[...], dt_ref[...]
    mag = jnp.exp(lr * dt)
    ab_re = mag * jnp.cos(li * dt)
    ab_im = mag * jnp.sin(li * dt)
    den = lr * lr + li * li
    nr, ni = ab_re - 1.0, ab_im
    co_re = (nr * lr + ni * li) / den
    co_im = (ni * lr - nr * li) / den
    br, bi = br_ref[...], bi_ref[...]
    ar_ref[...] = ab_re
    ai_ref[...] = ab_im
    bbr_ref[...] = co_re * br - co_im * bi
    bbi_ref[...] = co_re * bi + co_im * br


def _s5_params(lam_re, lam_im, log_dt, b_re, b_im, c_re, c_im):
    g, p, c = SSM_GROUPS, SSM_STATE, SSM_GROUP
    shape = (2, g, c, p)
    n = 2 * g * c
    bc = lambda a: jnp.broadcast_to(a, shape).reshape(n, p)
    dt = jnp.exp(log_dt)
    args = (bc(lam_re[:, :, None, :]), bc(lam_im[:, :, None, :]), bc(dt[:, :, None, None]),
            jnp.transpose(b_re, (0, 1, 3, 2)).reshape(n, p), jnp.transpose(b_im, (0, 1, 3, 2)).reshape(n, p))
    spec = pl.BlockSpec((n, p), lambda: (0, 0))
    out = jax.ShapeDtypeStruct((n, p), F32)
    ab_re, ab_im, bb_re, bb_im = pl.pallas_call(
        _s5_disc_kernel, in_specs=[spec] * 5, out_specs=[spec] * 4, out_shape=[out] * 4, name="s5_disc")(*args)
    a = jnp.stack([ab_re.reshape(shape)[:, :, 0, :], ab_im.reshape(shape)[:, :, 0, :]], axis=1)
    a = a.reshape(2, 2, S5_SLABS, 128)
    eye = jnp.eye(16, dtype=F32)

    def blockdiag_in(bb):
        return jnp.einsum('dqgcp,gh->dqgchp', bb.reshape(2, S5_QB, 16, c, p), eye).reshape(2, S5_QB, 16 * c, 16 * p)

    def blockdiag_out(cc):
        return jnp.einsum('dqgcp,gh->dqgphc', cc.reshape(2, S5_QB, 16, c, p), eye).reshape(2, S5_QB, 16 * p, 16 * c)

    bb = jnp.concatenate([blockdiag_in(bb_re.reshape(shape)), blockdiag_in(bb_im.reshape(shape))], axis=-1)
    cc = jnp.stack([blockdiag_out(c_re), blockdiag_out(-c_im)], axis=2)
    return a, bb.astype(BF16), cc.astype(BF16)


def _s5_scan_kernel(u0, u1, u2, u3, bb_ref, cc_ref, a_ref, h0_ref, y_ref, hout_ref,
                    bur, bui, xsr, xsi, st):
    d = pl.program_id(0)
    starts, ends = _walk_flags(d, pl.program_id(1))

    @pl.when(starts)
    def _():
        st[0] = h0_ref[0, 0, 0]
        st[1] = h0_ref[0, 0, 1]

    half = S5_SLABS_Q * 128
    for q, u in enumerate((u0, u1, u2, u3)):
        buq = _dot(u[...].astype(BF16), bb_ref[0, q])
        for j in range(S5_SLABS_Q):
            row0 = (q * S5_SLABS_Q + j) * S5_PITCH
            bur[row0:row0 + S5_TC, :] = buq[:, 128 * j:128 * (j + 1)]
            bui[row0:row0 + S5_TC, :] = buq[:, half + 128 * j:half + 128 * (j + 1)]

    ar = a_ref[0, 0]
    ai = a_ref[0, 1]

    def step(i, carry):
        xr, xi = carry
        t = jnp.where(d == 0, i, S5_TC - 1 - i)
        rows = pl.ds(t, S5_SLABS, stride=S5_PITCH)
        nxr = ar * xr - ai * xi + bur[rows, :]
        nxi = ar * xi + ai * xr + bui[rows, :]
        xsr[rows, :] = nxr
        xsi[rows, :] = nxi
        return nxr, nxi

    xr, xi = lax.fori_loop(0, S5_TC, step, (st[0], st[1]))
    st[0] = xr
    st[1] = xi

    @pl.when(ends)
    def _():
        hout_ref[0, 0, 0] = xr
        hout_ref[0, 0, 1] = xi

    for q in range(S5_QB):
        acc = jnp.zeros((S5_TC, 16 * SSM_GROUP), F32)
        for j in range(S5_SLABS_Q):
            row0 = (q * S5_SLABS_Q + j) * S5_PITCH
            ks = slice(128 * j, 128 * (j + 1))
            acc += _dot(xsr[row0:row0 + S5_TC, :].astype(BF16), cc_ref[0, q, 0, ks, :])
            acc += _dot(xsi[row0:row0 + S5_TC, :].astype(BF16), cc_ref[0, q, 1, ks, :])
        y_ref[0, :, 256 * q:256 * (q + 1)] = acc


def _s5_scan(proj, a, bb, cc, h0):
    ucol = OFF_SSM // 256

    def u_spec(q):
        return pl.BlockSpec((S5_TC, 256), lambda d, i: (_walk_block(d, i), ucol + q))

    state = pl.BlockSpec((1, 1, 2, S5_SLABS, 128), lambda d, i: (_block_seq(_walk_block(d, i))[0], d, 0, 0, 0))
    slab = pltpu.VMEM((S5_SLABS * S5_PITCH, 128), F32)
    return pl.pallas_call(
        _s5_scan_kernel,
        grid=(2, N_SEQ_BLOCKS),
        in_specs=[u_spec(q) for q in range(S5_QB)] + [
            pl.BlockSpec((1, S5_QB, 256, 2 * S5_SLABS_Q * 128), lambda d, i: (d, 0, 0, 0)),
            pl.BlockSpec((1, S5_QB, 2, S5_SLABS_Q * 128, 256), lambda d, i: (d, 0, 0, 0, 0)),
            pl.BlockSpec((1, 2, S5_SLABS, 128), lambda d, i: (d, 0, 0, 0)),
            state,
        ],
        out_specs=[pl.BlockSpec((1, S5_TC, SSM_WIDTH), lambda d, i: (d, _walk_block(d, i), 0)), state],
        out_shape=[jax.ShapeDtypeStruct((2, T_ALL, SSM_WIDTH), F32),
                   jax.ShapeDtypeStruct((N_SEQS, 2, 2, S5_SLABS, 128), F32)],
        scratch_shapes=[slab, slab, slab, slab, pltpu.VMEM((2, S5_SLABS, 128), F32)],
        compiler_params=_params(("parallel", "arbitrary")),
        name="s5_scan",
    )(proj, proj, proj, proj, bb, cc, a, h0)


def _s5_glu_kernel(ua_ref, ub_ref, y0_ref, y1_ref, d_ref, w_ref, o_ref):
    u = jnp.concatenate([ua_ref[...], ub_ref[...]], axis=-1)
    y = jax.nn.gelu(u * d_ref[0] + y0_ref[0] + y1_ref[0])
    o_ref[...] = (y * jax.nn.sigmoid(_dot(y.astype(BF16), w_ref[0].astype(BF16)))).astype(BF16)


S5_GLU_ROWS = 512


def _s5_glu(proj, row0, y, ssm_d, w_glu, layer):
    rows = y.shape[1]
    tr = S5_GLU_ROWS
    blk0 = row0 // tr
    ucol = OFF_SSM // 512
    return pl.pallas_call(
        _s5_glu_kernel,
        grid=(rows // tr,),
        in_specs=[
            pl.BlockSpec((tr, 512), lambda i: (blk0 + i, ucol)),
            pl.BlockSpec((tr, 512), lambda i: (blk0 + i, ucol + 1)),
            pl.BlockSpec((1, tr, SSM_WIDTH), lambda i: (0, i, 0)),
            pl.BlockSpec((1, tr, SSM_WIDTH), lambda i: (1, i, 0)),
            pl.BlockSpec((1, 1, SSM_WIDTH), lambda i: (layer, 0, 0)),
            pl.BlockSpec((1, SSM_WIDTH, SSM_WIDTH), lambda i: (layer, 0, 0)),
        ],
        out_specs=pl.BlockSpec((tr, SSM_WIDTH), lambda i: (i, 0)),
        out_shape=jax.ShapeDtypeStruct((rows, SSM_WIDTH), BF16),
        compiler_params=_params(("parallel",)),
        name="s5_glu",
    )(proj, proj, y, y, ssm_d.reshape(DEPTH, 1, SSM_WIDTH), w_glu)


RW_ROWS = 256
RW_C = 64
RW_PAIRS = RWKV_HEADS // 2
RW_LOWRANK0 = 3 * RWKV_WIDTH
RW_PAIR_GROUP = 8


def _head_indicator():
    ind = (np.arange(RWKV_WIDTH)[:, None] // RWKV_HEAD == np.arange(128)[None, :]).astype(np.float32)
    return jnp.asarray(ind, BF16), jnp.asarray(ind.T, BF16)


def _head_sum(x, ind, ind_t):
    return _dot_exact_lhs(_dot_exact_lhs(x, ind), ind_t)


def _rwkv_prep_kernel(x_ref, prev_ref, next_ref, mu_ref, kk_ref, ka_ref, rk_ref, w0_ref, a0_ref, w2_ref, a2_ref,
                      g2_ref, ind_ref, indt_ref,
                      r_out, kkn_out, v_out, g_out, bonus_out, lw_out, kka_out, kd_out, *, seq_blocks):
    i = pl.program_id(0)
    n_prompt_blocks = T_PROMPT // RW_ROWS
    j = i - n_prompt_blocks
    in_seq = jnp.where(i < n_prompt_blocks, i % seq_blocks[0], j % seq_blocks[1])
    n_in_seq = jnp.where(i < n_prompt_blocks, seq_blocks[0], seq_blocks[1])
    first = in_seq == 0
    last = in_seq == n_in_seq - 1
    x = x_ref[...]
    row = lax.broadcasted_iota(jnp.int32, x.shape, 0)
    prev_row = jnp.where(first, 0.0, prev_ref[7:8, :])
    next_row = jnp.where(last, 0.0, next_ref[0:1, :])
    xp = jnp.where(row == 0, prev_row, pltpu.roll(x, 1, 0))
    xn = jnp.where(row == RW_ROWS - 1, next_row, pltpu.roll(x, RW_ROWS - 1, 0))
    z = x + mu_ref[...] * (0.5 * (xp + xn) - x)

    w = RWKV_WIDTH
    r, k, v = z[:, 0:w], z[:, w:2 * w], z[:, 2 * w:3 * w]
    low = z[:, RW_LOWRANK0:RW_LOWRANK0 + 128]
    gl = z[:, RW_LOWRANK0 + 128:RW_LOWRANK0 + 256]
    ind, ind_t = ind_ref[...], indt_ref[...]

    kk = k * kk_ref[...]
    kk = kk * lax.rsqrt(_head_sum(kk * kk, ind, ind_t) + 1e-12)
    r_out[...] = r
    kkn_out[...] = kk
    v_out[...] = v
    g_out[...] = _dot(jax.nn.sigmoid(gl).astype(BF16), g2_ref[...].astype(BF16))
    tanh_low = jnp.tanh(low).astype(BF16)
    low_b = low.astype(BF16)
    bonus = jnp.zeros_like(r)
    for d in range(2):
        w_log = -jax.nn.softplus(-(w0_ref[d:d + 1, :] + _dot(tanh_low, w2_ref[d].astype(BF16)))) - 0.5
        a = jax.nn.sigmoid(a0_ref[d:d + 1, :] + _dot(low_b, a2_ref[d].astype(BF16)))
        kd = k * (1.0 + (a - 1.0) * ka_ref[...])
        lw_out[d] = -jnp.exp(w_log)
        kka_out[d] = kk * a
        kd_out[d] = kd
        bonus = bonus + _head_sum(r * kd * rk_ref[...], ind, ind_t) * v
    bonus_out[...] = bonus


def _rwkv_prep(proj_r, lp):
    t = proj_r.shape[0]
    nb = t // RW_ROWS
    halo = RW_ROWS // 8
    w = RWKV_WIDTH
    zpad = jnp.zeros((2, 64, w), F32)
    w2 = jnp.concatenate([lp['rwkv_w2'], zpad], axis=1)
    a2 = jnp.concatenate([zpad, lp['rwkv_a2']], axis=1)
    ind, ind_t = _head_indicator()
    row = lambda a: a.reshape(1, -1)
    full = lambda shape: pl.BlockSpec(shape, lambda i: (0,) * len(shape))
    tok = pl.BlockSpec((RW_ROWS, w), lambda i: (i, 0))
    tok2 = pl.BlockSpec((2, RW_ROWS, w), lambda i: (0, i, 0))
    f1 = jax.ShapeDtypeStruct((t, w), F32)
    f2 = jax.ShapeDtypeStruct((2, t, w), F32)
    return pl.pallas_call(
        functools.partial(_rwkv_prep_kernel, seq_blocks=(SEQ // RW_ROWS, DEC_SEQ // RW_ROWS)),
        grid=(nb,),
        in_specs=[
            pl.BlockSpec((RW_ROWS, RWKV_IN_W), lambda i: (i, 0)),
            pl.BlockSpec((8, RWKV_IN_W), lambda i: (jnp.maximum(i * halo - 1, 0), 0)),
            pl.BlockSpec((8, RWKV_IN_W), lambda i: (jnp.minimum((i + 1) * halo, nb * halo - 1), 0)),
            full((1, RWKV_IN_W)), full((1, w)), full((1, w)), full((1, w)), full((2, w)), full((2, w)),
            full((2, 128, w)), full((2, 128, w)), full((GATE_RANK, w)), full((w, 128)), full((128, w)),
        ],
        out_specs=[tok, tok, tok, tok, tok, tok2, tok2, tok2],
        out_shape=[f1, f1, f1, f1, f1, f2, f2, f2],
        compiler_params=_params(("parallel",)),
        name="rwkv_prep",
    )(proj_r, proj_r, proj_r, row(lp['rwkv_mu']), row(lp['rwkv_k_k']), row(lp['rwkv_k_a']), row(lp['rwkv_r_k']),
      lp['rwkv_w0'], lp['rwkv_a0'], w2, a2, lp['rwkv_g2'], ind, ind_t)


def _rwkv_chunk_kernel(r_ref, kk_ref, v_ref, lw_ref, kka_ref, kd_ref, r2_out, y0_out, mneg_out, sadd_out, gc_out):
    d = pl.program_id(0)
    c = RW_C
    ri = lax.broadcasted_iota(jnp.int32, (2 * c, 2 * c), 0)
    ci = lax.broadcasted_iota(jnp.int32, (2 * c, 2 * c), 1)
    lower_half_rows = ri >= c
    same_head = lower_half_rows == (ci >= c)
    ti, tj = ri % c, ci % c
    before = (tj - ti) * (1 - 2 * d) < 0
    strict = same_head & before
    incl = same_head & (before | (ti == tj))
    eye = (ri == ci).astype(F32)
    tri_c = (incl[0:c, 0:c]).astype(BF16)
    lane = lax.broadcasted_iota(jnp.int32, (1, 2 * c), 1)
    m_a = (lane < RWKV_HEAD).astype(F32)
    m_b = 1.0 - m_a
    bf = lambda a: a.astype(BF16)

    def group(ps):
        each = lambda f, *xs: [f(*a) for a in zip(*xs)]
        sl = [slice(128 * p, 128 * (p + 1)) for p in ps]
        r, kk, v = ([ref[:, s] for s in sl] for ref in (r_ref, kk_ref, v_ref))
        lw, kka, kd = ([ref[0, :, s] for s in sl] for ref in (lw_ref, kka_ref, kd_ref))
        cum = each(lambda x: _dot_exact_rhs(tri_c, x), lw)
        tot = each(lambda x: jnp.where(d == 0, x[c - 1:c, :], x[0:1, :]), cum)
        g_rem = each(lambda t, x: jnp.exp(t - x), tot, cum)
        g_inv = each(lambda x: jnp.exp(-x), cum)
        qk = each(lambda k_, x, l: k_ * jnp.exp(x - l), kk, cum, lw)
        rt = each(lambda r_, x: r_ * jnp.exp(x), r, cum)
        lhs_q = each(lambda q: jnp.concatenate([q * m_a, q * m_b], axis=0), qk)
        lhs_r = each(lambda q: jnp.concatenate([q * m_a, q * m_b], axis=0), rt)
        rhs = each(lambda k_, a_, g: bf(jnp.concatenate([k_ * g, a_ * g], axis=0)), kd, kka, g_inv)
        x = each(lambda l, rr: _dot_nt(bf(l), rr), lhs_q, rhs)
        z = each(lambda l, rr: _dot_nt(bf(l), rr), lhs_r, rhs)
        xr = each(lambda t: pltpu.roll(t, c, 1), x)
        zr = each(lambda t: pltpu.roll(t, c, 1), z)
        qk_bd = each(lambda t, tr: bf(jnp.where(strict, jnp.where(lower_half_rows, tr, t), 0.0)), x, xr)
        qa_bd = each(lambda t, tr: jnp.where(strict, jnp.where(lower_half_rows, t, tr), 0.0), x, xr)
        rk_bd = each(lambda t, tr: bf(jnp.where(incl, jnp.where(lower_half_rows, tr, t), 0.0)), z, zr)
        ra_bd = each(lambda t, tr: bf(jnp.where(incl, jnp.where(lower_half_rows, t, tr), 0.0)), z, zr)
        inv = each(lambda n_: eye - n_, qa_bd)
        power = qa_bd
        for _ in range(5):
            pb = each(bf, power)
            power = each(_dot, pb, pb)
            inv = each(lambda i_, p_: i_ + _dot(bf(i_), bf(p_)), inv, power)
        v_st = each(lambda t: bf(jnp.concatenate([t * m_a, t * m_b], axis=0)), v)
        av_st = each(_dot, qk_bd, v_st)
        sol = each(lambda i_, q, av: _dot(bf(i_), jnp.concatenate([bf(q), bf(av)], axis=1)), inv, lhs_q, av_st)
        ra_sol = each(lambda a_, s_: _dot(a_, bf(s_)), ra_bd, sol)
        rk_v = each(_dot, rk_bd, v_st)
        fold = lambda t: t[0:c] + t[c:2 * c]
        for i, p in enumerate(ps):
            w_f = bf(fold(sol[i][:, 0:128]))
            u0_f = bf(fold(sol[i][:, 128:256]))
            ah_b = bf(kka[i] * g_rem[i])
            r2_out[0, :, sl[i]] = bf(fold(lhs_r[i] - ra_sol[i][:, 0:128]))
            y0_out[0, :, sl[i]] = fold(rk_v[i] - ra_sol[i][:, 128:256])
            mneg_out[0, 0, p] = bf(jnp.where(same_head, _dot_tn(w_f, ah_b), 0.0))
            sadd_out[0, 0, p] = jnp.where(
                same_head, _dot_tn(bf(v[i]), bf(kd[i] * g_rem[i])) - _dot_tn(u0_f, ah_b), 0.0)
            gc_out[0, 0, :, sl[i]] = jnp.exp(tot[i])

    for p0 in range(0, RW_PAIRS, RW_PAIR_GROUP):
        group(list(range(p0, p0 + RW_PAIR_GROUP)))


def _rwkv_chunks(r, kk, v, lw, kka, kd):
    t = r.shape[0]
    nck = t // RW_C
    w = RWKV_WIDTH
    tok = pl.BlockSpec((RW_C, w), lambda d, i: (i, 0))
    tok2 = pl.BlockSpec((1, RW_C, w), lambda d, i: (d, i, 0))
    mat = pl.BlockSpec((1, 1, RW_PAIRS, 128, 128), lambda d, i: (d, i, 0, 0, 0))
    mat_shape = lambda dtype: jax.ShapeDtypeStruct((2, nck, RW_PAIRS, 128, 128), dtype)
    return pl.pallas_call(
        _rwkv_chunk_kernel,
        grid=(2, nck),
        in_specs=[tok, tok, tok, tok2, tok2, tok2],
        out_specs=[tok2, tok2, mat, mat, pl.BlockSpec((1, 1, 1, w), lambda d, i: (d, i, 0, 0))],
        out_shape=[jax.ShapeDtypeStruct((2, t, w), BF16), jax.ShapeDtypeStruct((2, t, w), F32),
                   mat_shape(BF16), mat_shape(F32), jax.ShapeDtypeStruct((2, nck, 1, w), F32)],
        compiler_params=_params(("parallel", "parallel")),
        name="rwkv_chunks",
    )(r, kk, v, lw, kka, kd)


RW_BLOCK_CHUNKS = SEQ_BLOCK // RW_C


def _rwkv_state_kernel(r2_ref, y0_ref, mneg_ref, sadd_ref, gc_ref, s0_ref, y_out, s_out, st):
    d = pl.program_id(0)
    starts, ends = _walk_flags(d, pl.program_id(1))

    @pl.when(starts)
    def _():
        st[...] = s0_ref[0, 0]

    def chunk(i, carry):
        ck = jnp.where(d == 0, i, RW_BLOCK_CHUNKS - 1 - i)
        rows = pl.ds(pl.multiple_of(ck * RW_C, RW_C), RW_C)
        for p in range(RW_PAIRS):
            ls = slice(128 * p, 128 * (p + 1))
            s = st[p]
            sb = s.astype(BF16)
            y_out[0, rows, ls] = _dot_nt(r2_ref[0, rows, ls], sb) + y0_ref[0, rows, ls]
            st[p] = s * gc_ref[0, ck, :, ls] - _dot(sb, mneg_ref[0, ck, p]) + sadd_ref[0, ck, p]
        return carry

    lax.fori_loop(0, RW_BLOCK_CHUNKS, chunk, 0)

    @pl.when(ends)
    def _():
        s_out[0, 0] = st[...]


def _rwkv_state(r2, y0, mneg, sadd, gc, s0):
    w = RWKV_WIDTH
    tok = pl.BlockSpec((1, SEQ_BLOCK, w), lambda d, i: (d, _walk_block(d, i), 0))
    mat = pl.BlockSpec((1, RW_BLOCK_CHUNKS, RW_PAIRS, 128, 128), lambda d, i: (d, _walk_block(d, i), 0, 0, 0))
    state = pl.BlockSpec((1, 1, RW_PAIRS, 128, 128), lambda d, i: (_block_seq(_walk_block(d, i))[0], d, 0, 0, 0))
    return pl.pallas_call(
        _rwkv_state_kernel,
        grid=(2, N_SEQ_BLOCKS),
        in_specs=[tok, tok, mat, mat,
                  pl.BlockSpec((1, RW_BLOCK_CHUNKS, 1, w), lambda d, i: (d, _walk_block(d, i), 0, 0)), state],
        out_specs=[tok, state],
        out_shape=[jax.ShapeDtypeStruct((2, T_ALL, w), F32),
                   jax.ShapeDtypeStruct((N_SEQS, 2, RW_PAIRS, 128, 128), F32)],
        scratch_shapes=[pltpu.VMEM((RW_PAIRS, 128, 128), F32)],
        compiler_params=_params(("parallel", "arbitrary")),
        name="rwkv_state",
    )(r2, y0, mneg, sadd, gc, s0)


def _pack_state(s):
    n = s.shape[0]
    sp = s.reshape(n, 2, RW_PAIRS, 2, RWKV_HEAD, RWKV_HEAD)
    z = jnp.zeros_like(sp[:, :, :, 0])
    top = jnp.concatenate([sp[:, :, :, 0], z], axis=-1)
    bot = jnp.concatenate([z, sp[:, :, :, 1]], axis=-1)
    return jnp.concatenate([top, bot], axis=-2)


def _unpack_state(sp):
    n = sp.shape[0]
    h = RWKV_HEAD
    return jnp.stack([sp[..., 0:h, 0:h], sp[..., h:, h:]], axis=3).reshape(n, 2, RWKV_HEADS, h, h)


def _rwkv_post_kernel(y_ref, bonus_ref, g_ref, lng_ref, lnb_ref, ind_ref, indt_ref, o_ref):
    ind, ind_t = ind_ref[...], indt_ref[...]
    y = y_ref[0] + y_ref[1]
    mu = _head_sum(y, ind, ind_t) * (1.0 / RWKV_HEAD)
    yc = y - mu
    var = _head_sum(yc * yc, ind, ind_t) * (1.0 / RWKV_HEAD)
    yn = yc * lax.rsqrt(var + RWKV_GN_EPS)
    o_ref[...] = ((yn * lng_ref[...] + lnb_ref[...] + bonus_ref[...]) * g_ref[...]).astype(BF16)


def _rwkv_post(y, bonus, g, ln_g, ln_b):
    t = bonus.shape[0]
    w = RWKV_WIDTH
    ind, ind_t = _head_indicator()
    tok = pl.BlockSpec((RW_ROWS, w), lambda i: (i, 0))
    full = lambda shape: pl.BlockSpec(shape, lambda i: (0,) * len(shape))
    return pl.pallas_call(
        _rwkv_post_kernel,
        grid=(t // RW_ROWS,),
        in_specs=[pl.BlockSpec((2, RW_ROWS, w), lambda i: (0, i, 0)), tok, tok, full((1, w)), full((1, w)),
                  full((w, 128)), full((128, w))],
        out_specs=tok,
        out_shape=jax.ShapeDtypeStruct((t, w), BF16),
        compiler_params=_params(("parallel",)),
        name="rwkv_post",
    )(y, bonus, g, ln_g.reshape(1, w), ln_b.reshape(1, w), ind, ind_t)


def _rwkv_mixer(proj_r, lp, s0_sample):
    r, kk, v, g, bonus, lw, kka, kd = _rwkv_prep(proj_r, lp)
    r2, y0, mneg, sadd, gc = _rwkv_chunks(r, kk, v, lw, kka, kd)
    s0 = jnp.concatenate([jnp.zeros((BATCH, 2, RW_PAIRS, 128, 128), F32), _pack_state(s0_sample)], axis=0)
    y, s_fin = _rwkv_state(r2, y0, mneg, sadd, gc, s0)
    return _rwkv_post(y, bonus, g, lp['rwkv_ln_g'], lp['rwkv_ln_b']), _unpack_state(s_fin[:BATCH])


MOE_ROWS = 256
MOE_BLOCKS = T_ALL * EXPERT_TOPK // MOE_ROWS + N_EXPERTS
MOE_PAD_ROWS = MOE_BLOCKS * MOE_ROWS
MOE_FF_TILE = 512
MOE_OUT_TILE = 2048


def _top2_sum(vals):
    best = None
    for a in range(len(vals)):
        for b in range(a + 1, len(vals)):
            s = vals[a] + vals[b]
            best = s if best is None else jnp.maximum(best, s)
    return best


def _first_argmax(vals):
    idx = jnp.zeros(vals[0].shape, jnp.int32)
    best = vals[0]
    for j in range(1, len(vals)):
        upd = vals[j] > best
        idx = jnp.where(upd, j, idx)
        best = jnp.where(upd, vals[j], best)
    return idx, best


def _pick(idx, vals):
    out = vals[-1]
    for j in range(len(vals) - 2, -1, -1):
        out = jnp.where(idx == j, vals[j], out)
    return out


def _ffn_norm_route_kernel(x_ref, g_ref, mod_ref, wh_ref, wl_ref, bias_ref, h_ref, idx_ref, wts_ref):
    y = _rms(x_ref[...], g_ref[...])
    h = y * (1.0 + mod_ref[0, 4:5, :]) + mod_ref[0, 3:4, :]
    h_ref[...] = h
    hh = h.astype(BF16)
    hl = (h - hh.astype(F32)).astype(BF16)
    wh, wl = wh_ref[...], wl_ref[...]
    logits = _dot_nt(wh, hh) + _dot_nt(wh, hl) + _dot_nt(wl, hh)
    scores = jax.nn.sigmoid(logits)
    sel = scores + bias_ref[...]
    gs = EXPERTS_PER_GROUP
    sel_rows = [sel[e:e + 1, :] for e in range(N_EXPERTS)]
    sc_rows = [scores[e:e + 1, :] for e in range(N_EXPERTS)]
    grp, _ = _first_argmax([_top2_sum(sel_rows[gs * g:gs * (g + 1)]) for g in range(N_EXPERT_GROUPS)])
    v = [_pick(grp, [sel_rows[gs * g + j] for g in range(N_EXPERT_GROUPS)]) for j in range(gs)]
    s = [_pick(grp, [sc_rows[gs * g + j] for g in range(N_EXPERT_GROUPS)]) for j in range(gs)]
    i1, _ = _first_argmax(v)
    i2, _ = _first_argmax([jnp.where(i1 == j, -jnp.inf, v[j]) for j in range(gs)])
    w1, w2 = _pick(i1, s), _pick(i2, s)
    tot = w1 + w2
    idx_ref[0:1, :] = grp * gs + i1
    idx_ref[1:2, :] = grp * gs + i2
    wts_ref[0:1, :] = w1 / tot
    wts_ref[1:2, :] = w2 / tot


def _ffn_norm_route(x, g, mod, w_router, router_bias):
    t = x.shape[0]
    wt = w_router.T
    wh = wt.astype(BF16)
    wl = (wt - wh.astype(F32)).astype(BF16)
    full = lambda shape: pl.BlockSpec(shape, lambda i: (0,) * len(shape))
    return pl.pallas_call(
        _ffn_norm_route_kernel,
        grid=(t // NORM_ROWS,),
        in_specs=[
            pl.BlockSpec((NORM_ROWS, D_MODEL), lambda i: (i, 0)),
            full((1, D_MODEL)),
            pl.BlockSpec((1, 6, D_MODEL), lambda i: (_cond_group(i, NORM_ROWS), 0, 0)),
            full((N_EXPERTS, D_MODEL)), full((N_EXPERTS, D_MODEL)), full((N_EXPERTS, 1)),
        ],
        out_specs=[pl.BlockSpec((NORM_ROWS, D_MODEL), lambda i: (i, 0)),
                   pl.BlockSpec((EXPERT_TOPK, NORM_ROWS), lambda i: (0, i)),
                   pl.BlockSpec((EXPERT_TOPK, NORM_ROWS), lambda i: (0, i))],
        out_shape=[jax.ShapeDtypeStruct((t, D_MODEL), F32),
                   jax.ShapeDtypeStruct((EXPERT_TOPK, t), jnp.int32),
                   jax.ShapeDtypeStruct((EXPERT_TOPK, t), F32)],
        compiler_params=_params(("parallel",)),
        name="ffn_norm_route",
    )(x, g.reshape(1, D_MODEL), mod, wh, wl, router_bias.reshape(N_EXPERTS, 1))


def _dispatch_plan(idx):
    t = idx.shape[1]
    flat_e = idx.T.reshape(-1)
    onehot = (flat_e[:, None] == jnp.arange(N_EXPERTS, dtype=jnp.int32)[None, :]).astype(jnp.int32)
    csum = jnp.cumsum(onehot, axis=0)
    rank = jnp.take_along_axis(csum, flat_e[:, None], axis=1)[:, 0] - 1
    counts = csum[-1]
    padded = ((counts + MOE_ROWS - 1) // MOE_ROWS) * MOE_ROWS
    pad_end = jnp.cumsum(padded)
    pad_start = pad_end - padded
    dest = (pad_start[flat_e] + rank).astype(jnp.int32)
    tok = jnp.arange(t * EXPERT_TOPK, dtype=jnp.int32) // EXPERT_TOPK
    row_tok = jnp.zeros((MOE_PAD_ROWS,), jnp.int32).at[dest].set(tok)
    block_row0 = jnp.arange(MOE_BLOCKS, dtype=jnp.int32) * MOE_ROWS
    block_e = jnp.minimum(jnp.sum((pad_end[None, :] <= block_row0[:, None]).astype(jnp.int32), axis=1),
                          N_EXPERTS - 1).astype(jnp.int32)
    n_used = (pad_end[-1] // MOE_ROWS).astype(jnp.int32).reshape(1)
    return row_tok, block_e, n_used, dest


DMA_UNROLL = 8


def _row_copy(src_hbm, row, dst, slot, sem):
    return pltpu.make_async_copy(src_hbm.at[pl.ds(row, 1), :], dst.at[pl.ds(slot, 1), :], sem)


def _start_rows(src_hbm, idx_ref, idx0, stride, n, dst, sem):
    def body(r, c):
        _row_copy(src_hbm, idx_ref[idx0 + stride * r], dst, r, sem).start()
        return c
    lax.fori_loop(0, n, body, 0, unroll=DMA_UNROLL)


def _wait_rows(src_hbm, n, dst, sem):
    def body(r, c):
        _row_copy(src_hbm, 0, dst, r, sem).wait()
        return c
    lax.fori_loop(0, n, body, 0, unroll=DMA_UNROLL)


def _gather_rows_kernel(tok_ref, nu_ref, h_hbm, o_ref, buf, sem):
    i = pl.program_id(0)
    n = nu_ref[0]

    @pl.when((i == 0) & (n > 0))
    def _():
        _start_rows(h_hbm, tok_ref, 0, 1, MOE_ROWS, buf.at[0], sem.at[0])

    @pl.when(i + 1 < n)
    def _():
        nxt = (i + 1) % 2
        _start_rows(h_hbm, tok_ref, (i + 1) * MOE_ROWS, 1, MOE_ROWS, buf.at[nxt], sem.at[nxt])

    @pl.when(i < n)
    def _():
        cur = i % 2
        _wait_rows(h_hbm, MOE_ROWS, buf.at[cur], sem.at[cur])
        o_ref[...] = buf[cur].astype(BF16)

    @pl.when(i >= n)
    def _():
        o_ref[...] = jnp.zeros_like(o_ref)


def _gather_rows(h, row_tok, n_used):
    return pl.pallas_call(
        _gather_rows_kernel,
        grid_spec=pltpu.PrefetchScalarGridSpec(
            num_scalar_prefetch=2,
            grid=(MOE_BLOCKS,),
            in_specs=[pl.BlockSpec(memory_space=pl.ANY)],
            out_specs=pl.BlockSpec((MOE_ROWS, D_MODEL), lambda i, tok, nu: (i, 0)),
            scratch_shapes=[pltpu.VMEM((2, MOE_ROWS, D_MODEL), F32), pltpu.SemaphoreType.DMA((2,))],
        ),
        out_shape=jax.ShapeDtypeStruct((MOE_PAD_ROWS, D_MODEL), BF16),
        compiler_params=_params(("arbitrary",)),
        name="moe_gather",
    )(row_tok, n_used, h)


def _expert_up_kernel(be_ref, nu_ref, x_ref, w1_ref, w3_ref, o_ref):
    i = pl.program_id(1)

    @pl.when(i < nu_ref[0])
    def _():
        x = x_ref[...]
        a = _dot(x, w1_ref[0, 0].astype(BF16))
        b = _dot(x, w3_ref[0, 0].astype(BF16))
        o_ref[...] = (a * jax.nn.sigmoid(a) * b).astype(BF16)

    @pl.when(i >= nu_ref[0])
    def _():
        o_ref[...] = jnp.zeros_like(o_ref)


def _expert_up(xb, w_gate, w_up, layer, block_e, n_used):
    wspec = pl.BlockSpec((1, 1, D_MODEL, MOE_FF_TILE), lambda j, i, be, nu: (layer, be[i], 0, j))
    return pl.pallas_call(
        _expert_up_kernel,
        grid_spec=pltpu.PrefetchScalarGridSpec(
            num_scalar_prefetch=2,
            grid=(EXPERT_FF // MOE_FF_TILE, MOE_BLOCKS),
            in_specs=[pl.BlockSpec((MOE_ROWS, D_MODEL), lambda j, i, be, nu: (i, 0)), wspec, wspec],
            out_specs=pl.BlockSpec((MOE_ROWS, MOE_FF_TILE), lambda j, i, be, nu: (i, j)),
        ),
        out_shape=jax.ShapeDtypeStruct((MOE_PAD_ROWS, EXPERT_FF), BF16),
        compiler_params=_params(("parallel", "arbitrary")),
        name="moe_up",
    )(block_e, n_used, xb, w_gate, w_up)


def _expert_down_kernel(be_ref, nu_ref, h_ref, w2_ref, o_ref):
    i = pl.program_id(1)

    @pl.when(i < nu_ref[0])
    def _():
        o_ref[...] = _dot(h_ref[...], w2_ref[0, 0].astype(BF16))

    @pl.when(i >= nu_ref[0])
    def _():
        o_ref[...] = jnp.zeros_like(o_ref)


def _expert_down(hmid, w_down, layer, block_e, n_used):
    return pl.pallas_call(
        _expert_down_kernel,
        grid_spec=pltpu.PrefetchScalarGridSpec(
            num_scalar_prefetch=2,
            grid=(D_MODEL // MOE_OUT_TILE, MOE_BLOCKS),
            in_specs=[
                pl.BlockSpec((MOE_ROWS, EXPERT_FF), lambda j, i, be, nu: (i, 0)),
                pl.BlockSpec((1, 1, EXPERT_FF, MOE_OUT_TILE), lambda j, i, be, nu: (layer, be[i], 0, j)),
            ],
            out_specs=pl.BlockSpec((MOE_ROWS, MOE_OUT_TILE), lambda j, i, be, nu: (i, j)),
        ),
        out_shape=jax.ShapeDtypeStruct((MOE_PAD_ROWS, D_MODEL), F32),
        compiler_params=_params(("parallel", "arbitrary")),
        name="moe_down",
    )(block_e, n_used, hmid, w_down)


COMBINE_ROWS = 256


def _combine_kernel(pos_ref, yb_hbm, x_ref, w_ref, mod_ref, o_ref, buf, sem):
    i = pl.program_id(0)
    n = pl.num_programs(0)
    per_block = COMBINE_ROWS * EXPERT_TOPK

    def start(blk, slot):
        for k in range(EXPERT_TOPK):
            _start_rows(yb_hbm, pos_ref, blk * per_block + k, EXPERT_TOPK, COMBINE_ROWS, buf.at[slot, k],
                        sem.at[slot])

    @pl.when(i == 0)
    def _():
        start(0, 0)

    @pl.when(i + 1 < n)
    def _():
        start(i + 1, (i + 1) % 2)

    cur = i % 2
    for k in range(EXPERT_TOPK):
        _wait_rows(yb_hbm, COMBINE_ROWS, buf.at[cur, k], sem.at[cur])
    y = buf[cur, 0] * w_ref[:, 0:1] + buf[cur, 1] * w_ref[:, 1:2]
    o_ref[...] = x_ref[...] + mod_ref[0, 5:6, :] * y


def _combine(yb, dest, wts, x, mod):
    t = x.shape[0]
    return pl.pallas_call(
        _combine_kernel,
        grid_spec=pltpu.PrefetchScalarGridSpec(
            num_scalar_prefetch=1,
            grid=(t // COMBINE_ROWS,),
            in_specs=[
                pl.BlockSpec(memory_space=pl.ANY),
                pl.BlockSpec((COMBINE_ROWS, D_MODEL), lambda i, pos: (i, 0)),
                pl.BlockSpec((COMBINE_ROWS, EXPERT_TOPK), lambda i, pos: (i, 0)),
                pl.BlockSpec((1, 6, D_MODEL), lambda i, pos: (_cond_group(i, COMBINE_ROWS), 0, 0)),
            ],
            out_specs=pl.BlockSpec((COMBINE_ROWS, D_MODEL), lambda i, pos: (i, 0)),
            scratch_shapes=[pltpu.VMEM((2, EXPERT_TOPK, COMBINE_ROWS, D_MODEL), F32),
                            pltpu.SemaphoreType.DMA((2,))],
        ),
        out_shape=jax.ShapeDtypeStruct((t, D_MODEL), F32),
        compiler_params=_params(("arbitrary",)),
        name="moe_combine",
    )(dest, yb, x, wts.T, mod)


def _moe_residual(x, g, mod, lp_router, w_gate, w_up, w_down, layer):
    h, idx, wts = _ffn_norm_route(x, g, mod, *lp_router)
    row_tok, block_e, n_used, dest = _dispatch_plan(idx)
    xb = _gather_rows(h, row_tok, n_used)
    hmid = _expert_up(xb, w_gate, w_up, layer, block_e, n_used)
    yb = _expert_down(hmid, w_down, layer, block_e, n_used)
    return _combine(yb, dest, wts, x, mod)


def _layer_params(args, layer):
    return {k: v[layer] for k, v in args.items()}


def kernel(x_prompt, x_sample, cache_na_k, cache_na_v, cache_gqa_k, cache_gqa_v, state_ssm, state_rwkv, c, c_ctx,
           w_ada, b_ada, norm_mix, norm_ffn, w_in, na_rpb, gqa_q_norm, gqa_k_norm, ssm_lam_re, ssm_lam_im,
           ssm_log_dt, ssm_b_re, ssm_b_im, ssm_c_re, ssm_c_im, ssm_d, ssm_w_glu, rwkv_mu, rwkv_w0, rwkv_w2, rwkv_a0,
           rwkv_a2, rwkv_k_k, rwkv_k_a, rwkv_r_k, rwkv_g2, rwkv_ln_g, rwkv_ln_b, w_branch, w_out, w_router,
           router_bias, w_exp_gate, w_exp_up, w_exp_down, norm_final):
    x = jnp.concatenate([x_prompt.reshape(T_PROMPT, D_MODEL), x_sample.reshape(T_SAMPLE, D_MODEL)], axis=0)
    cond = jnp.concatenate([c_ctx[None, :], c, jnp.zeros((COND_PAD - N_COND, D_MODEL), F32)], axis=0)
    mod_all = _ada_all(cond, w_ada, b_ada).reshape(DEPTH, COND_PAD, 6, D_MODEL)
    cos, sin = _rope_tables()
    rwkv_args = dict(rwkv_mu=rwkv_mu, rwkv_w0=rwkv_w0, rwkv_w2=rwkv_w2, rwkv_a0=rwkv_a0, rwkv_a2=rwkv_a2,
                     rwkv_k_k=rwkv_k_k, rwkv_k_a=rwkv_k_a, rwkv_r_k=rwkv_r_k.reshape(DEPTH, RWKV_WIDTH),
                     rwkv_g2=rwkv_g2, rwkv_ln_g=rwkv_ln_g, rwkv_ln_b=rwkv_ln_b)
    caches = [[] for _ in range(6)]
    for l in range(DEPTH):
        mod = mod_all[l]
        h = _norm_mod(x, norm_mix[l], mod, 0)
        proj = _in_proj(h, w_in, l, 0, OFF_RWKV)
        proj_r = _in_proj(h, w_in, l, OFF_RWKV, RWKV_IN_W)
        proj_g = _in_proj(h, w_in, l, OFF_GATE, N_BRANCH * D_MODEL)

        y_na_p, nk, nv = _ctx_na(proj)
        y_gqa_p, gk, gv = _ctx_gqa(proj, gqa_q_norm[l], gqa_k_norm[l])
        y_na_s = _lat_na(proj, cache_na_k, cache_na_v, l, _na_band_bias(na_rpb[l]))
        y_gqa_s = _lat_gqa(proj, cache_gqa_k, cache_gqa_v, l, gqa_q_norm[l], gqa_k_norm[l], cos, sin)

        a, bb, cc = _s5_params(ssm_lam_re[l], ssm_lam_im[l], ssm_log_dt[l], ssm_b_re[l], ssm_b_im[l],
                               ssm_c_re[l], ssm_c_im[l])
        h0 = jnp.concatenate([jnp.zeros((BATCH, 2, 2, S5_SLABS, 128), F32),
                              state_ssm[:, l].reshape(DEC_BATCH, 2, 2, S5_SLABS, 128)], axis=0)
        y_scan, ssm_fin = _s5_scan(proj, a, bb, cc, h0)
        ssm_state = ssm_fin[:BATCH]
        y_ssm = _s5_glu(proj, 0, y_scan, ssm_d, ssm_w_glu, l)

        y_rwkv, rwkv_state = _rwkv_mixer(proj_r, _layer_params(rwkv_args, l), state_rwkv[:, l])

        merged = _branch_merge((y_na_p, y_na_s), (y_gqa_p, y_gqa_s), y_ssm, y_rwkv, proj_g, w_branch, l)
        x = _out_proj_residual(merged, w_out, l, x, mod, 2)
        x = _moe_residual(x, norm_ffn[l], mod, (w_router, router_bias), w_exp_gate, w_exp_up, w_exp_down, l)

        for lst, val in zip(caches, (nk, nv, gk, gv, ssm_state.reshape(BATCH, 2, 2, SSM_GROUPS, SSM_STATE),
                                     rwkv_state)):
            lst.append(val)

    y = _final_norm(x, norm_final)
    outs = [jnp.stack(lst, axis=1) for lst in caches]
    return (y[:T_PROMPT].reshape(BATCH, SEQ, D_MODEL), y[T_PROMPT:].reshape(DEC_BATCH, DEC_SEQ, D_MODEL), *outs)
```

```python
import functools

import numpy as np
import jax
import jax.numpy as jnp
from jax import lax
from jax.experimental import pallas as pl
from jax.experimental.pallas import tpu as pltpu

F32 = jnp.float32
BF16 = jnp.bfloat16

D_MODEL = 4096
BATCH = 16
SEQ = 256
DEPTH = 4
DEC_BATCH = 2
DEC_SEQ = 1024
PAST_LEN = 512
GRID_W = 64
HEAD_DIM = 128
EPS = 1e-6
NEG = -1e30
NA_HEADS = 8
NA_WIN_ROWS = 8
NA_WIN_COLS = 16
NA_W = NA_HEADS * HEAD_DIM
GQA_HEADS = 8
GQA_KV_HEADS = 2
GQA_GROUP = GQA_HEADS // GQA_KV_HEADS
GQA_Q_W = GQA_HEADS * HEAD_DIM
GQA_KV_W = GQA_KV_HEADS * HEAD_DIM
GQA_QKV_W = GQA_Q_W + 2 * GQA_KV_W
ROPE_THETA = 10000.0
SSM_WIDTH = 1024
SSM_GROUP = 16
SSM_GROUPS = SSM_WIDTH // SSM_GROUP
SSM_STATE = 64
RWKV_WIDTH = 1024
RWKV_HEAD = 64
RWKV_HEADS = RWKV_WIDTH // RWKV_HEAD
DECAY_RANK = 64
ICLR_RANK = 64
GATE_RANK = 128
RWKV_IN_W = 3 * RWKV_WIDTH + DECAY_RANK + ICLR_RANK + GATE_RANK
RWKV_GN_EPS = 64e-5
N_BRANCH = 4
OFF_NA = 0
OFF_GQA = 3 * NA_W
OFF_SSM = OFF_GQA + GQA_QKV_W
OFF_RWKV = OFF_SSM + SSM_WIDTH
OFF_GATE = OFF_RWKV + RWKV_IN_W
IN_W = OFF_GATE + N_BRANCH * D_MODEL
N_EXPERTS = 16
N_EXPERT_GROUPS = 4
EXPERTS_PER_GROUP = N_EXPERTS // N_EXPERT_GROUPS
EXPERT_TOPK = 2
EXPERT_FF = 1024

T_PROMPT = BATCH * SEQ
T_SAMPLE = DEC_BATCH * DEC_SEQ
T_ALL = T_PROMPT + T_SAMPLE
N_COND = 1 + DEC_BATCH
COND_PAD = 8
ATT_SCALE = HEAD_DIM ** -0.5

VMEM_LIMIT = 56 * 1024 * 1024


def _params(sem):
    return pltpu.CompilerParams(dimension_semantics=sem, vmem_limit_bytes=VMEM_LIMIT)


def _cond_group(i, rows_per_block):
    n_prompt_blocks = T_PROMPT // rows_per_block
    blocks_per_seq = DEC_SEQ // rows_per_block
    return jnp.where(i < n_prompt_blocks, 0, 1 + (i - n_prompt_blocks) // blocks_per_seq)


def _dot(a, b):
    return jnp.dot(a, b, preferred_element_type=F32)


def _dot_nt(a, b):
    return lax.dot_general(a, b, (((1,), (1,)), ((), ())), preferred_element_type=F32)


def _dot_tn(a, b):
    return lax.dot_general(a, b, (((0,), (0,)), ((), ())), preferred_element_type=F32)


def _split3(x):
    hi = x.astype(BF16)
    r1 = x - hi.astype(F32)
    mid = r1.astype(BF16)
    lo = (r1 - mid.astype(F32)).astype(BF16)
    return hi, mid, lo


def _dot_exact_rhs(a_bf16, x):
    hi, mid, lo = _split3(x)
    return _dot(a_bf16, hi) + _dot(a_bf16, mid) + _dot(a_bf16, lo)


def _dot_exact_lhs(x, b_bf16):
    hi, mid, lo = _split3(x)
    return _dot(hi, b_bf16) + _dot(mid, b_bf16) + _dot(lo, b_bf16)


def _rms(x, g):
    return x * lax.rsqrt(jnp.mean(x * x, axis=-1, keepdims=True) + EPS) * g


def _softmax_rows(s):
    m = jnp.max(s, axis=-1, keepdims=True)
    e = jnp.exp(s - m)
    return e / jnp.sum(e, axis=-1, keepdims=True)


def _ada_kernel(c_ref, w_ref, b_ref, o_ref):
    c = c_ref[...]
    s = (c * jax.nn.sigmoid(c)).astype(BF16)
    o_ref[0] = _dot(s, w_ref[0].astype(BF16)) + b_ref[0]


def _ada_all(cond, w_ada, b_ada):
    tn = 512
    n6 = 6 * D_MODEL
    return pl.pallas_call(
        _ada_kernel,
        grid=(DEPTH, n6 // tn),
        in_specs=[
            pl.BlockSpec((COND_PAD, D_MODEL), lambda l, j: (0, 0)),
            pl.BlockSpec((1, D_MODEL, tn), lambda l, j: (l, 0, j)),
            pl.BlockSpec((1, 1, tn), lambda l, j: (l, 0, j)),
        ],
        out_specs=pl.BlockSpec((1, COND_PAD, tn), lambda l, j: (l, 0, j)),
        out_shape=jax.ShapeDtypeStruct((DEPTH, COND_PAD, n6), F32),
        compiler_params=_params(("parallel", "parallel")),
        name="ada",
    )(cond, w_ada, b_ada.reshape(DEPTH, 1, n6))


NORM_ROWS = 256


def _norm_mod_kernel(x_ref, g_ref, mod_ref, o_ref, *, shift_idx):
    y = _rms(x_ref[...], g_ref[...])
    scale = mod_ref[0, shift_idx + 1:shift_idx + 2, :]
    shift = mod_ref[0, shift_idx:shift_idx + 1, :]
    o_ref[...] = (y * (1.0 + scale) + shift).astype(o_ref.dtype)


def _norm_mod(x, g, mod, shift_idx, out_dtype=BF16):
    t = x.shape[0]
    return pl.pallas_call(
        functools.partial(_norm_mod_kernel, shift_idx=shift_idx),
        grid=(t // NORM_ROWS,),
        in_specs=[
            pl.BlockSpec((NORM_ROWS, D_MODEL), lambda i: (i, 0)),
            pl.BlockSpec((1, D_MODEL), lambda i: (0, 0)),
            pl.BlockSpec((1, 6, D_MODEL), lambda i: (_cond_group(i, NORM_ROWS), 0, 0)),
        ],
        out_specs=pl.BlockSpec((NORM_ROWS, D_MODEL), lambda i: (i, 0)),
        out_shape=jax.ShapeDtypeStruct((t, D_MODEL), out_dtype),
        compiler_params=_params(("parallel",)),
        name="norm_mod",
    )(x, g.reshape(1, D_MODEL), mod)


def _final_norm_kernel(x_ref, g_ref, o_ref):
    o_ref[...] = _rms(x_ref[...], g_ref[...])


def _final_norm(x, g):
    t = x.shape[0]
    return pl.pallas_call(
        _final_norm_kernel,
        grid=(t // NORM_ROWS,),
        in_specs=[pl.BlockSpec((NORM_ROWS, D_MODEL), lambda i: (i, 0)),
                  pl.BlockSpec((1, D_MODEL), lambda i: (0, 0))],
        out_specs=pl.BlockSpec((NORM_ROWS, D_MODEL), lambda i: (i, 0)),
        out_shape=jax.ShapeDtypeStruct((t, D_MODEL), F32),
        compiler_params=_params(("parallel",)),
        name="final_norm",
    )(x, g.reshape(1, D_MODEL))


MM_TM = 1024
MM_TN = 256
IN_TM = 2048


def _proj_kernel(x_ref, w_ref, o_ref):
    o_ref[...] = _dot(x_ref[...], w_ref[0].astype(BF16))


def _in_proj(h, w_in, layer, col0, n_cols):
    t = h.shape[0]
    blk0 = col0 // MM_TN
    return pl.pallas_call(
        _proj_kernel,
        grid=(t // IN_TM, n_cols // MM_TN),
        in_specs=[
            pl.BlockSpec((IN_TM, D_MODEL), lambda i, j: (i, 0)),
            pl.BlockSpec((1, D_MODEL, MM_TN), lambda i, j: (layer, 0, blk0 + j)),
        ],
        out_specs=pl.BlockSpec((IN_TM, MM_TN), lambda i, j: (i, j)),
        out_shape=jax.ShapeDtypeStruct((t, n_cols), F32),
        compiler_params=_params(("parallel", "parallel")),
        name="in_proj",
    )(h, w_in)


def _merge_kernel(y0p, y0s, y1p, y1s, y2, y3, wb_ref, g0, g1, g2, g3, o_ref):
    in_prompt = pl.program_id(0) < T_PROMPT // MM_TM
    ys = (jnp.where(in_prompt, y0p[...], y0s[...]), jnp.where(in_prompt, y1p[...], y1s[...]), y2[...], y3[...])
    acc = None
    for n, (y, g) in enumerate(zip(ys, (g0, g1, g2, g3))):
        br = _dot(y, wb_ref[0, n].astype(BF16))
        term = jax.nn.sigmoid(g[...]) * br
        acc = term if acc is None else acc + term
    o_ref[...] = acc.astype(BF16)


def _branch_merge(y_na, y_gqa, y_ssm, y_rwkv, proj, w_branch, layer):
    t = proj.shape[0]
    bw = y_ssm.shape[1]
    per_branch = D_MODEL // MM_TN
    prompt_blocks = T_PROMPT // MM_TM
    y_spec = pl.BlockSpec((MM_TM, bw), lambda i, j: (i, 0))
    yp_spec = pl.BlockSpec((MM_TM, bw), lambda i, j: (jnp.minimum(i, prompt_blocks - 1), 0))
    ys_spec = pl.BlockSpec((MM_TM, bw), lambda i, j: (jnp.maximum(i - prompt_blocks, 0), 0))

    def gate_spec(n):
        return pl.BlockSpec((MM_TM, MM_TN), lambda i, j: (i, n * per_branch + j))

    return pl.pallas_call(
        _merge_kernel,
        grid=(t // MM_TM, D_MODEL // MM_TN),
        in_specs=[yp_spec, ys_spec, yp_spec, ys_spec, y_spec, y_spec]
        + [pl.BlockSpec((1, N_BRANCH, bw, MM_TN), lambda i, j: (layer, 0, 0, j))]
        + [gate_spec(n) for n in range(N_BRANCH)],
        out_specs=pl.BlockSpec((MM_TM, MM_TN), lambda i, j: (i, j)),
        out_shape=jax.ShapeDtypeStruct((t, D_MODEL), BF16),
        compiler_params=_params(("parallel", "parallel")),
        name="branch_merge",
    )(*y_na, *y_gqa, y_ssm, y_rwkv, w_branch, proj, proj, proj, proj)


def _out_proj_kernel(m_ref, w_ref, x_ref, mod_ref, o_ref, *, gate_idx):
    out = _dot(m_ref[...], w_ref[0].astype(BF16))
    o_ref[...] = x_ref[...] + mod_ref[0, gate_idx:gate_idx + 1, :] * out


def _out_proj_residual(merged, w_out, layer, x, mod, gate_idx):
    t = x.shape[0]
    return pl.pallas_call(
        functools.partial(_out_proj_kernel, gate_idx=gate_idx),
        grid=(t // MM_TM, D_MODEL // MM_TN),
        in_specs=[
            pl.BlockSpec((MM_TM, D_MODEL), lambda i, j: (i, 0)),
            pl.BlockSpec((1, D_MODEL, MM_TN), lambda i, j: (layer, 0, j)),
            pl.BlockSpec((MM_TM, MM_TN), lambda i, j: (i, j)),
            pl.BlockSpec((1, 6, MM_TN), lambda i, j: (_cond_group(i, MM_TM), 0, j)),
        ],
        out_specs=pl.BlockSpec((MM_TM, MM_TN), lambda i, j: (i, j)),
        out_shape=jax.ShapeDtypeStruct((t, D_MODEL), F32),
        compiler_params=_params(("parallel", "parallel")),
        name="out_proj",
    )(merged, w_out, x, mod)


def _attend(q, k, v):
    p = _softmax_rows(_dot_nt(q, k) * ATT_SCALE)
    return _dot(p.astype(BF16), v)


def _ctx_na_kernel(q_ref, k_ref, v_ref, y_ref, ko_ref, vo_ref):
    for h in range(NA_HEADS):
        sl = slice(HEAD_DIM * h, HEAD_DIM * (h + 1))
        kf = k_ref[:, sl]
        vf = v_ref[:, sl]
        y = _attend(q_ref[:, sl].astype(BF16), kf.astype(BF16), vf.astype(BF16))
        y_ref[:, sl] = y.astype(BF16)
        ko_ref[0, h] = kf
        vo_ref[0, h] = vf


def _ctx_na(proj):
    cache = jax.ShapeDtypeStruct((BATCH, NA_HEADS, SEQ, HEAD_DIM), F32)
    cache_spec = pl.BlockSpec((1, NA_HEADS, SEQ, HEAD_DIM), lambda b: (b, 0, 0, 0))
    return pl.pallas_call(
        _ctx_na_kernel,
        grid=(BATCH,),
        in_specs=[pl.BlockSpec((SEQ, NA_W), lambda b, c=c: (b, c)) for c in range(3)],
        out_specs=[pl.BlockSpec((SEQ, NA_W), lambda b: (b, 0)), cache_spec, cache_spec],
        out_shape=[jax.ShapeDtypeStruct((T_PROMPT, NA_W), BF16), cache, cache],
        compiler_params=_params(("parallel",)),
        name="ctx_na",
    )(proj, proj, proj)


def _ctx_gqa_kernel(q_ref, k_ref, v_ref, gq_ref, gk_ref, y_ref, ko_ref, vo_ref):
    for kv in range(GQA_KV_HEADS):
        sl = slice(HEAD_DIM * kv, HEAD_DIM * (kv + 1))
        kn = _rms(k_ref[:, sl], gk_ref[...])
        vf = v_ref[:, sl]
        ko_ref[0, kv] = kn
        vo_ref[0, kv] = vf
        kb = kn.astype(BF16)
        vb = vf.astype(BF16)
        for g in range(GQA_GROUP):
            h = kv * GQA_GROUP + g
            hs = slice(HEAD_DIM * h, HEAD_DIM * (h + 1))
            qn = _rms(q_ref[:, hs], gq_ref[...])
            y_ref[:, hs] = _attend(qn.astype(BF16), kb, vb).astype(BF16)


def _ctx_gqa(proj, gq, gk):
    cache = jax.ShapeDtypeStruct((BATCH, GQA_KV_HEADS, SEQ, HEAD_DIM), F32)
    cache_spec = pl.BlockSpec((1, GQA_KV_HEADS, SEQ, HEAD_DIM), lambda b: (b, 0, 0, 0))
    norm_spec = pl.BlockSpec((1, HEAD_DIM), lambda b: (0, 0))
    return pl.pallas_call(
        _ctx_gqa_kernel,
        grid=(BATCH,),
        in_specs=[
            pl.BlockSpec((SEQ, GQA_Q_W), lambda b: (b, OFF_GQA // GQA_Q_W)),
            pl.BlockSpec((SEQ, GQA_KV_W), lambda b: (b, (OFF_GQA + GQA_Q_W) // GQA_KV_W)),
            pl.BlockSpec((SEQ, GQA_KV_W), lambda b: (b, (OFF_GQA + GQA_Q_W) // GQA_KV_W + 1)),
            norm_spec, norm_spec,
        ],
        out_specs=[pl.BlockSpec((SEQ, GQA_Q_W), lambda b: (b, 0)), cache_spec, cache_spec],
        out_shape=[jax.ShapeDtypeStruct((T_PROMPT, GQA_Q_W), BF16), cache, cache],
        compiler_params=_params(("parallel",)),
        name="ctx_gqa",
    )(proj, proj, proj, gq.reshape(1, HEAD_DIM), gk.reshape(1, HEAD_DIM))


LAT_Q_ROWS = 256
LAT_ROW0 = T_PROMPT // DEC_SEQ


def _rope_tables():
    t = np.arange(DEC_SEQ)
    half = HEAD_DIM // 2
    inv = jnp.asarray(ROPE_THETA, F32) ** (-jnp.arange(0, half, 2, dtype=F32) / half)
    rows = jnp.asarray(t // GRID_W, F32)
    cols = jnp.asarray(t % GRID_W, F32)
    ang_r = rows[:, None] * inv[None, :]
    ang_c = cols[:, None] * inv[None, :]
    cos = jnp.concatenate([jnp.cos(ang_r)] * 2 + [jnp.cos(ang_c)] * 2, axis=-1)
    sin = jnp.concatenate([-jnp.sin(ang_r), jnp.sin(ang_r), -jnp.sin(ang_c), jnp.sin(ang_c)], axis=-1)
    return cos, sin


def _rope(x, cos, sin):
    q = HEAD_DIM // 4
    lane = lax.broadcasted_iota(jnp.int32, x.shape, 1)
    partner = jnp.where((lane % (2 * q)) < q, pltpu.roll(x, HEAD_DIM - q, 1), pltpu.roll(x, q, 1))
    return x * cos + partner * sin


def _lat_gqa_kernel(q_ref, k_ref, v_ref, ck_ref, cv_ref, gq_ref, gk_ref, cos_ref, sin_ref, y_ref, kall, vall):
    cos = cos_ref[...]
    sin = sin_ref[...]
    kall[0:PAST_LEN, :] = ck_ref[0, 0, 0].astype(BF16)
    vall[0:PAST_LEN, :] = cv_ref[0, 0, 0].astype(BF16)
    kall[PAST_LEN:, :] = _rope(_rms(k_ref[...], gk_ref[...]), cos, sin).astype(BF16)
    vall[PAST_LEN:, :] = v_ref[...].astype(BF16)
    for g in range(GQA_GROUP):
        hs = slice(HEAD_DIM * g, HEAD_DIM * (g + 1))
        for qb in range(DEC_SEQ // LAT_Q_ROWS):
            rs = slice(LAT_Q_ROWS * qb, LAT_Q_ROWS * (qb + 1))
            qn = _rope(_rms(q_ref[rs, hs], gq_ref[...]), cos[rs], sin[rs])
            y_ref[rs, hs] = _attend(qn.astype(BF16), kall[...], vall[...]).astype(BF16)


def _lat_gqa(proj, cache_k, cache_v, layer, gq, gk, cos, sin):
    group_w = GQA_GROUP * HEAD_DIM
    kcol = (OFF_GQA + GQA_Q_W) // HEAD_DIM
    cache_spec = pl.BlockSpec((1, 1, 1, PAST_LEN, HEAD_DIM), lambda b, kv: (b, layer, kv, 0, 0))
    norm_spec = pl.BlockSpec((1, HEAD_DIM), lambda b, kv: (0, 0))
    tab_spec = pl.BlockSpec((DEC_SEQ, HEAD_DIM), lambda b, kv: (0, 0))
    return pl.pallas_call(
        _lat_gqa_kernel,
        grid=(DEC_BATCH, GQA_KV_HEADS),
        in_specs=[
            pl.BlockSpec((DEC_SEQ, group_w), lambda b, kv: (LAT_ROW0 + b, OFF_GQA // group_w + kv)),
            pl.BlockSpec((DEC_SEQ, HEAD_DIM), lambda b, kv: (LAT_ROW0 + b, kcol + kv)),
            pl.BlockSpec((DEC_SEQ, HEAD_DIM), lambda b, kv: (LAT_ROW0 + b, kcol + GQA_KV_HEADS + kv)),
            cache_spec, cache_spec, norm_spec, norm_spec, tab_spec, tab_spec,
        ],
        out_specs=pl.BlockSpec((DEC_SEQ, group_w), lambda b, kv: (b, kv)),
        out_shape=jax.ShapeDtypeStruct((T_SAMPLE, GQA_Q_W), BF16),
        scratch_shapes=[pltpu.VMEM((PAST_LEN + DEC_SEQ, HEAD_DIM), BF16)] * 2,
        compiler_params=_params(("parallel", "parallel")),
        name="lat_gqa",
    )(proj, proj, proj, cache_k, cache_v, gq.reshape(1, HEAD_DIM), gk.reshape(1, HEAD_DIM), cos, sin)


NA_ROWS = DEC_SEQ // GRID_W
NA_WR = min(NA_WIN_ROWS, NA_ROWS)
NA_BAND = NA_WR * GRID_W
NA_DR = 2 * NA_WIN_ROWS - 1
NA_DC = 2 * NA_WIN_COLS - 1


def _na_row_window(r):
    r0 = min(max(r - NA_WR // 2, 0), NA_ROWS - NA_WR)
    return r0, r0 - r + NA_WIN_ROWS - 1


def _na_table_kernel(rpb_ref, sel_ref, ok_ref, o_ref):
    picked = _dot_exact_lhs(rpb_ref[...], sel_ref[...])
    o_ref[...] = jnp.where(ok_ref[...] > 0.5, picked, NEG)


def _na_band_bias(rpb):
    c = np.arange(GRID_W)
    dc = c[None, :] - c[:, None] + NA_WIN_COLS - 1
    c0 = np.clip(c - NA_WIN_COLS // 2, 0, GRID_W - NA_WIN_COLS)
    ok = (c[None, :] >= c0[:, None]) & (c[None, :] < c0[:, None] + NA_WIN_COLS)
    sel = np.zeros((128, GRID_W * GRID_W), np.float32)
    flat_dc = np.clip(dc, 0, NA_DC - 1).reshape(-1)
    sel[flat_dc, np.arange(GRID_W * GRID_W)] = 1.0
    n = NA_HEADS * NA_DR
    rpb2 = jnp.pad(rpb.reshape(n, NA_DC).astype(F32), ((0, 0), (0, 128 - NA_DC)))
    full = lambda shape: pl.BlockSpec(shape, lambda: (0,) * len(shape))
    table = pl.pallas_call(
        _na_table_kernel,
        in_specs=[full((n, 128)), full((128, GRID_W * GRID_W)), full((1, GRID_W * GRID_W))],
        out_specs=full((n, GRID_W * GRID_W)),
        out_shape=jax.ShapeDtypeStruct((n, GRID_W * GRID_W), F32),
        name="na_table",
    )(rpb2, jnp.asarray(sel, BF16), jnp.asarray(ok.reshape(1, -1), F32))
    table = table.reshape(NA_HEADS, NA_DR, GRID_W, GRID_W)
    bands = [table[:, a0:a0 + NA_WR].transpose(0, 2, 1, 3).reshape(NA_HEADS, GRID_W, NA_BAND)
             for a0 in range(NA_DR - NA_WR + 1)]
    return jnp.stack(bands, axis=1)


def _lat_na_kernel(q_ref, k_ref, v_ref, ck_ref, cv_ref, bias_ref, y_ref):
    kb = k_ref[...].astype(BF16)
    vb = v_ref[...].astype(BF16)
    ck = ck_ref[0, 0, 0].astype(BF16)
    cv = cv_ref[0, 0, 0].astype(BF16)
    for r in range(NA_ROWS):
        r0, a0 = _na_row_window(r)
        rs = slice(GRID_W * r, GRID_W * (r + 1))
        ks = slice(GRID_W * r0, GRID_W * r0 + NA_BAND)
        q = q_ref[rs, :].astype(BF16)
        s_loc = _dot_nt(q, kb[ks]) * ATT_SCALE + bias_ref[0, a0]
        s_ctx = _dot_nt(q, ck) * ATT_SCALE
        m = jnp.maximum(jnp.max(s_loc, axis=-1, keepdims=True), jnp.max(s_ctx, axis=-1, keepdims=True))
        e_loc = jnp.exp(s_loc - m)
        e_ctx = jnp.exp(s_ctx - m)
        den = jnp.sum(e_loc, axis=-1, keepdims=True) + jnp.sum(e_ctx, axis=-1, keepdims=True)
        out = _dot((e_loc / den).astype(BF16), vb[ks]) + _dot((e_ctx / den).astype(BF16), cv)
        y_ref[rs, :] = out.astype(BF16)


def _lat_na(proj, cache_k, cache_v, layer, bias):
    cache_spec = pl.BlockSpec((1, 1, 1, PAST_LEN, HEAD_DIM), lambda h, b: (b, layer, h, 0, 0))

    def qkv_spec(c):
        return pl.BlockSpec((DEC_SEQ, HEAD_DIM), lambda h, b: (LAT_ROW0 + b, c * NA_HEADS + h))

    return pl.pallas_call(
        _lat_na_kernel,
        grid=(NA_HEADS, DEC_BATCH),
        in_specs=[qkv_spec(0), qkv_spec(1), qkv_spec(2), cache_spec, cache_spec,
                  pl.BlockSpec((1, NA_DR - NA_WR + 1, GRID_W, NA_BAND), lambda h, b: (h, 0, 0, 0))],
        out_specs=pl.BlockSpec((DEC_SEQ, HEAD_DIM), lambda h, b: (b, h)),
        out_shape=jax.ShapeDtypeStruct((T_SAMPLE, NA_W), BF16),
        compiler_params=_params(("parallel", "parallel")),
        name="lat_na",
    )(proj, proj, proj, cache_k, cache_v, bias)


SEQ_BLOCK = 256
N_SEQ_BLOCKS = T_ALL // SEQ_BLOCK
N_SEQS = BATCH + DEC_BATCH


def _walk_block(d, i):
    return jnp.where(d == 0, i, N_SEQ_BLOCKS - 1 - i)


def _block_seq(j):
    prompt_blocks = T_PROMPT // SEQ_BLOCK
    per_prompt, per_sample = SEQ // SEQ_BLOCK, DEC_SEQ // SEQ_BLOCK
    js = j - prompt_blocks
    in_prompt = j < prompt_blocks
    return (jnp.where(in_prompt, j // per_prompt, BATCH + js // per_sample),
            jnp.where(in_prompt, j % per_prompt, js % per_sample),
            jnp.where(in_prompt, per_prompt, per_sample))


def _walk_flags(d, i):
    _, pos, n = _block_seq(_walk_block(d, i))
    return pos == jnp.where(d == 0, 0, n - 1), pos == jnp.where(d == 0, n - 1, 0)


S5_TC = SEQ_BLOCK
S5_PITCH = S5_TC + 8
S5_SLABS = SSM_GROUPS * SSM_STATE // 128
S5_QB = 4
S5_SLABS_Q = S5_SLABS // S5_QB


def _s5_disc_kernel(lr_ref, li_ref, dt_ref, br_ref, bi_ref, ar_ref, ai_ref, bbr_ref, bbi_ref):
    lr, li, dt = lr_ref[...], li_ref[...], dt_ref[...]
    mag = jnp.exp(lr * dt)
    ab_re = mag * jnp.cos(li * dt)
    ab_im = mag * jnp.sin(li * dt)
    den = lr * lr + li * li
    nr, ni = ab_re - 1.0, ab_im
    co_re = (nr * lr + ni * li) / den
    co_im = (ni * lr - nr * li) / den
    br, bi = br_ref[...], bi_ref[...]
    ar_ref[...] = ab_re
    ai_ref[...] = ab_im
    bbr_ref[...] = co_re * br - co_im * bi
    bbi_ref[...] = co_re * bi + co_im * br


def _s5_params(lam_re, lam_im, log_dt, b_re, b_im, c_re, c_im):
    g, p, c = SSM_GROUPS, SSM_STATE, SSM_GROUP
    shape = (2, g, c, p)
    n = 2 * g * c
    bc = lambda a: jnp.broadcast_to(a, shape).reshape(n, p)
    dt = jnp.exp(log_dt)
    args = (bc(lam_re[:, :, None, :]), bc(lam_im[:, :, None, :]), bc(dt[:, :, None, None]),
            jnp.transpose(b_re, (0, 1, 3, 2)).reshape(n, p), jnp.transpose(b_im, (0, 1, 3, 2)).reshape(n, p))
    spec = pl.BlockSpec((n, p), lambda: (0, 0))
    out = jax.ShapeDtypeStruct((n, p), F32)
    ab_re, ab_im, bb_re, bb_im = pl.pallas_call(
        _s5_disc_kernel, in_specs=[spec] * 5, out_specs=[spec] * 4, out_shape=[out] * 4, name="s5_disc")(*args)
    a = jnp.stack([ab_re.reshape(shape)[:, :, 0, :], ab_im.reshape(shape)[:, :, 0, :]], axis=1)
    a = a.reshape(2, 2, S5_SLABS, 128)
    eye = jnp.eye(16, dtype=F32)

    def blockdiag_in(bb):
        return jnp.einsum('dqgcp,gh->dqgchp', bb.reshape(2, S5_QB, 16, c, p), eye).reshape(2, S5_QB, 16 * c, 16 * p)

    def blockdiag_out(cc):
        return jnp.einsum('dqgcp,gh->dqgphc', cc.reshape(2, S5_QB, 16, c, p), eye).reshape(2, S5_QB, 16 * p, 16 * c)

    bb = jnp.concatenate([blockdiag_in(bb_re.reshape(shape)), blockdiag_in(bb_im.reshape(shape))], axis=-1)
    cc = jnp.stack([blockdiag_out(c_re).reshape(2, S5_QB, S5_SLABS_Q, 128, 16 * c),
                    blockdiag_out(-c_im).reshape(2, S5_QB, S5_SLABS_Q, 128, 16 * c)], axis=3)
    cc = cc.reshape(2, S5_QB, S5_SLABS_Q, 256, 16 * c)
    return a, bb.astype(BF16), cc.astype(BF16)


def _s5_scan_kernel(u0, u1, u2, u3, bb_ref, cc_ref, a_ref, h0_ref, y_ref, hout_ref,
                    bur, bui, xsr, xsi, st):
    d = pl.program_id(0)
    starts, ends = _walk_flags(d, pl.program_id(1))

    @pl.when(starts)
    def _():
        st[0] = h0_ref[0, 0, 0]
        st[1] = h0_ref[0, 0, 1]

    half = S5_SLABS_Q * 128
    for q, u in enumerate((u0, u1, u2, u3)):
        buq = _dot(u[...].astype(BF16), bb_ref[0, q])
        for j in range(S5_SLABS_Q):
            row0 = (q * S5_SLABS_Q + j) * S5_PITCH
            bur[row0:row0 + S5_TC, :] = buq[:, 128 * j:128 * (j + 1)]
            bui[row0:row0 + S5_TC, :] = buq[:, half + 128 * j:half + 128 * (j + 1)]

    ar = a_ref[0, 0]
    ai = a_ref[0, 1]

    def step(i, carry):
        xr, xi = carry
        t = jnp.where(d == 0, i, S5_TC - 1 - i)
        rows = pl.ds(t, S5_SLABS, stride=S5_PITCH)
        nxr = ar * xr - ai * xi + bur[rows, :]
        nxi = ar * xi + ai * xr + bui[rows, :]
        xsr[rows, :] = nxr
        xsi[rows, :] = nxi
        return nxr, nxi

    xr, xi = lax.fori_loop(0, S5_TC, step, (st[0], st[1]), unroll=4)
    st[0] = xr
    st[1] = xi

    @pl.when(ends)
    def _():
        hout_ref[0, 0, 0] = xr
        hout_ref[0, 0, 1] = xi

    for q in range(S5_QB):
        acc = jnp.zeros((S5_TC, 16 * SSM_GROUP), F32)
        for j in range(S5_SLABS_Q):
            row0 = (q * S5_SLABS_Q + j) * S5_PITCH
            x_ri = jnp.concatenate([xsr[row0:row0 + S5_TC, :], xsi[row0:row0 + S5_TC, :]], axis=1)
            acc += _dot(x_ri.astype(BF16), cc_ref[0, q, j])
        y_ref[0, :, 256 * q:256 * (q + 1)] = acc


def _s5_scan(proj, a, bb, cc, h0):
    ucol = OFF_SSM // 256

    def u_spec(q):
        return pl.BlockSpec((S5_TC, 256), lambda d, i: (_walk_block(d, i), ucol + q))

    state = pl.BlockSpec((1, 1, 2, S5_SLABS, 128), lambda d, i: (_block_seq(_walk_block(d, i))[0], d, 0, 0, 0))
    slab = pltpu.VMEM((S5_SLABS * S5_PITCH, 128), F32)
    return pl.pallas_call(
        _s5_scan_kernel,
        grid=(2, N_SEQ_BLOCKS),
        in_specs=[u_spec(q) for q in range(S5_QB)] + [
            pl.BlockSpec((1, S5_QB, 256, 2 * S5_SLABS_Q * 128), lambda d, i: (d, 0, 0, 0)),
            pl.BlockSpec((1, S5_QB, S5_SLABS_Q, 256, 256), lambda d, i: (d, 0, 0, 0, 0)),
            pl.BlockSpec((1, 2, S5_SLABS, 128), lambda d, i: (d, 0, 0, 0)),
            state,
        ],
        out_specs=[pl.BlockSpec((1, S5_TC, SSM_WIDTH), lambda d, i: (d, _walk_block(d, i), 0)), state],
        out_shape=[jax.ShapeDtypeStruct((2, T_ALL, SSM_WIDTH), F32),
                   jax.ShapeDtypeStruct((N_SEQS, 2, 2, S5_SLABS, 128), F32)],
        scratch_shapes=[slab, slab, slab, slab, pltpu.VMEM((2, S5_SLABS, 128), F32)],
        compiler_params=_params(("parallel", "arbitrary")),
        name="s5_scan",
    )(proj, proj, proj, proj, bb, cc, a, h0)


def _s5_glu_kernel(ua_ref, ub_ref, y0_ref, y1_ref, d_ref, w_ref, o_ref):
    u = jnp.concatenate([ua_ref[...], ub_ref[...]], axis=-1)
    y = jax.nn.gelu(u * d_ref[0] + y0_ref[0] + y1_ref[0])
    o_ref[...] = (y * jax.nn.sigmoid(_dot(y.astype(BF16), w_ref[0].astype(BF16)))).astype(BF16)


S5_GLU_ROWS = 512


def _s5_glu(proj, row0, y, ssm_d, w_glu, layer):
    rows = y.shape[1]
    tr = S5_GLU_ROWS
    blk0 = row0 // tr
    ucol = OFF_SSM // 512
    return pl.pallas_call(
        _s5_glu_kernel,
        grid=(rows // tr,),
        in_specs=[
            pl.BlockSpec((tr, 512), lambda i: (blk0 + i, ucol)),
            pl.BlockSpec((tr, 512), lambda i: (blk0 + i, ucol + 1)),
            pl.BlockSpec((1, tr, SSM_WIDTH), lambda i: (0, i, 0)),
            pl.BlockSpec((1, tr, SSM_WIDTH), lambda i: (1, i, 0)),
            pl.BlockSpec((1, 1, SSM_WIDTH), lambda i: (layer, 0, 0)),
            pl.BlockSpec((1, SSM_WIDTH, SSM_WIDTH), lambda i: (layer, 0, 0)),
        ],
        out_specs=pl.BlockSpec((tr, SSM_WIDTH), lambda i: (i, 0)),
        out_shape=jax.ShapeDtypeStruct((rows, SSM_WIDTH), BF16),
        compiler_params=_params(("parallel",)),
        name="s5_glu",
    )(proj, proj, y, y, ssm_d.reshape(DEPTH, 1, SSM_WIDTH), w_glu)


RW_ROWS = 256
RW_C = 64
RW_PAIRS = RWKV_HEADS // 2
RW_LOWRANK0 = 3 * RWKV_WIDTH
RW_STEP_CHUNKS = 2
RW_GROUP = 16


def _head_indicator():
    ind = (np.arange(RWKV_WIDTH)[:, None] // RWKV_HEAD == np.arange(128)[None, :]).astype(np.float32)
    return jnp.asarray(ind, BF16), jnp.asarray(ind.T, BF16)


def _head_sum(x, ind, ind_t):
    return _dot_exact_lhs(_dot_exact_lhs(x, ind), ind_t)


def _rwkv_prep_kernel(x_ref, prev_ref, next_ref, mu_ref, kk_ref, ka_ref, rk_ref, w0_ref, a0_ref, w2_ref, a2_ref,
                      g2_ref, ind_ref, indt_ref,
                      r_out, kkn_out, v_out, g_out, bonus_out, lw_out, kka_out, kd_out, *, seq_blocks):
    i = pl.program_id(0)
    n_prompt_blocks = T_PROMPT // RW_ROWS
    j = i - n_prompt_blocks
    in_seq = jnp.where(i < n_prompt_blocks, i % seq_blocks[0], j % seq_blocks[1])
    n_in_seq = jnp.where(i < n_prompt_blocks, seq_blocks[0], seq_blocks[1])
    first = in_seq == 0
    last = in_seq == n_in_seq - 1
    x = x_ref[...]
    row = lax.broadcasted_iota(jnp.int32, x.shape, 0)
    prev_row = jnp.where(first, 0.0, prev_ref[7:8, :])
    next_row = jnp.where(last, 0.0, next_ref[0:1, :])
    xp = jnp.where(row == 0, prev_row, pltpu.roll(x, 1, 0))
    xn = jnp.where(row == RW_ROWS - 1, next_row, pltpu.roll(x, RW_ROWS - 1, 0))
    z = x + mu_ref[...] * (0.5 * (xp + xn) - x)

    w = RWKV_WIDTH
    r, k, v = z[:, 0:w], z[:, w:2 * w], z[:, 2 * w:3 * w]
    low = z[:, RW_LOWRANK0:RW_LOWRANK0 + 128]
    gl = z[:, RW_LOWRANK0 + 128:RW_LOWRANK0 + 256]
    ind, ind_t = ind_ref[...], indt_ref[...]

    kk = k * kk_ref[...]
    kk = kk * lax.rsqrt(_head_sum(kk * kk, ind, ind_t) + 1e-12)
    r_out[...] = r
    kkn_out[...] = kk
    v_out[...] = v
    g_out[...] = _dot(jax.nn.sigmoid(gl).astype(BF16), g2_ref[...].astype(BF16))
    tanh_low = jnp.tanh(low).astype(BF16)
    low_b = low.astype(BF16)
    bonus = jnp.zeros_like(r)
    for d in range(2):
        w_log = -jax.nn.softplus(-(w0_ref[d:d + 1, :] + _dot(tanh_low, w2_ref[d].astype(BF16)))) - 0.5
        a = jax.nn.sigmoid(a0_ref[d:d + 1, :] + _dot(low_b, a2_ref[d].astype(BF16)))
        kd = k * (1.0 + (a - 1.0) * ka_ref[...])
        lw_out[d] = -jnp.exp(w_log)
        kka_out[d] = kk * a
        kd_out[d] = kd
        bonus = bonus + _head_sum(r * kd * rk_ref[...], ind, ind_t) * v
    bonus_out[...] = bonus


def _rwkv_prep(proj_r, lp):
    t = proj_r.shape[0]
    nb = t // RW_ROWS
    halo = RW_ROWS // 8
    w = RWKV_WIDTH
    zpad = jnp.zeros((2, 64, w), F32)
    w2 = jnp.concatenate([lp['rwkv_w2'], zpad], axis=1)
    a2 = jnp.concatenate([zpad, lp['rwkv_a2']], axis=1)
    ind, ind_t = _head_indicator()
    row = lambda a: a.reshape(1, -1)
    full = lambda shape: pl.BlockSpec(shape, lambda i: (0,) * len(shape))
    tok = pl.BlockSpec((RW_ROWS, w), lambda i: (i, 0))
    tok2 = pl.BlockSpec((2, RW_ROWS, w), lambda i: (0, i, 0))
    f1 = jax.ShapeDtypeStruct((t, w), F32)
    f2 = jax.ShapeDtypeStruct((2, t, w), F32)
    return pl.pallas_call(
        functools.partial(_rwkv_prep_kernel, seq_blocks=(SEQ // RW_ROWS, DEC_SEQ // RW_ROWS)),
        grid=(nb,),
        in_specs=[
            pl.BlockSpec((RW_ROWS, RWKV_IN_W), lambda i: (i, 0)),
            pl.BlockSpec((8, RWKV_IN_W), lambda i: (jnp.maximum(i * halo - 1, 0), 0)),
            pl.BlockSpec((8, RWKV_IN_W), lambda i: (jnp.minimum((i + 1) * halo, nb * halo - 1), 0)),
            full((1, RWKV_IN_W)), full((1, w)), full((1, w)), full((1, w)), full((2, w)), full((2, w)),
            full((2, 128, w)), full((2, 128, w)), full((GATE_RANK, w)), full((w, 128)), full((128, w)),
        ],
        out_specs=[tok, tok, tok, tok, tok, tok2, tok2, tok2],
        out_shape=[f1, f1, f1, f1, f1, f2, f2, f2],
        compiler_params=_params(("parallel",)),
        name="rwkv_prep",
    )(proj_r, proj_r, proj_r, row(lp['rwkv_mu']), row(lp['rwkv_k_k']), row(lp['rwkv_k_a']), row(lp['rwkv_r_k']),
      lp['rwkv_w0'], lp['rwkv_a0'], w2, a2, lp['rwkv_g2'], ind, ind_t)


def _rwkv_chunk_kernel(r_ref, kk_ref, v_ref, lw_ref, kka_ref, kd_ref, r2_out, y0_out, mneg_out, sadd_out, gc_out):
    d = pl.program_id(0)
    c = RW_C
    ri = lax.broadcasted_iota(jnp.int32, (2 * c, 2 * c), 0)
    ci = lax.broadcasted_iota(jnp.int32, (2 * c, 2 * c), 1)
    lower_half_rows = ri >= c
    same_head = lower_half_rows == (ci >= c)
    ti, tj = ri % c, ci % c
    before = (tj - ti) * (1 - 2 * d) < 0
    strict = same_head & before
    incl = same_head & (before | (ti == tj))
    eye = (ri == ci).astype(F32)
    tri_c = (incl[0:c, 0:c]).astype(BF16)
    lane = lax.broadcasted_iota(jnp.int32, (1, 2 * c), 1)
    m_a = (lane < RWKV_HEAD).astype(F32)
    m_b = 1.0 - m_a
    bf = lambda a: a.astype(BF16)

    def group(items):
        each = lambda f, *xs: [f(*a) for a in zip(*xs)]
        sl = [slice(128 * p, 128 * (p + 1)) for _, p in items]
        rs = [slice(c * k, c * (k + 1)) for k, _ in items]
        r, kk, v = ([ref[q, s] for q, s in zip(rs, sl)] for ref in (r_ref, kk_ref, v_ref))
        lw, kka, kd = ([ref[0, q, s] for q, s in zip(rs, sl)] for ref in (lw_ref, kka_ref, kd_ref))
        cum = each(lambda x: _dot_exact_rhs(tri_c, x), lw)
        tot = each(lambda x: jnp.where(d == 0, x[c - 1:c, :], x[0:1, :]), cum)
        g_rem = each(lambda t, x: jnp.exp(t - x), tot, cum)
        g_inv = each(lambda x: jnp.exp(-x), cum)
        qk = each(lambda k_, x, l: k_ * jnp.exp(x - l), kk, cum, lw)
        rt = each(lambda r_, x: r_ * jnp.exp(x), r, cum)
        lhs_q = each(lambda q: jnp.concatenate([q * m_a, q * m_b], axis=0), qk)
        lhs_r = each(lambda q: jnp.concatenate([q * m_a, q * m_b], axis=0), rt)
        rhs = each(lambda k_, a_, g: bf(jnp.concatenate([k_ * g, a_ * g], axis=0)), kd, kka, g_inv)
        x = each(lambda l, rr: _dot_nt(bf(l), rr), lhs_q, rhs)
        z = each(lambda l, rr: _dot_nt(bf(l), rr), lhs_r, rhs)
        xr = each(lambda t: pltpu.roll(t, c, 1), x)
        zr = each(lambda t: pltpu.roll(t, c, 1), z)
        qk_bd = each(lambda t, tr: bf(jnp.where(strict, jnp.where(lower_half_rows, tr, t), 0.0)), x, xr)
        qa_bd = each(lambda t, tr: jnp.where(strict, jnp.where(lower_half_rows, t, tr), 0.0), x, xr)
        rk_bd = each(lambda t, tr: bf(jnp.where(incl, jnp.where(lower_half_rows, tr, t), 0.0)), z, zr)
        ra_bd = each(lambda t, tr: bf(jnp.where(incl, jnp.where(lower_half_rows, t, tr), 0.0)), z, zr)
        inv = each(lambda n_: eye - n_, qa_bd)
        power = qa_bd
        for _ in range(5):
            pb = each(bf, power)
            power = each(_dot, pb, pb)
            inv = each(lambda i_, p_: i_ + _dot(bf(i_), bf(p_)), inv, power)
        v_st = each(lambda t: bf(jnp.concatenate([t * m_a, t * m_b], axis=0)), v)
        av_st = each(_dot, qk_bd, v_st)
        sol = each(lambda i_, q, av: _dot(bf(i_), jnp.concatenate([bf(q), bf(av)], axis=1)), inv, lhs_q, av_st)
        ra_sol = each(lambda a_, s_: _dot(a_, bf(s_)), ra_bd, sol)
        rk_v = each(_dot, rk_bd, v_st)
        fold = lambda t: t[0:c] + t[c:2 * c]
        for i, (k, p) in enumerate(items):
            w_f = bf(fold(sol[i][:, 0:128]))
            u0_f = bf(fold(sol[i][:, 128:256]))
            ah_b = bf(kka[i] * g_rem[i])
            r2_out[0, rs[i], sl[i]] = bf(fold(lhs_r[i] - ra_sol[i][:, 0:128]))
            y0_out[0, rs[i], sl[i]] = fold(rk_v[i] - ra_sol[i][:, 128:256])
            mneg_out[0, k, p] = bf(jnp.where(same_head, _dot_tn(w_f, ah_b), 0.0))
            sadd_out[0, k, p] = jnp.where(
                same_head, _dot_tn(bf(v[i]), bf(kd[i] * g_rem[i])) - _dot_tn(u0_f, ah_b), 0.0)
            gc_out[0, k, :, sl[i]] = jnp.exp(tot[i])

    items = [(k, p) for k in range(RW_STEP_CHUNKS) for p in range(RW_PAIRS)]
    for i0 in range(0, len(items), RW_GROUP):
        group(items[i0:i0 + RW_GROUP])


def _rwkv_chunks(r, kk, v, lw, kka, kd):
    t = r.shape[0]
    nck = t // RW_C
    w = RWKV_WIDTH
    rows = RW_STEP_CHUNKS * RW_C
    tok = pl.BlockSpec((rows, w), lambda d, i: (i, 0))
    tok2 = pl.BlockSpec((1, rows, w), lambda d, i: (d, i, 0))
    mat = pl.BlockSpec((1, RW_STEP_CHUNKS, RW_PAIRS, 128, 128), lambda d, i: (d, i, 0, 0, 0))
    mat_shape = lambda dtype: jax.ShapeDtypeStruct((2, nck, RW_PAIRS, 128, 128), dtype)
    return pl.pallas_call(
        _rwkv_chunk_kernel,
        grid=(2, nck // RW_STEP_CHUNKS),
        in_specs=[tok, tok, tok, tok2, tok2, tok2],
        out_specs=[tok2, tok2, mat, mat, pl.BlockSpec((1, RW_STEP_CHUNKS, 1, w), lambda d, i: (d, i, 0, 0))],
        out_shape=[jax.ShapeDtypeStruct((2, t, w), BF16), jax.ShapeDtypeStruct((2, t, w), F32),
                   mat_shape(BF16), mat_shape(F32), jax.ShapeDtypeStruct((2, nck, 1, w), F32)],
        compiler_params=_params(("parallel", "parallel")),
        name="rwkv_chunks",
    )(r, kk, v, lw, kka, kd)


RW_BLOCK_CHUNKS = SEQ_BLOCK // RW_C


def _rwkv_state_kernel(r2_ref, y0_ref, mneg_ref, sadd_ref, gc_ref, s0_ref, y_out, s_out, st):
    d = pl.program_id(0)
    starts, ends = _walk_flags(d, pl.program_id(1))

    @pl.when(starts)
    def _():
        st[...] = s0_ref[0, 0]

    def chunk(i, carry):
        ck = jnp.where(d == 0, i, RW_BLOCK_CHUNKS - 1 - i)
        rows = pl.ds(pl.multiple_of(ck * RW_C, RW_C), RW_C)
        for p in range(RW_PAIRS):
            ls = slice(128 * p, 128 * (p + 1))
            s = st[p]
            sb = s.astype(BF16)
            y_out[0, rows, ls] = _dot_nt(r2_ref[0, rows, ls], sb) + y0_ref[0, rows, ls]
            st[p] = s * gc_ref[0, ck, :, ls] - _dot(sb, mneg_ref[0, ck, p]) + sadd_ref[0, ck, p]
        return carry

    lax.fori_loop(0, RW_BLOCK_CHUNKS, chunk, 0, unroll=True)

    @pl.when(ends)
    def _():
        s_out[0, 0] = st[...]


def _rwkv_state(r2, y0, mneg, sadd, gc, s0):
    w = RWKV_WIDTH
    tok = pl.BlockSpec((1, SEQ_BLOCK, w), lambda d, i: (d, _walk_block(d, i), 0))
    mat = pl.BlockSpec((1, RW_BLOCK_CHUNKS, RW_PAIRS, 128, 128), lambda d, i: (d, _walk_block(d, i), 0, 0, 0))
    state = pl.BlockSpec((1, 1, RW_PAIRS, 128, 128), lambda d, i: (_block_seq(_walk_block(d, i))[0], d, 0, 0, 0))
    return pl.pallas_call(
        _rwkv_state_kernel,
        grid=(2, N_SEQ_BLOCKS),
        in_specs=[tok, tok, mat, mat,
                  pl.BlockSpec((1, RW_BLOCK_CHUNKS, 1, w), lambda d, i: (d, _walk_block(d, i), 0, 0)), state],
        out_specs=[tok, state],
        out_shape=[jax.ShapeDtypeStruct((2, T_ALL, w), F32),
                   jax.ShapeDtypeStruct((N_SEQS, 2, RW_PAIRS, 128, 128), F32)],
        scratch_shapes=[pltpu.VMEM((RW_PAIRS, 128, 128), F32)],
        compiler_params=_params(("parallel", "arbitrary")),
        name="rwkv_state",
    )(r2, y0, mneg, sadd, gc, s0)


def _pack_state(s):
    n = s.shape[0]
    sp = s.reshape(n, 2, RW_PAIRS, 2, RWKV_HEAD, RWKV_HEAD)
    z = jnp.zeros_like(sp[:, :, :, 0])
    top = jnp.concatenate([sp[:, :, :, 0], z], axis=-1)
    bot = jnp.concatenate([z, sp[:, :, :, 1]], axis=-1)
    return jnp.concatenate([top, bot], axis=-2)


def _unpack_state(sp):
    n = sp.shape[0]
    h = RWKV_HEAD
    return jnp.stack([sp[..., 0:h, 0:h], sp[..., h:, h:]], axis=3).reshape(n, 2, RWKV_HEADS, h, h)


def _rwkv_post_kernel(y_ref, bonus_ref, g_ref, lng_ref, lnb_ref, ind_ref, indt_ref, o_ref):
    ind, ind_t = ind_ref[...], indt_ref[...]
    y = y_ref[0] + y_ref[1]
    mu = _head_sum(y, ind, ind_t) * (1.0 / RWKV_HEAD)
    yc = y - mu
    var = _head_sum(yc * yc, ind, ind_t) * (1.0 / RWKV_HEAD)
    yn = yc * lax.rsqrt(var + RWKV_GN_EPS)
    o_ref[...] = ((yn * lng_ref[...] + lnb_ref[...] + bonus_ref[...]) * g_ref[...]).astype(BF16)


def _rwkv_post(y, bonus, g, ln_g, ln_b):
    t = bonus.shape[0]
    w = RWKV_WIDTH
    ind, ind_t = _head_indicator()
    tok = pl.BlockSpec((RW_ROWS, w), lambda i: (i, 0))
    full = lambda shape: pl.BlockSpec(shape, lambda i: (0,) * len(shape))
    return pl.pallas_call(
        _rwkv_post_kernel,
        grid=(t // RW_ROWS,),
        in_specs=[pl.BlockSpec((2, RW_ROWS, w), lambda i: (0, i, 0)), tok, tok, full((1, w)), full((1, w)),
                  full((w, 128)), full((128, w))],
        out_specs=tok,
        out_shape=jax.ShapeDtypeStruct((t, w), BF16),
        compiler_params=_params(("parallel",)),
        name="rwkv_post",
    )(y, bonus, g, ln_g.reshape(1, w), ln_b.reshape(1, w), ind, ind_t)


def _rwkv_mixer(proj_r, lp, s0_sample):
    r, kk, v, g, bonus, lw, kka, kd = _rwkv_prep(proj_r, lp)
    r2, y0, mneg, sadd, gc = _rwkv_chunks(r, kk, v, lw, kka, kd)
    s0 = jnp.concatenate([jnp.zeros((BATCH, 2, RW_PAIRS, 128, 128), F32), _pack_state(s0_sample)], axis=0)
    y, s_fin = _rwkv_state(r2, y0, mneg, sadd, gc, s0)
    return _rwkv_post(y, bonus, g, lp['rwkv_ln_g'], lp['rwkv_ln_b']), _unpack_state(s_fin[:BATCH])


MOE_ROWS = 256
MOE_BLOCKS = T_ALL * EXPERT_TOPK // MOE_ROWS + N_EXPERTS
MOE_PAD_ROWS = MOE_BLOCKS * MOE_ROWS
MOE_FF_TILE = 512
MOE_OUT_TILE = 2048


def _top2_sum(vals):
    best = None
    for a in range(len(vals)):
        for b in range(a + 1, len(vals)):
            s = vals[a] + vals[b]
            best = s if best is None else jnp.maximum(best, s)
    return best


def _first_argmax(vals):
    idx = jnp.zeros(vals[0].shape, jnp.int32)
    best = vals[0]
    for j in range(1, len(vals)):
        upd = vals[j] > best
        idx = jnp.where(upd, j, idx)
        best = jnp.where(upd, vals[j], best)
    return idx, best


def _pick(idx, vals):
    out = vals[-1]
    for j in range(len(vals) - 2, -1, -1):
        out = jnp.where(idx == j, vals[j], out)
    return out


def _ffn_norm_route_kernel(x_ref, g_ref, mod_ref, wh_ref, wl_ref, bias_ref, h_ref, idx_ref, wts_ref):
    y = _rms(x_ref[...], g_ref[...])
    h = y * (1.0 + mod_ref[0, 4:5, :]) + mod_ref[0, 3:4, :]
    h_ref[...] = h
    hh = h.astype(BF16)
    hl = (h - hh.astype(F32)).astype(BF16)
    wh, wl = wh_ref[...], wl_ref[...]
    logits = _dot_nt(wh, hh) + _dot_nt(wh, hl) + _dot_nt(wl, hh)
    scores = jax.nn.sigmoid(logits)
    sel = scores + bias_ref[...]
    gs = EXPERTS_PER_GROUP
    sel_rows = [sel[e:e + 1, :] for e in range(N_EXPERTS)]
    sc_rows = [scores[e:e + 1, :] for e in range(N_EXPERTS)]
    grp, _ = _first_argmax([_top2_sum(sel_rows[gs * g:gs * (g + 1)]) for g in range(N_EXPERT_GROUPS)])
    v = [_pick(grp, [sel_rows[gs * g + j] for g in range(N_EXPERT_GROUPS)]) for j in range(gs)]
    s = [_pick(grp, [sc_rows[gs * g + j] for g in range(N_EXPERT_GROUPS)]) for j in range(gs)]
    i1, _ = _first_argmax(v)
    i2, _ = _first_argmax([jnp.where(i1 == j, -jnp.inf, v[j]) for j in range(gs)])
    w1, w2 = _pick(i1, s), _pick(i2, s)
    tot = w1 + w2
    idx_ref[0:1, :] = grp * gs + i1
    idx_ref[1:2, :] = grp * gs + i2
    wts_ref[0:1, :] = w1 / tot
    wts_ref[1:2, :] = w2 / tot


def _ffn_norm_route(x, g, mod, w_router, router_bias):
    t = x.shape[0]
    wt = w_router.T
    wh = wt.astype(BF16)
    wl = (wt - wh.astype(F32)).astype(BF16)
    full = lambda shape: pl.BlockSpec(shape, lambda i: (0,) * len(shape))
    return pl.pallas_call(
        _ffn_norm_route_kernel,
        grid=(t // NORM_ROWS,),
        in_specs=[
            pl.BlockSpec((NORM_ROWS, D_MODEL), lambda i: (i, 0)),
            full((1, D_MODEL)),
            pl.BlockSpec((1, 6, D_MODEL), lambda i: (_cond_group(i, NORM_ROWS), 0, 0)),
            full((N_EXPERTS, D_MODEL)), full((N_EXPERTS, D_MODEL)), full((N_EXPERTS, 1)),
        ],
        out_specs=[pl.BlockSpec((NORM_ROWS, D_MODEL), lambda i: (i, 0)),
                   pl.BlockSpec((EXPERT_TOPK, NORM_ROWS), lambda i: (0, i)),
                   pl.BlockSpec((EXPERT_TOPK, NORM_ROWS), lambda i: (0, i))],
        out_shape=[jax.ShapeDtypeStruct((t, D_MODEL), F32),
                   jax.ShapeDtypeStruct((EXPERT_TOPK, t), jnp.int32),
                   jax.ShapeDtypeStruct((EXPERT_TOPK, t), F32)],
        compiler_params=_params(("parallel",)),
        name="ffn_norm_route",
    )(x, g.reshape(1, D_MODEL), mod, wh, wl, router_bias.reshape(N_EXPERTS, 1))


def _dispatch_plan(idx):
    t = idx.shape[1]
    flat_e = idx.T.reshape(-1)
    onehot = (flat_e[:, None] == jnp.arange(N_EXPERTS, dtype=jnp.int32)[None, :]).astype(jnp.int32)
    csum = jnp.cumsum(onehot, axis=0)
    rank = jnp.take_along_axis(csum, flat_e[:, None], axis=1)[:, 0] - 1
    counts = csum[-1]
    padded = ((counts + MOE_ROWS - 1) // MOE_ROWS) * MOE_ROWS
    pad_end = jnp.cumsum(padded)
    pad_start = pad_end - padded
    dest = (pad_start[flat_e] + rank).astype(jnp.int32)
    tok = jnp.arange(t * EXPERT_TOPK, dtype=jnp.int32) // EXPERT_TOPK
    row_tok = jnp.zeros((MOE_PAD_ROWS,), jnp.int32).at[dest].set(tok)
    block_row0 = jnp.arange(MOE_BLOCKS, dtype=jnp.int32) * MOE_ROWS
    block_e = jnp.minimum(jnp.sum((pad_end[None, :] <= block_row0[:, None]).astype(jnp.int32), axis=1),
                          N_EXPERTS - 1).astype(jnp.int32)
    n_used = (pad_end[-1] // MOE_ROWS).astype(jnp.int32).reshape(1)
    return row_tok, block_e, n_used, dest


DMA_UNROLL = 8


def _row_copy(src_hbm, row, dst, slot, sem):
    return pltpu.make_async_copy(src_hbm.at[pl.ds(row, 1), :], dst.at[pl.ds(slot, 1), :], sem)


def _start_rows(src_hbm, idx_ref, idx0, stride, n, dst, sem):
    def body(r, c):
        _row_copy(src_hbm, idx_ref[idx0 + stride * r], dst, r, sem).start()
        return c
    lax.fori_loop(0, n, body, 0, unroll=DMA_UNROLL)


def _wait_rows(src_hbm, n, dst, sem):
    def body(r, c):
        _row_copy(src_hbm, 0, dst, r, sem).wait()
        return c
    lax.fori_loop(0, n, body, 0, unroll=DMA_UNROLL)


def _gather_rows_kernel(tok_ref, nu_ref, h_hbm, o_ref, buf, sem):
    i = pl.program_id(0)
    n = nu_ref[0]

    @pl.when((i == 0) & (n > 0))
    def _():
        _start_rows(h_hbm, tok_ref, 0, 1, MOE_ROWS, buf.at[0], sem.at[0])

    @pl.when(i + 1 < n)
    def _():
        nxt = (i + 1) % 2
        _start_rows(h_hbm, tok_ref, (i + 1) * MOE_ROWS, 1, MOE_ROWS, buf.at[nxt], sem.at[nxt])

    @pl.when(i < n)
    def _():
        cur = i % 2
        _wait_rows(h_hbm, MOE_ROWS, buf.at[cur], sem.at[cur])
        o_ref[...] = buf[cur].astype(BF16)

    @pl.when(i >= n)
    def _():
        o_ref[...] = jnp.zeros_like(o_ref)


def _gather_rows(h, row_tok, n_used):
    return pl.pallas_call(
        _gather_rows_kernel,
        grid_spec=pltpu.PrefetchScalarGridSpec(
            num_scalar_prefetch=2,
            grid=(MOE_BLOCKS,),
            in_specs=[pl.BlockSpec(memory_space=pl.ANY)],
            out_specs=pl.BlockSpec((MOE_ROWS, D_MODEL), lambda i, tok, nu: (i, 0)),
            scratch_shapes=[pltpu.VMEM((2, MOE_ROWS, D_MODEL), F32), pltpu.SemaphoreType.DMA((2,))],
        ),
        out_shape=jax.ShapeDtypeStruct((MOE_PAD_ROWS, D_MODEL), BF16),
        compiler_params=_params(("arbitrary",)),
        name="moe_gather",
    )(row_tok, n_used, h)


def _expert_up_kernel(be_ref, nu_ref, x_ref, w1_ref, w3_ref, o_ref):
    i = pl.program_id(1)

    @pl.when(i < nu_ref[0])
    def _():
        x = x_ref[...]
        a = _dot(x, w1_ref[0, 0].astype(BF16))
        b = _dot(x, w3_ref[0, 0].astype(BF16))
        o_ref[...] = (a * jax.nn.sigmoid(a) * b).astype(BF16)

    @pl.when(i >= nu_ref[0])
    def _():
        o_ref[...] = jnp.zeros_like(o_ref)


def _expert_up(xb, w_gate, w_up, layer, block_e, n_used):
    wspec = pl.BlockSpec((1, 1, D_MODEL, MOE_FF_TILE), lambda j, i, be, nu: (layer, be[i], 0, j))
    return pl.pallas_call(
        _expert_up_kernel,
        grid_spec=pltpu.PrefetchScalarGridSpec(
            num_scalar_prefetch=2,
            grid=(EXPERT_FF // MOE_FF_TILE, MOE_BLOCKS),
            in_specs=[pl.BlockSpec((MOE_ROWS, D_MODEL), lambda j, i, be, nu: (i, 0)), wspec, wspec],
            out_specs=pl.BlockSpec((MOE_ROWS, MOE_FF_TILE), lambda j, i, be, nu: (i, j)),
        ),
        out_shape=jax.ShapeDtypeStruct((MOE_PAD_ROWS, EXPERT_FF), BF16),
        compiler_params=_params(("parallel", "arbitrary")),
        name="moe_up",
    )(block_e, n_used, xb, w_gate, w_up)


def _expert_down_kernel(be_ref, nu_ref, h_ref, w2_ref, o_ref):
    i = pl.program_id(1)

    @pl.when(i < nu_ref[0])
    def _():
        o_ref[...] = _dot(h_ref[...], w2_ref[0, 0].astype(BF16))

    @pl.when(i >= nu_ref[0])
    def _():
        o_ref[...] = jnp.zeros_like(o_ref)


def _expert_down(hmid, w_down, layer, block_e, n_used):
    return pl.pallas_call(
        _expert_down_kernel,
        grid_spec=pltpu.PrefetchScalarGridSpec(
            num_scalar_prefetch=2,
            grid=(D_MODEL // MOE_OUT_TILE, MOE_BLOCKS),
            in_specs=[
                pl.BlockSpec((MOE_ROWS, EXPERT_FF), lambda j, i, be, nu: (i, 0)),
                pl.BlockSpec((1, 1, EXPERT_FF, MOE_OUT_TILE), lambda j, i, be, nu: (layer, be[i], 0, j)),
            ],
            out_specs=pl.BlockSpec((MOE_ROWS, MOE_OUT_TILE), lambda j, i, be, nu: (i, j)),
        ),
        out_shape=jax.ShapeDtypeStruct((MOE_PAD_ROWS, D_MODEL), F32),
        compiler_params=_params(("parallel", "arbitrary")),
        name="moe_down",
    )(block_e, n_used, hmid, w_down)


COMBINE_ROWS = 256


def _combine_kernel(pos_ref, yb_hbm, x_ref, w_ref, mod_ref, o_ref, buf, sem):
    i = pl.program_id(0)
    n = pl.num_programs(0)
    per_block = COMBINE_ROWS * EXPERT_TOPK

    def start(blk, slot):
        for k in range(EXPERT_TOPK):
            _start_rows(yb_hbm, pos_ref, blk * per_block + k, EXPERT_TOPK, COMBINE_ROWS, buf.at[slot, k],
                        sem.at[slot])

    @pl.when(i == 0)
    def _():
        start(0, 0)

    @pl.when(i + 1 < n)
    def _():
        start(i + 1, (i + 1) % 2)

    cur = i % 2
    for k in range(EXPERT_TOPK):
        _wait_rows(yb_hbm, COMBINE_ROWS, buf.at[cur, k], sem.at[cur])
    y = buf[cur, 0] * w_ref[:, 0:1] + buf[cur, 1] * w_ref[:, 1:2]
    o_ref[...] = x_ref[...] + mod_ref[0, 5:6, :] * y


def _combine(yb, dest, wts, x, mod):
    t = x.shape[0]
    return pl.pallas_call(
        _combine_kernel,
        grid_spec=pltpu.PrefetchScalarGridSpec(
            num_scalar_prefetch=1,
            grid=(t // COMBINE_ROWS,),
            in_specs=[
                pl.BlockSpec(memory_space=pl.ANY),
                pl.BlockSpec((COMBINE_ROWS, D_MODEL), lambda i, pos: (i, 0)),
                pl.BlockSpec((COMBINE_ROWS, EXPERT_TOPK), lambda i, pos: (i, 0)),
                pl.BlockSpec((1, 6, D_MODEL), lambda i, pos: (_cond_group(i, COMBINE_ROWS), 0, 0)),
            ],
            out_specs=pl.BlockSpec((COMBINE_ROWS, D_MODEL), lambda i, pos: (i, 0)),
            scratch_shapes=[pltpu.VMEM((2, EXPERT_TOPK, COMBINE_ROWS, D_MODEL), F32),
                            pltpu.SemaphoreType.DMA((2,))],
        ),
        out_shape=jax.ShapeDtypeStruct((t, D_MODEL), F32),
        compiler_params=_params(("arbitrary",)),
        name="moe_combine",
    )(dest, yb, x, wts.T, mod)


def _moe_residual(x, g, mod, lp_router, w_gate, w_up, w_down, layer):
    h, idx, wts = _ffn_norm_route(x, g, mod, *lp_router)
    row_tok, block_e, n_used, dest = _dispatch_plan(idx)
    xb = _gather_rows(h, row_tok, n_used)
    hmid = _expert_up(xb, w_gate, w_up, layer, block_e, n_used)
    yb = _expert_down(hmid, w_down, layer, block_e, n_used)
    return _combine(yb, dest, wts, x, mod)


def _layer_params(args, layer):
    return {k: v[layer] for k, v in args.items()}


def kernel(x_prompt, x_sample, cache_na_k, cache_na_v, cache_gqa_k, cache_gqa_v, state_ssm, state_rwkv, c, c_ctx,
           w_ada, b_ada, norm_mix, norm_ffn, w_in, na_rpb, gqa_q_norm, gqa_k_norm, ssm_lam_re, ssm_lam_im,
           ssm_log_dt, ssm_b_re, ssm_b_im, ssm_c_re, ssm_c_im, ssm_d, ssm_w_glu, rwkv_mu, rwkv_w0, rwkv_w2, rwkv_a0,
           rwkv_a2, rwkv_k_k, rwkv_k_a, rwkv_r_k, rwkv_g2, rwkv_ln_g, rwkv_ln_b, w_branch, w_out, w_router,
           router_bias, w_exp_gate, w_exp_up, w_exp_down, norm_final):
    x = jnp.concatenate([x_prompt.reshape(T_PROMPT, D_MODEL), x_sample.reshape(T_SAMPLE, D_MODEL)], axis=0)
    cond = jnp.concatenate([c_ctx[None, :], c, jnp.zeros((COND_PAD - N_COND, D_MODEL), F32)], axis=0)
    mod_all = _ada_all(cond, w_ada, b_ada).reshape(DEPTH, COND_PAD, 6, D_MODEL)
    cos, sin = _rope_tables()
    rwkv_args = dict(rwkv_mu=rwkv_mu, rwkv_w0=rwkv_w0, rwkv_w2=rwkv_w2, rwkv_a0=rwkv_a0, rwkv_a2=rwkv_a2,
                     rwkv_k_k=rwkv_k_k, rwkv_k_a=rwkv_k_a, rwkv_r_k=rwkv_r_k.reshape(DEPTH, RWKV_WIDTH),
                     rwkv_g2=rwkv_g2, rwkv_ln_g=rwkv_ln_g, rwkv_ln_b=rwkv_ln_b)
    caches = [[] for _ in range(6)]
    for l in range(DEPTH):
        mod = mod_all[l]
        h = _norm_mod(x, norm_mix[l], mod, 0)
        proj = _in_proj(h, w_in, l, 0, OFF_RWKV)
        proj_r = _in_proj(h, w_in, l, OFF_RWKV, RWKV_IN_W)
        proj_g = _in_proj(h, w_in, l, OFF_GATE, N_BRANCH * D_MODEL)

        y_na_p, nk, nv = _ctx_na(proj)
        y_gqa_p, gk, gv = _ctx_gqa(proj, gqa_q_norm[l], gqa_k_norm[l])
        y_na_s = _lat_na(proj, cache_na_k, cache_na_v, l, _na_band_bias(na_rpb[l]))
        y_gqa_s = _lat_gqa(proj, cache_gqa_k, cache_gqa_v, l, gqa_q_norm[l], gqa_k_norm[l], cos, sin)

        a, bb, cc = _s5_params(ssm_lam_re[l], ssm_lam_im[l], ssm_log_dt[l], ssm_b_re[l], ssm_b_im[l],
                               ssm_c_re[l], ssm_c_im[l])
        h0 = jnp.concatenate([jnp.zeros((BATCH, 2, 2, S5_SLABS, 128), F32),
                              state_ssm[:, l].reshape(DEC_BATCH, 2, 2, S5_SLABS, 128)], axis=0)
        y_scan, ssm_fin = _s5_scan(proj, a, bb, cc, h0)
        ssm_state = ssm_fin[:BATCH]
        y_ssm = _s5_glu(proj, 0, y_scan, ssm_d, ssm_w_glu, l)

        y_rwkv, rwkv_state = _rwkv_mixer(proj_r, _layer_params(rwkv_args, l), state_rwkv[:, l])

        merged = _branch_merge((y_na_p, y_na_s), (y_gqa_p, y_gqa_s), y_ssm, y_rwkv, proj_g, w_branch, l)
        x = _out_proj_residual(merged, w_out, l, x, mod, 2)
        x = _moe_residual(x, norm_ffn[l], mod, (w_router, router_bias), w_exp_gate, w_exp_up, w_exp_down, l)

        for lst, val in zip(caches, (nk, nv, gk, gv, ssm_state.reshape(BATCH, 2, 2, SSM_GROUPS, SSM_STATE),
                                     rwkv_state)):
            lst.append(val)

    y = _final_norm(x, norm_final)
    outs = [jnp.stack(lst, axis=1) for lst in caches]
    return (y[:T_PROMPT].reshape(BATCH, SEQ, D_MODEL), y[T_PROMPT:].reshape(DEC_BATCH, DEC_SEQ, D_MODEL), *outs)
```

```python
import functools

import numpy as np
import jax
import jax.numpy as jnp
from jax import lax
from jax.experimental import pallas as pl
from jax.experimental.pallas import tpu as pltpu

F32 = jnp.float32
BF16 = jnp.bfloat16

D_MODEL = 4096
BATCH = 16
SEQ = 256
DEPTH = 4
DEC_BATCH = 2
DEC_SEQ = 1024
PAST_LEN = 512
GRID_W = 64
HEAD_DIM = 128
EPS = 1e-6
NEG = -1e30
NA_HEADS = 8
NA_WIN_ROWS = 8
NA_WIN_COLS = 16
NA_W = NA_HEADS * HEAD_DIM
GQA_HEADS = 8
GQA_KV_HEADS = 2
GQA_GROUP = GQA_HEADS // GQA_KV_HEADS
GQA_Q_W = GQA_HEADS * HEAD_DIM
GQA_KV_W = GQA_KV_HEADS * HEAD_DIM
GQA_QKV_W = GQA_Q_W + 2 * GQA_KV_W
ROPE_THETA = 10000.0
SSM_WIDTH = 1024
SSM_GROUP = 16
SSM_GROUPS = SSM_WIDTH // SSM_GROUP
SSM_STATE = 64
RWKV_WIDTH = 1024
RWKV_HEAD = 64
RWKV_HEADS = RWKV_WIDTH // RWKV_HEAD
DECAY_RANK = 64
ICLR_RANK = 64
GATE_RANK = 128
RWKV_IN_W = 3 * RWKV_WIDTH + DECAY_RANK + ICLR_RANK + GATE_RANK
RWKV_GN_EPS = 64e-5
N_BRANCH = 4
OFF_NA = 0
OFF_GQA = 3 * NA_W
OFF_SSM = OFF_GQA + GQA_QKV_W
OFF_RWKV = OFF_SSM + SSM_WIDTH
OFF_GATE = OFF_RWKV + RWKV_IN_W
IN_W = OFF_GATE + N_BRANCH * D_MODEL
N_EXPERTS = 16
N_EXPERT_GROUPS = 4
EXPERTS_PER_GROUP = N_EXPERTS // N_EXPERT_GROUPS
EXPERT_TOPK = 2
EXPERT_FF = 1024

T_PROMPT = BATCH * SEQ
T_SAMPLE = DEC_BATCH * DEC_SEQ
T_ALL = T_PROMPT + T_SAMPLE
N_COND = 1 + DEC_BATCH
COND_PAD = 8
ATT_SCALE = HEAD_DIM ** -0.5

VMEM_LIMIT = 56 * 1024 * 1024


def _params(sem):
    return pltpu.CompilerParams(dimension_semantics=sem, vmem_limit_bytes=VMEM_LIMIT)


def _cond_group(i, rows_per_block):
    n_prompt_blocks = T_PROMPT // rows_per_block
    blocks_per_seq = DEC_SEQ // rows_per_block
    return jnp.where(i < n_prompt_blocks, 0, 1 + (i - n_prompt_blocks) // blocks_per_seq)


def _dot(a, b):
    return jnp.dot(a, b, preferred_element_type=F32)


def _dot_nt(a, b):
    return lax.dot_general(a, b, (((1,), (1,)), ((), ())), preferred_element_type=F32)


def _dot_tn(a, b):
    return lax.dot_general(a, b, (((0,), (0,)), ((), ())), preferred_element_type=F32)


def _split3(x):
    hi = x.astype(BF16)
    r1 = x - hi.astype(F32)
    mid = r1.astype(BF16)
    lo = (r1 - mid.astype(F32)).astype(BF16)
    return hi, mid, lo


def _dot_exact_rhs(a_bf16, x):
    hi, mid, lo = _split3(x)
    return _dot(a_bf16, hi) + _dot(a_bf16, mid) + _dot(a_bf16, lo)


def _dot_exact_lhs(x, b_bf16):
    hi, mid, lo = _split3(x)
    return _dot(hi, b_bf16) + _dot(mid, b_bf16) + _dot(lo, b_bf16)


def _rms(x, g):
    return x * lax.rsqrt(jnp.mean(x * x, axis=-1, keepdims=True) + EPS) * g


def _softmax_rows(s):
    m = jnp.max(s, axis=-1, keepdims=True)
    e = jnp.exp(s - m)
    return e / jnp.sum(e, axis=-1, keepdims=True)


def _ada_kernel(c_ref, w_ref, b_ref, o_ref):
    c = c_ref[...]
    s = (c * jax.nn.sigmoid(c)).astype(BF16)
    o_ref[0] = _dot(s, w_ref[0].astype(BF16)) + b_ref[0]


def _ada_all(cond, w_ada, b_ada):
    tn = 512
    n6 = 6 * D_MODEL
    return pl.pallas_call(
        _ada_kernel,
        grid=(DEPTH, n6 // tn),
        in_specs=[
            pl.BlockSpec((COND_PAD, D_MODEL), lambda l, j: (0, 0)),
            pl.BlockSpec((1, D_MODEL, tn), lambda l, j: (l, 0, j)),
            pl.BlockSpec((1, 1, tn), lambda l, j: (l, 0, j)),
        ],
        out_specs=pl.BlockSpec((1, COND_PAD, tn), lambda l, j: (l, 0, j)),
        out_shape=jax.ShapeDtypeStruct((DEPTH, COND_PAD, n6), F32),
        compiler_params=_params(("parallel", "parallel")),
        name="ada",
    )(cond, w_ada, b_ada.reshape(DEPTH, 1, n6))


NORM_ROWS = 256


def _norm_mod_kernel(x_ref, g_ref, mod_ref, o_ref, *, shift_idx):
    y = _rms(x_ref[...], g_ref[...])
    scale = mod_ref[0, shift_idx + 1:shift_idx + 2, :]
    shift = mod_ref[0, shift_idx:shift_idx + 1, :]
    o_ref[...] = (y * (1.0 + scale) + shift).astype(o_ref.dtype)


def _norm_mod(x, g, mod, shift_idx, out_dtype=BF16):
    t = x.shape[0]
    return pl.pallas_call(
        functools.partial(_norm_mod_kernel, shift_idx=shift_idx),
        grid=(t // NORM_ROWS,),
        in_specs=[
            pl.BlockSpec((NORM_ROWS, D_MODEL), lambda i: (i, 0)),
            pl.BlockSpec((1, D_MODEL), lambda i: (0, 0)),
            pl.BlockSpec((1, 6, D_MODEL), lambda i: (_cond_group(i, NORM_ROWS), 0, 0)),
        ],
        out_specs=pl.BlockSpec((NORM_ROWS, D_MODEL), lambda i: (i, 0)),
        out_shape=jax.ShapeDtypeStruct((t, D_MODEL), out_dtype),
        compiler_params=_params(("parallel",)),
        name="norm_mod",
    )(x, g.reshape(1, D_MODEL), mod)


def _final_norm_kernel(x_ref, g_ref, o_ref):
    o_ref[...] = _rms(x_ref[...], g_ref[...])


def _final_norm(x, g):
    t = x.shape[0]
    return pl.pallas_call(
        _final_norm_kernel,
        grid=(t // NORM_ROWS,),
        in_specs=[pl.BlockSpec((NORM_ROWS, D_MODEL), lambda i: (i, 0)),
                  pl.BlockSpec((1, D_MODEL), lambda i: (0, 0))],
        out_specs=pl.BlockSpec((NORM_ROWS, D_MODEL), lambda i: (i, 0)),
        out_shape=jax.ShapeDtypeStruct((t, D_MODEL), F32),
        compiler_params=_params(("parallel",)),
        name="final_norm",
    )(x, g.reshape(1, D_MODEL))


MM_TM = 1024
MM_TN = 256
IN_TM = 2048
OUT_TN = 512


def _proj_kernel(x_ref, w_ref, o_ref):
    o_ref[...] = _dot(x_ref[...], w_ref[0].astype(BF16))


def _in_proj(h, w_in, layer, col0, n_cols):
    t = h.shape[0]
    blk0 = col0 // MM_TN
    return pl.pallas_call(
        _proj_kernel,
        grid=(t // IN_TM, n_cols // MM_TN),
        in_specs=[
            pl.BlockSpec((IN_TM, D_MODEL), lambda i, j: (i, 0)),
            pl.BlockSpec((1, D_MODEL, MM_TN), lambda i, j: (layer, 0, blk0 + j)),
        ],
        out_specs=pl.BlockSpec((IN_TM, MM_TN), lambda i, j: (i, j)),
        out_shape=jax.ShapeDtypeStruct((t, n_cols), F32),
        compiler_params=_params(("parallel", "parallel")),
        name="in_proj",
    )(h, w_in)


def _merge_kernel(y0p, y0s, y1p, y1s, y2, y3, wb_ref, g0, g1, g2, g3, o_ref):
    in_prompt = pl.program_id(0) < T_PROMPT // MM_TM
    ys = (jnp.where(in_prompt, y0p[...], y0s[...]), jnp.where(in_prompt, y1p[...], y1s[...]), y2[...], y3[...])
    acc = None
    for n, (y, g) in enumerate(zip(ys, (g0, g1, g2, g3))):
        br = _dot(y, wb_ref[0, n].astype(BF16))
        term = jax.nn.sigmoid(g[...]) * br
        acc = term if acc is None else acc + term
    o_ref[...] = acc.astype(BF16)


def _branch_merge(y_na, y_gqa, y_ssm, y_rwkv, proj, w_branch, layer):
    t = proj.shape[0]
    bw = y_ssm.shape[1]
    per_branch = D_MODEL // MM_TN
    prompt_blocks = T_PROMPT // MM_TM
    y_spec = pl.BlockSpec((MM_TM, bw), lambda i, j: (i, 0))
    yp_spec = pl.BlockSpec((MM_TM, bw), lambda i, j: (jnp.minimum(i, prompt_blocks - 1), 0))
    ys_spec = pl.BlockSpec((MM_TM, bw), lambda i, j: (jnp.maximum(i - prompt_blocks, 0), 0))

    def gate_spec(n):
        return pl.BlockSpec((MM_TM, MM_TN), lambda i, j: (i, n * per_branch + j))

    return pl.pallas_call(
        _merge_kernel,
        grid=(t // MM_TM, D_MODEL // MM_TN),
        in_specs=[yp_spec, ys_spec, yp_spec, ys_spec, y_spec, y_spec]
        + [pl.BlockSpec((1, N_BRANCH, bw, MM_TN), lambda i, j: (layer, 0, 0, j))]
        + [gate_spec(n) for n in range(N_BRANCH)],
        out_specs=pl.BlockSpec((MM_TM, MM_TN), lambda i, j: (i, j)),
        out_shape=jax.ShapeDtypeStruct((t, D_MODEL), BF16),
        compiler_params=_params(("parallel", "parallel")),
        name="branch_merge",
    )(*y_na, *y_gqa, y_ssm, y_rwkv, w_branch, proj, proj, proj, proj)


def _out_proj_kernel(m_ref, w_ref, x_ref, mod_ref, o_ref, *, gate_idx):
    out = _dot(m_ref[...], w_ref[0].astype(BF16))
    o_ref[...] = x_ref[...] + mod_ref[0, gate_idx:gate_idx + 1, :] * out


def _out_proj_residual(merged, w_out, layer, x, mod, gate_idx):
    t = x.shape[0]
    tn = OUT_TN
    return pl.pallas_call(
        functools.partial(_out_proj_kernel, gate_idx=gate_idx),
        grid=(t // MM_TM, D_MODEL // tn),
        in_specs=[
            pl.BlockSpec((MM_TM, D_MODEL), lambda i, j: (i, 0)),
            pl.BlockSpec((1, D_MODEL, tn), lambda i, j: (layer, 0, j)),
            pl.BlockSpec((MM_TM, tn), lambda i, j: (i, j)),
            pl.BlockSpec((1, 6, tn), lambda i, j: (_cond_group(i, MM_TM), 0, j)),
        ],
        out_specs=pl.BlockSpec((MM_TM, tn), lambda i, j: (i, j)),
        out_shape=jax.ShapeDtypeStruct((t, D_MODEL), F32),
        compiler_params=_params(("parallel", "parallel")),
        name="out_proj",
    )(merged, w_out, x, mod)


def _attend(q, k, v):
    p = _softmax_rows(_dot_nt(q, k) * ATT_SCALE)
    return _dot(p.astype(BF16), v)


def _ctx_na_kernel(q_ref, k_ref, v_ref, y_ref, ko_ref, vo_ref):
    for h in range(NA_HEADS):
        sl = slice(HEAD_DIM * h, HEAD_DIM * (h + 1))
        kf = k_ref[:, sl]
        vf = v_ref[:, sl]
        y = _attend(q_ref[:, sl].astype(BF16), kf.astype(BF16), vf.astype(BF16))
        y_ref[:, sl] = y.astype(BF16)
        ko_ref[0, h] = kf
        vo_ref[0, h] = vf


def _ctx_na(proj):
    cache = jax.ShapeDtypeStruct((BATCH, NA_HEADS, SEQ, HEAD_DIM), F32)
    cache_spec = pl.BlockSpec((1, NA_HEADS, SEQ, HEAD_DIM), lambda b: (b, 0, 0, 0))
    return pl.pallas_call(
        _ctx_na_kernel,
        grid=(BATCH,),
        in_specs=[pl.BlockSpec((SEQ, NA_W), lambda b, c=c: (b, c)) for c in range(3)],
        out_specs=[pl.BlockSpec((SEQ, NA_W), lambda b: (b, 0)), cache_spec, cache_spec],
        out_shape=[jax.ShapeDtypeStruct((T_PROMPT, NA_W), BF16), cache, cache],
        compiler_params=_params(("parallel",)),
        name="ctx_na",
    )(proj, proj, proj)


def _ctx_gqa_kernel(q_ref, k_ref, v_ref, gq_ref, gk_ref, y_ref, ko_ref, vo_ref):
    for kv in range(GQA_KV_HEADS):
        sl = slice(HEAD_DIM * kv, HEAD_DIM * (kv + 1))
        kn = _rms(k_ref[:, sl], gk_ref[...])
        vf = v_ref[:, sl]
        ko_ref[0, kv] = kn
        vo_ref[0, kv] = vf
        kb = kn.astype(BF16)
        vb = vf.astype(BF16)
        for g in range(GQA_GROUP):
            h = kv * GQA_GROUP + g
            hs = slice(HEAD_DIM * h, HEAD_DIM * (h + 1))
            qn = _rms(q_ref[:, hs], gq_ref[...])
            y_ref[:, hs] = _attend(qn.astype(BF16), kb, vb).astype(BF16)


def _ctx_gqa(proj, gq, gk):
    cache = jax.ShapeDtypeStruct((BATCH, GQA_KV_HEADS, SEQ, HEAD_DIM), F32)
    cache_spec = pl.BlockSpec((1, GQA_KV_HEADS, SEQ, HEAD_DIM), lambda b: (b, 0, 0, 0))
    norm_spec = pl.BlockSpec((1, HEAD_DIM), lambda b: (0, 0))
    return pl.pallas_call(
        _ctx_gqa_kernel,
        grid=(BATCH,),
        in_specs=[
            pl.BlockSpec((SEQ, GQA_Q_W), lambda b: (b, OFF_GQA // GQA_Q_W)),
            pl.BlockSpec((SEQ, GQA_KV_W), lambda b: (b, (OFF_GQA + GQA_Q_W) // GQA_KV_W)),
            pl.BlockSpec((SEQ, GQA_KV_W), lambda b: (b, (OFF_GQA + GQA_Q_W) // GQA_KV_W + 1)),
            norm_spec, norm_spec,
        ],
        out_specs=[pl.BlockSpec((SEQ, GQA_Q_W), lambda b: (b, 0)), cache_spec, cache_spec],
        out_shape=[jax.ShapeDtypeStruct((T_PROMPT, GQA_Q_W), BF16), cache, cache],
        compiler_params=_params(("parallel",)),
        name="ctx_gqa",
    )(proj, proj, proj, gq.reshape(1, HEAD_DIM), gk.reshape(1, HEAD_DIM))


LAT_Q_ROWS = 256
LAT_ROW0 = T_PROMPT // DEC_SEQ


def _rope_tables():
    t = np.arange(DEC_SEQ)
    half = HEAD_DIM // 2
    inv = jnp.asarray(ROPE_THETA, F32) ** (-jnp.arange(0, half, 2, dtype=F32) / half)
    rows = jnp.asarray(t // GRID_W, F32)
    cols = jnp.asarray(t % GRID_W, F32)
    ang_r = rows[:, None] * inv[None, :]
    ang_c = cols[:, None] * inv[None, :]
    cos = jnp.concatenate([jnp.cos(ang_r)] * 2 + [jnp.cos(ang_c)] * 2, axis=-1)
    sin = jnp.concatenate([-jnp.sin(ang_r), jnp.sin(ang_r), -jnp.sin(ang_c), jnp.sin(ang_c)], axis=-1)
    return cos, sin


def _rope(x, cos, sin):
    q = HEAD_DIM // 4
    lane = lax.broadcasted_iota(jnp.int32, x.shape, 1)
    partner = jnp.where((lane % (2 * q)) < q, pltpu.roll(x, HEAD_DIM - q, 1), pltpu.roll(x, q, 1))
    return x * cos + partner * sin


def _lat_gqa_kernel(q_ref, k_ref, v_ref, ck_ref, cv_ref, gq_ref, gk_ref, cos_ref, sin_ref, y_ref, kall, vall):
    cos = cos_ref[...]
    sin = sin_ref[...]
    kall[0:PAST_LEN, :] = ck_ref[0, 0, 0].astype(BF16)
    vall[0:PAST_LEN, :] = cv_ref[0, 0, 0].astype(BF16)
    kall[PAST_LEN:, :] = _rope(_rms(k_ref[...], gk_ref[...]), cos, sin).astype(BF16)
    vall[PAST_LEN:, :] = v_ref[...].astype(BF16)
    for g in range(GQA_GROUP):
        hs = slice(HEAD_DIM * g, HEAD_DIM * (g + 1))
        for qb in range(DEC_SEQ // LAT_Q_ROWS):
            rs = slice(LAT_Q_ROWS * qb, LAT_Q_ROWS * (qb + 1))
            qn = _rope(_rms(q_ref[rs, hs], gq_ref[...]), cos[rs], sin[rs])
            y_ref[rs, hs] = _attend(qn.astype(BF16), kall[...], vall[...]).astype(BF16)


def _lat_gqa(proj, cache_k, cache_v, layer, gq, gk, cos, sin):
    group_w = GQA_GROUP * HEAD_DIM
    kcol = (OFF_GQA + GQA_Q_W) // HEAD_DIM
    cache_spec = pl.BlockSpec((1, 1, 1, PAST_LEN, HEAD_DIM), lambda b, kv: (b, layer, kv, 0, 0))
    norm_spec = pl.BlockSpec((1, HEAD_DIM), lambda b, kv: (0, 0))
    tab_spec = pl.BlockSpec((DEC_SEQ, HEAD_DIM), lambda b, kv: (0, 0))
    return pl.pallas_call(
        _lat_gqa_kernel,
        grid=(DEC_BATCH, GQA_KV_HEADS),
        in_specs=[
            pl.BlockSpec((DEC_SEQ, group_w), lambda b, kv: (LAT_ROW0 + b, OFF_GQA // group_w + kv)),
            pl.BlockSpec((DEC_SEQ, HEAD_DIM), lambda b, kv: (LAT_ROW0 + b, kcol + kv)),
            pl.BlockSpec((DEC_SEQ, HEAD_DIM), lambda b, kv: (LAT_ROW0 + b, kcol + GQA_KV_HEADS + kv)),
            cache_spec, cache_spec, norm_spec, norm_spec, tab_spec, tab_spec,
        ],
        out_specs=pl.BlockSpec((DEC_SEQ, group_w), lambda b, kv: (b, kv)),
        out_shape=jax.ShapeDtypeStruct((T_SAMPLE, GQA_Q_W), BF16),
        scratch_shapes=[pltpu.VMEM((PAST_LEN + DEC_SEQ, HEAD_DIM), BF16)] * 2,
        compiler_params=_params(("parallel", "parallel")),
        name="lat_gqa",
    )(proj, proj, proj, cache_k, cache_v, gq.reshape(1, HEAD_DIM), gk.reshape(1, HEAD_DIM), cos, sin)


NA_ROWS = DEC_SEQ // GRID_W
NA_WR = min(NA_WIN_ROWS, NA_ROWS)
NA_BAND = NA_WR * GRID_W
NA_DR = 2 * NA_WIN_ROWS - 1
NA_DC = 2 * NA_WIN_COLS - 1
NA_GROUP_ROWS = 5


def _na_row_window(r):
    r0 = min(max(r - NA_WR // 2, 0), NA_ROWS - NA_WR)
    return r0, r0 - r + NA_WIN_ROWS - 1


def _na_table_kernel(rpb_ref, sel_ref, ok_ref, o_ref):
    picked = _dot_exact_lhs(rpb_ref[...], sel_ref[...])
    o_ref[...] = jnp.where(ok_ref[...] > 0.5, picked, NEG)


def _na_band_bias(rpb):
    c = np.arange(GRID_W)
    dc = c[None, :] - c[:, None] + NA_WIN_COLS - 1
    c0 = np.clip(c - NA_WIN_COLS // 2, 0, GRID_W - NA_WIN_COLS)
    ok = (c[None, :] >= c0[:, None]) & (c[None, :] < c0[:, None] + NA_WIN_COLS)
    sel = np.zeros((128, GRID_W * GRID_W), np.float32)
    flat_dc = np.clip(dc, 0, NA_DC - 1).reshape(-1)
    sel[flat_dc, np.arange(GRID_W * GRID_W)] = 1.0
    n = NA_HEADS * NA_DR
    rpb2 = jnp.pad(rpb.reshape(n, NA_DC).astype(F32), ((0, 0), (0, 128 - NA_DC)))
    full = lambda shape: pl.BlockSpec(shape, lambda: (0,) * len(shape))
    table = pl.pallas_call(
        _na_table_kernel,
        in_specs=[full((n, 128)), full((128, GRID_W * GRID_W)), full((1, GRID_W * GRID_W))],
        out_specs=full((n, GRID_W * GRID_W)),
        out_shape=jax.ShapeDtypeStruct((n, GRID_W * GRID_W), F32),
        name="na_table",
    )(rpb2, jnp.asarray(sel, BF16), jnp.asarray(ok.reshape(1, -1), F32))
    table = table.reshape(NA_HEADS, NA_DR, GRID_W, GRID_W)
    neg = jnp.full((NA_HEADS, GRID_W, GRID_W), NEG, F32)
    tiles = []
    for a0 in range(NA_DR - NA_WR + 1):
        band = table[:, a0:a0 + NA_WR].transpose(0, 2, 1, 3).reshape(NA_HEADS, GRID_W, NA_BAND)
        tiles += [jnp.concatenate([band, neg], axis=-1), jnp.concatenate([neg, band], axis=-1)]
    return jnp.stack(tiles, axis=1)


def _na_groups():
    groups, r = [], 0
    while r < NA_ROWS:
        k0, end = _na_row_window(r)[0], r + 1
        while (end < NA_ROWS and end - r < NA_GROUP_ROWS
               and _na_row_window(end)[0] + NA_WR - k0 <= NA_WR + 1):
            end += 1
        groups.append((r, end, k0, _na_row_window(end - 1)[0] + NA_WR - k0))
        r = end
    return groups


def _lat_na_kernel(q_ref, k_ref, v_ref, ck_ref, cv_ref, bias_ref, y_ref):
    kb = k_ref[...].astype(BF16)
    vb = v_ref[...].astype(BF16)
    ck = ck_ref[0, 0, 0].astype(BF16)
    cv = cv_ref[0, 0, 0].astype(BF16)
    for ra, rb, k0, nk in _na_groups():
        rs = slice(GRID_W * ra, GRID_W * rb)
        ks = slice(GRID_W * k0, GRID_W * (k0 + nk))
        q = q_ref[rs, :].astype(BF16)
        bias = jnp.concatenate(
            [bias_ref[0, 2 * _na_row_window(r)[1] + (_na_row_window(r)[0] - k0), :, 0:GRID_W * nk]
             for r in range(ra, rb)], axis=0)
        s_loc = _dot_nt(q, kb[ks]) * ATT_SCALE + bias
        s_ctx = _dot_nt(q, ck) * ATT_SCALE
        m = jnp.maximum(jnp.max(s_loc, axis=-1, keepdims=True), jnp.max(s_ctx, axis=-1, keepdims=True))
        e_loc = jnp.exp(s_loc - m)
        e_ctx = jnp.exp(s_ctx - m)
        den = jnp.sum(e_loc, axis=-1, keepdims=True) + jnp.sum(e_ctx, axis=-1, keepdims=True)
        out = _dot((e_loc / den).astype(BF16), vb[ks]) + _dot((e_ctx / den).astype(BF16), cv)
        y_ref[rs, :] = out.astype(BF16)


def _lat_na(proj, cache_k, cache_v, layer, bias):
    cache_spec = pl.BlockSpec((1, 1, 1, PAST_LEN, HEAD_DIM), lambda h, b: (b, layer, h, 0, 0))

    def qkv_spec(c):
        return pl.BlockSpec((DEC_SEQ, HEAD_DIM), lambda h, b: (LAT_ROW0 + b, c * NA_HEADS + h))

    return pl.pallas_call(
        _lat_na_kernel,
        grid=(NA_HEADS, DEC_BATCH),
        in_specs=[qkv_spec(0), qkv_spec(1), qkv_spec(2), cache_spec, cache_spec,
                  pl.BlockSpec((1, 2 * (NA_DR - NA_WR + 1), GRID_W, NA_BAND + GRID_W), lambda h, b: (h, 0, 0, 0))],
        out_specs=pl.BlockSpec((DEC_SEQ, HEAD_DIM), lambda h, b: (b, h)),
        out_shape=jax.ShapeDtypeStruct((T_SAMPLE, NA_W), BF16),
        compiler_params=_params(("parallel", "parallel")),
        name="lat_na",
    )(proj, proj, proj, cache_k, cache_v, bias)


SEQ_BLOCK = 256
N_SEQ_BLOCKS = T_ALL // SEQ_BLOCK
N_SEQS = BATCH + DEC_BATCH


def _walk_block(d, i):
    return jnp.where(d == 0, i, N_SEQ_BLOCKS - 1 - i)


def _block_seq(j):
    prompt_blocks = T_PROMPT // SEQ_BLOCK
    per_prompt, per_sample = SEQ // SEQ_BLOCK, DEC_SEQ // SEQ_BLOCK
    js = j - prompt_blocks
    in_prompt = j < prompt_blocks
    return (jnp.where(in_prompt, j // per_prompt, BATCH + js // per_sample),
            jnp.where(in_prompt, j % per_prompt, js % per_sample),
            jnp.where(in_prompt, per_prompt, per_sample))


def _walk_flags(d, i):
    _, pos, n = _block_seq(_walk_block(d, i))
    return pos == jnp.where(d == 0, 0, n - 1), pos == jnp.where(d == 0, n - 1, 0)


S5_TC = SEQ_BLOCK
S5_PITCH = S5_TC + 8
S5_SLABS = SSM_GROUPS * SSM_STATE // 128
S5_QB = 4
S5_SLABS_Q = S5_SLABS // S5_QB


def _s5_disc_kernel(lr_ref, li_ref, dt_ref, br_ref, bi_ref, ar_ref, ai_ref, bbr_ref, bbi_ref):
    lr, li, dt = lr_ref[...], li_ref[...], dt_ref[...]
    mag = jnp.exp(lr * dt)
    ab_re = mag * jnp.cos(li * dt)
    ab_im = mag * jnp.sin(li * dt)
    den = lr * lr + li * li
    nr, ni = ab_re - 1.0, ab_im
    co_re = (nr * lr + ni * li) / den
    co_im = (ni * lr - nr * li) / den
    br, bi = br_ref[...], bi_ref[...]
    ar_ref[...] = ab_re
    ai_ref[...] = ab_im
    bbr_ref[...] = co_re * br - co_im * bi
    bbi_ref[...] = co_re * bi + co_im * br


def _s5_params(lam_re, lam_im, log_dt, b_re, b_im, c_re, c_im):
    g, p, c = SSM_GROUPS, SSM_STATE, SSM_GROUP
    shape = (2, g, c, p)
    n = 2 * g * c
    bc = lambda a: jnp.broadcast_to(a, shape).reshape(n, p)
    dt = jnp.exp(log_dt)
    args = (bc(lam_re[:, :, None, :]), bc(lam_im[:, :, None, :]), bc(dt[:, :, None, None]),
            jnp.transpose(b_re, (0, 1, 3, 2)).reshape(n, p), jnp.transpose(b_im, (0, 1, 3, 2)).reshape(n, p))
    spec = pl.BlockSpec((n, p), lambda: (0, 0))
    out = jax.ShapeDtypeStruct((n, p), F32)
    ab_re, ab_im, bb_re, bb_im = pl.pallas_call(
        _s5_disc_kernel, in_specs=[spec] * 5, out_specs=[spec] * 4, out_shape=[out] * 4, name="s5_disc")(*args)
    a = jnp.stack([ab_re.reshape(shape)[:, :, 0, :], ab_im.reshape(shape)[:, :, 0, :]], axis=1)
    a = a.reshape(2, 2, S5_SLABS, 128)
    eye = jnp.eye(16, dtype=F32)

    def blockdiag_in(bb):
        return jnp.einsum('dqgcp,gh->dqgchp', bb.reshape(2, S5_QB, 16, c, p), eye).reshape(2, S5_QB, 16 * c, 16 * p)

    def blockdiag_out(cc):
        return jnp.einsum('dqgcp,gh->dqgphc', cc.reshape(2, S5_QB, 16, c, p), eye).reshape(2, S5_QB, 16 * p, 16 * c)

    bb = jnp.concatenate([blockdiag_in(bb_re.reshape(shape)), blockdiag_in(bb_im.reshape(shape))], axis=-1)
    cc = jnp.stack([blockdiag_out(c_re).reshape(2, S5_QB, S5_SLABS_Q, 128, 16 * c),
                    blockdiag_out(-c_im).reshape(2, S5_QB, S5_SLABS_Q, 128, 16 * c)], axis=3)
    cc = cc.reshape(2, S5_QB, S5_SLABS_Q, 256, 16 * c)
    return a, bb.astype(BF16), cc.astype(BF16)


def _s5_scan_kernel(u0, u1, u2, u3, bb_ref, cc_ref, a_ref, h0_ref, y_ref, hout_ref,
                    bur, bui, xsr, xsi, st):
    d = pl.program_id(0)
    starts, ends = _walk_flags(d, pl.program_id(1))

    @pl.when(starts)
    def _():
        st[0] = h0_ref[0, 0, 0]
        st[1] = h0_ref[0, 0, 1]

    half = S5_SLABS_Q * 128
    for q, u in enumerate((u0, u1, u2, u3)):
        buq = _dot(u[...].astype(BF16), bb_ref[0, q])
        for j in range(S5_SLABS_Q):
            row0 = (q * S5_SLABS_Q + j) * S5_PITCH
            bur[row0:row0 + S5_TC, :] = buq[:, 128 * j:128 * (j + 1)]
            bui[row0:row0 + S5_TC, :] = buq[:, half + 128 * j:half + 128 * (j + 1)]

    ar = a_ref[0, 0]
    ai = a_ref[0, 1]

    def step(i, carry):
        xr, xi = carry
        t = jnp.where(d == 0, i, S5_TC - 1 - i)
        rows = pl.ds(t, S5_SLABS, stride=S5_PITCH)
        nxr = ar * xr - ai * xi + bur[rows, :]
        nxi = ar * xi + ai * xr + bui[rows, :]
        xsr[rows, :] = nxr
        xsi[rows, :] = nxi
        return nxr, nxi

    xr, xi = lax.fori_loop(0, S5_TC, step, (st[0], st[1]), unroll=4)
    st[0] = xr
    st[1] = xi

    @pl.when(ends)
    def _():
        hout_ref[0, 0, 0] = xr
        hout_ref[0, 0, 1] = xi

    for q in range(S5_QB):
        acc = jnp.zeros((S5_TC, 16 * SSM_GROUP), F32)
        for j in range(S5_SLABS_Q):
            row0 = (q * S5_SLABS_Q + j) * S5_PITCH
            x_ri = jnp.concatenate([xsr[row0:row0 + S5_TC, :], xsi[row0:row0 + S5_TC, :]], axis=1)
            acc += _dot(x_ri.astype(BF16), cc_ref[0, q, j])
        y_ref[0, :, 256 * q:256 * (q + 1)] = acc


def _s5_scan(proj, a, bb, cc, h0):
    ucol = OFF_SSM // 256

    def u_spec(q):
        return pl.BlockSpec((S5_TC, 256), lambda d, i: (_walk_block(d, i), ucol + q))

    state = pl.BlockSpec((1, 1, 2, S5_SLABS, 128), lambda d, i: (_block_seq(_walk_block(d, i))[0], d, 0, 0, 0))
    slab = pltpu.VMEM((S5_SLABS * S5_PITCH, 128), F32)
    return pl.pallas_call(
        _s5_scan_kernel,
        grid=(2, N_SEQ_BLOCKS),
        in_specs=[u_spec(q) for q in range(S5_QB)] + [
            pl.BlockSpec((1, S5_QB, 256, 2 * S5_SLABS_Q * 128), lambda d, i: (d, 0, 0, 0)),
            pl.BlockSpec((1, S5_QB, S5_SLABS_Q, 256, 256), lambda d, i: (d, 0, 0, 0, 0)),
            pl.BlockSpec((1, 2, S5_SLABS, 128), lambda d, i: (d, 0, 0, 0)),
            state,
        ],
        out_specs=[pl.BlockSpec((1, S5_TC, SSM_WIDTH), lambda d, i: (d, _walk_block(d, i), 0)), state],
        out_shape=[jax.ShapeDtypeStruct((2, T_ALL, SSM_WIDTH), F32),
                   jax.ShapeDtypeStruct((N_SEQS, 2, 2, S5_SLABS, 128), F32)],
        scratch_shapes=[slab, slab, slab, slab, pltpu.VMEM((2, S5_SLABS, 128), F32)],
        compiler_params=_params(("parallel", "arbitrary")),
        name="s5_scan",
    )(proj, proj, proj, proj, bb, cc, a, h0)


def _s5_glu_kernel(ua_ref, ub_ref, y0_ref, y1_ref, d_ref, w_ref, o_ref):
    u = jnp.concatenate([ua_ref[...], ub_ref[...]], axis=-1)
    y = jax.nn.gelu(u * d_ref[0] + y0_ref[0] + y1_ref[0])
    o_ref[...] = (y * jax.nn.sigmoid(_dot(y.astype(BF16), w_ref[0].astype(BF16)))).astype(BF16)


S5_GLU_ROWS = 512


def _s5_glu(proj, row0, y, ssm_d, w_glu, layer):
    rows = y.shape[1]
    tr = S5_GLU_ROWS
    blk0 = row0 // tr
    ucol = OFF_SSM // 512
    return pl.pallas_call(
        _s5_glu_kernel,
        grid=(rows // tr,),
        in_specs=[
            pl.BlockSpec((tr, 512), lambda i: (blk0 + i, ucol)),
            pl.BlockSpec((tr, 512), lambda i: (blk0 + i, ucol + 1)),
            pl.BlockSpec((1, tr, SSM_WIDTH), lambda i: (0, i, 0)),
            pl.BlockSpec((1, tr, SSM_WIDTH), lambda i: (1, i, 0)),
            pl.BlockSpec((1, 1, SSM_WIDTH), lambda i: (layer, 0, 0)),
            pl.BlockSpec((1, SSM_WIDTH, SSM_WIDTH), lambda i: (layer, 0, 0)),
        ],
        out_specs=pl.BlockSpec((tr, SSM_WIDTH), lambda i: (i, 0)),
        out_shape=jax.ShapeDtypeStruct((rows, SSM_WIDTH), BF16),
        compiler_params=_params(("parallel",)),
        name="s5_glu",
    )(proj, proj, y, y, ssm_d.reshape(DEPTH, 1, SSM_WIDTH), w_glu)


RW_ROWS = 256
RW_C = 64
RW_PAIRS = RWKV_HEADS // 2
RW_LOWRANK0 = 3 * RWKV_WIDTH
RW_STEP_CHUNKS = 2
RW_GROUP = 16


def _head_indicator():
    ind = (np.arange(RWKV_WIDTH)[:, None] // RWKV_HEAD == np.arange(128)[None, :]).astype(np.float32)
    return jnp.asarray(ind, BF16), jnp.asarray(ind.T, BF16)


def _head_sum(x, ind, ind_t):
    return _dot_exact_lhs(_dot_exact_lhs(x, ind), ind_t)


def _rwkv_prep_kernel(x_ref, prev_ref, next_ref, mu_ref, kk_ref, ka_ref, rk_ref, w0_ref, a0_ref, w2_ref, a2_ref,
                      g2_ref, ind_ref, indt_ref,
                      r_out, kkn_out, v_out, g_out, bonus_out, lw_out, kka_out, kd_out, *, seq_blocks):
    i = pl.program_id(0)
    n_prompt_blocks = T_PROMPT // RW_ROWS
    j = i - n_prompt_blocks
    in_seq = jnp.where(i < n_prompt_blocks, i % seq_blocks[0], j % seq_blocks[1])
    n_in_seq = jnp.where(i < n_prompt_blocks, seq_blocks[0], seq_blocks[1])
    first = in_seq == 0
    last = in_seq == n_in_seq - 1
    x = x_ref[...]
    row = lax.broadcasted_iota(jnp.int32, x.shape, 0)
    prev_row = jnp.where(first, 0.0, prev_ref[7:8, :])
    next_row = jnp.where(last, 0.0, next_ref[0:1, :])
    xp = jnp.where(row == 0, prev_row, pltpu.roll(x, 1, 0))
    xn = jnp.where(row == RW_ROWS - 1, next_row, pltpu.roll(x, RW_ROWS - 1, 0))
    z = x + mu_ref[...] * (0.5 * (xp + xn) - x)

    w = RWKV_WIDTH
    r, k, v = z[:, 0:w], z[:, w:2 * w], z[:, 2 * w:3 * w]
    low = z[:, RW_LOWRANK0:RW_LOWRANK0 + 128]
    gl = z[:, RW_LOWRANK0 + 128:RW_LOWRANK0 + 256]
    ind, ind_t = ind_ref[...], indt_ref[...]

    kk = k * kk_ref[...]
    kk = kk * lax.rsqrt(_head_sum(kk * kk, ind, ind_t) + 1e-12)
    r_out[...] = r
    kkn_out[...] = kk
    v_out[...] = v
    g_out[...] = _dot(jax.nn.sigmoid(gl).astype(BF16), g2_ref[...].astype(BF16))
    tanh_low = jnp.tanh(low).astype(BF16)
    low_b = low.astype(BF16)
    bonus = jnp.zeros_like(r)
    for d in range(2):
        w_log = -jax.nn.softplus(-(w0_ref[d:d + 1, :] + _dot(tanh_low, w2_ref[d].astype(BF16)))) - 0.5
        a = jax.nn.sigmoid(a0_ref[d:d + 1, :] + _dot(low_b, a2_ref[d].astype(BF16)))
        kd = k * (1.0 + (a - 1.0) * ka_ref[...])
        lw_out[d] = -jnp.exp(w_log)
        kka_out[d] = kk * a
        kd_out[d] = kd
        bonus = bonus + _head_sum(r * kd * rk_ref[...], ind, ind_t) * v
    bonus_out[...] = bonus


def _rwkv_prep(proj_r, lp):
    t = proj_r.shape[0]
    nb = t // RW_ROWS
    halo = RW_ROWS // 8
    w = RWKV_WIDTH
    zpad = jnp.zeros((2, 64, w), F32)
    w2 = jnp.concatenate([lp['rwkv_w2'], zpad], axis=1)
    a2 = jnp.concatenate([zpad, lp['rwkv_a2']], axis=1)
    ind, ind_t = _head_indicator()
    row = lambda a: a.reshape(1, -1)
    full = lambda shape: pl.BlockSpec(shape, lambda i: (0,) * len(shape))
    tok = pl.BlockSpec((RW_ROWS, w), lambda i: (i, 0))
    tok2 = pl.BlockSpec((2, RW_ROWS, w), lambda i: (0, i, 0))
    f1 = jax.ShapeDtypeStruct((t, w), F32)
    f2 = jax.ShapeDtypeStruct((2, t, w), F32)
    return pl.pallas_call(
        functools.partial(_rwkv_prep_kernel, seq_blocks=(SEQ // RW_ROWS, DEC_SEQ // RW_ROWS)),
        grid=(nb,),
        in_specs=[
            pl.BlockSpec((RW_ROWS, RWKV_IN_W), lambda i: (i, 0)),
            pl.BlockSpec((8, RWKV_IN_W), lambda i: (jnp.maximum(i * halo - 1, 0), 0)),
            pl.BlockSpec((8, RWKV_IN_W), lambda i: (jnp.minimum((i + 1) * halo, nb * halo - 1), 0)),
            full((1, RWKV_IN_W)), full((1, w)), full((1, w)), full((1, w)), full((2, w)), full((2, w)),
            full((2, 128, w)), full((2, 128, w)), full((GATE_RANK, w)), full((w, 128)), full((128, w)),
        ],
        out_specs=[tok, tok, tok, tok, tok, tok2, tok2, tok2],
        out_shape=[f1, f1, f1, f1, f1, f2, f2, f2],
        compiler_params=_params(("parallel",)),
        name="rwkv_prep",
    )(proj_r, proj_r, proj_r, row(lp['rwkv_mu']), row(lp['rwkv_k_k']), row(lp['rwkv_k_a']), row(lp['rwkv_r_k']),
      lp['rwkv_w0'], lp['rwkv_a0'], w2, a2, lp['rwkv_g2'], ind, ind_t)


def _rwkv_chunk_kernel(r_ref, kk_ref, v_ref, lw_ref, kka_ref, kd_ref, r2_out, y0_out, mneg_out, sadd_out, gc_out):
    d = pl.program_id(0)
    c = RW_C
    ri = lax.broadcasted_iota(jnp.int32, (2 * c, 2 * c), 0)
    ci = lax.broadcasted_iota(jnp.int32, (2 * c, 2 * c), 1)
    lower_half_rows = ri >= c
    same_head = lower_half_rows == (ci >= c)
    ti, tj = ri % c, ci % c
    before = (tj - ti) * (1 - 2 * d) < 0
    strict = same_head & before
    incl = same_head & (before | (ti == tj))
    eye = (ri == ci).astype(F32)
    tri_c = (incl[0:c, 0:c]).astype(BF16)
    lane = lax.broadcasted_iota(jnp.int32, (1, 2 * c), 1)
    m_a = (lane < RWKV_HEAD).astype(F32)
    m_b = 1.0 - m_a
    bf = lambda a: a.astype(BF16)

    def group(items):
        each = lambda f, *xs: [f(*a) for a in zip(*xs)]
        sl = [slice(128 * p, 128 * (p + 1)) for _, p in items]
        rs = [slice(c * k, c * (k + 1)) for k, _ in items]
        r, kk, v = ([ref[q, s] for q, s in zip(rs, sl)] for ref in (r_ref, kk_ref, v_ref))
        lw, kka, kd = ([ref[0, q, s] for q, s in zip(rs, sl)] for ref in (lw_ref, kka_ref, kd_ref))
        cum = each(lambda x: _dot_exact_rhs(tri_c, x), lw)
        tot = each(lambda x: jnp.where(d == 0, x[c - 1:c, :], x[0:1, :]), cum)
        g_rem = each(lambda t, x: jnp.exp(t - x), tot, cum)
        g_inv = each(lambda x: jnp.exp(-x), cum)
        qk = each(lambda k_, x, l: k_ * jnp.exp(x - l), kk, cum, lw)
        rt = each(lambda r_, x: r_ * jnp.exp(x), r, cum)
        lhs_q = each(lambda q: jnp.concatenate([q * m_a, q * m_b], axis=0), qk)
        lhs_r = each(lambda q: jnp.concatenate([q * m_a, q * m_b], axis=0), rt)
        rhs = each(lambda k_, a_, g: bf(jnp.concatenate([k_ * g, a_ * g], axis=0)), kd, kka, g_inv)
        x = each(lambda l, rr: _dot_nt(bf(l), rr), lhs_q, rhs)
        z = each(lambda l, rr: _dot_nt(bf(l), rr), lhs_r, rhs)
        xr = each(lambda t: pltpu.roll(t, c, 1), x)
        zr = each(lambda t: pltpu.roll(t, c, 1), z)
        qk_bd = each(lambda t, tr: bf(jnp.where(strict, jnp.where(lower_half_rows, tr, t), 0.0)), x, xr)
        qa_bd = each(lambda t, tr: jnp.where(strict, jnp.where(lower_half_rows, t, tr), 0.0), x, xr)
        rk_bd = each(lambda t, tr: bf(jnp.where(incl, jnp.where(lower_half_rows, tr, t), 0.0)), z, zr)
        ra_bd = each(lambda t, tr: bf(jnp.where(incl, jnp.where(lower_half_rows, t, tr), 0.0)), z, zr)
        inv = each(lambda n_: eye - n_, qa_bd)
        power = qa_bd
        for _ in range(5):
            pb = each(bf, power)
            power = each(_dot, pb, pb)
            inv = each(lambda i_, p_: i_ + _dot(bf(i_), bf(p_)), inv, power)
        v_st = each(lambda t: bf(jnp.concatenate([t * m_a, t * m_b], axis=0)), v)
        av_st = each(_dot, qk_bd, v_st)
        sol = each(lambda i_, q, av: _dot(bf(i_), jnp.concatenate([bf(q), bf(av)], axis=1)), inv, lhs_q, av_st)
        ra_sol = each(lambda a_, s_: _dot(a_, bf(s_)), ra_bd, sol)
        rk_v = each(_dot, rk_bd, v_st)
        fold = lambda t: t[0:c] + t[c:2 * c]
        for i, (k, p) in enumerate(items):
            w_f = bf(fold(sol[i][:, 0:128]))
            u0_f = bf(fold(sol[i][:, 128:256]))
            ah_b = bf(kka[i] * g_rem[i])
            r2_out[0, rs[i], sl[i]] = bf(fold(lhs_r[i] - ra_sol[i][:, 0:128]))
            y0_out[0, rs[i], sl[i]] = fold(rk_v[i] - ra_sol[i][:, 128:256])
            mneg_out[0, k, p] = bf(jnp.where(same_head, _dot_tn(w_f, ah_b), 0.0))
            sadd_out[0, k, p] = jnp.where(
                same_head, _dot_tn(bf(v[i]), bf(kd[i] * g_rem[i])) - _dot_tn(u0_f, ah_b), 0.0)
            gc_out[0, k, :, sl[i]] = jnp.exp(tot[i])

    items = [(k, p) for k in range(RW_STEP_CHUNKS) for p in range(RW_PAIRS)]
    for i0 in range(0, len(items), RW_GROUP):
        group(items[i0:i0 + RW_GROUP])


def _rwkv_chunks(r, kk, v, lw, kka, kd):
    t = r.shape[0]
    nck = t // RW_C
    w = RWKV_WIDTH
    rows = RW_STEP_CHUNKS * RW_C
    tok = pl.BlockSpec((rows, w), lambda d, i: (i, 0))
    tok2 = pl.BlockSpec((1, rows, w), lambda d, i: (d, i, 0))
    mat = pl.BlockSpec((1, RW_STEP_CHUNKS, RW_PAIRS, 128, 128), lambda d, i: (d, i, 0, 0, 0))
    mat_shape = lambda dtype: jax.ShapeDtypeStruct((2, nck, RW_PAIRS, 128, 128), dtype)
    return pl.pallas_call(
        _rwkv_chunk_kernel,
        grid=(2, nck // RW_STEP_CHUNKS),
        in_specs=[tok, tok, tok, tok2, tok2, tok2],
        out_specs=[tok2, tok2, mat, mat, pl.BlockSpec((1, RW_STEP_CHUNKS, 1, w), lambda d, i: (d, i, 0, 0))],
        out_shape=[jax.ShapeDtypeStruct((2, t, w), BF16), jax.ShapeDtypeStruct((2, t, w), F32),
                   mat_shape(BF16), mat_shape(F32), jax.ShapeDtypeStruct((2, nck, 1, w), F32)],
        compiler_params=_params(("parallel", "parallel")),
        name="rwkv_chunks",
    )(r, kk, v, lw, kka, kd)


RW_BLOCK_CHUNKS = SEQ_BLOCK // RW_C


def _rwkv_state_kernel(r2_ref, y0_ref, mneg_ref, sadd_ref, gc_ref, s0_ref, y_out, s_out, st):
    d = pl.program_id(0)
    starts, ends = _walk_flags(d, pl.program_id(1))

    @pl.when(starts)
    def _():
        st[...] = s0_ref[0, 0]

    def chunk(i, carry):
        ck = jnp.where(d == 0, i, RW_BLOCK_CHUNKS - 1 - i)
        rows = pl.ds(pl.multiple_of(ck * RW_C, RW_C), RW_C)
        for p in range(RW_PAIRS):
            ls = slice(128 * p, 128 * (p + 1))
            s = st[p]
            sb = s.astype(BF16)
            y_out[0, rows, ls] = _dot_nt(r2_ref[0, rows, ls], sb) + y0_ref[0, rows, ls]
            st[p] = s * gc_ref[0, ck, :, ls] - _dot(sb, mneg_ref[0, ck, p]) + sadd_ref[0, ck, p]
        return carry

    lax.fori_loop(0, RW_BLOCK_CHUNKS, chunk, 0, unroll=True)

    @pl.when(ends)
    def _():
        s_out[0, 0] = st[...]


def _rwkv_state(r2, y0, mneg, sadd, gc, s0):
    w = RWKV_WIDTH
    tok = pl.BlockSpec((1, SEQ_BLOCK, w), lambda d, i: (d, _walk_block(d, i), 0))
    mat = pl.BlockSpec((1, RW_BLOCK_CHUNKS, RW_PAIRS, 128, 128), lambda d, i: (d, _walk_block(d, i), 0, 0, 0))
    state = pl.BlockSpec((1, 1, RW_PAIRS, 128, 128), lambda d, i: (_block_seq(_walk_block(d, i))[0], d, 0, 0, 0))
    return pl.pallas_call(
        _rwkv_state_kernel,
        grid=(2, N_SEQ_BLOCKS),
        in_specs=[tok, tok, mat, mat,
                  pl.BlockSpec((1, RW_BLOCK_CHUNKS, 1, w), lambda d, i: (d, _walk_block(d, i), 0, 0)), state],
        out_specs=[tok, state],
        out_shape=[jax.ShapeDtypeStruct((2, T_ALL, w), F32),
                   jax.ShapeDtypeStruct((N_SEQS, 2, RW_PAIRS, 128, 128), F32)],
        scratch_shapes=[pltpu.VMEM((RW_PAIRS, 128, 128), F32)],
        compiler_params=_params(("parallel", "arbitrary")),
        name="rwkv_state",
    )(r2, y0, mneg, sadd, gc, s0)


def _pack_state(s):
    n = s.shape[0]
    sp = s.reshape(n, 2, RW_PAIRS, 2, RWKV_HEAD, RWKV_HEAD)
    z = jnp.zeros_like(sp[:, :, :, 0])
    top = jnp.concatenate([sp[:, :, :, 0], z], axis=-1)
    bot = jnp.concatenate([z, sp[:, :, :, 1]], axis=-1)
    return jnp.concatenate([top, bot], axis=-2)


def _unpack_state(sp):
    n = sp.shape[0]
    h = RWKV_HEAD
    return jnp.stack([sp[..., 0:h, 0:h], sp[..., h:, h:]], axis=3).reshape(n, 2, RWKV_HEADS, h, h)


def _rwkv_post_kernel(y_ref, bonus_ref, g_ref, lng_ref, lnb_ref, ind_ref, indt_ref, o_ref):
    ind, ind_t = ind_ref[...], indt_ref[...]
    y = y_ref[0] + y_ref[1]
    mu = _head_sum(y, ind, ind_t) * (1.0 / RWKV_HEAD)
    yc = y - mu
    var = _head_sum(yc * yc, ind, ind_t) * (1.0 / RWKV_HEAD)
    yn = yc * lax.rsqrt(var + RWKV_GN_EPS)
    o_ref[...] = ((yn * lng_ref[...] + lnb_ref[...] + bonus_ref[...]) * g_ref[...]).astype(BF16)


def _rwkv_post(y, bonus, g, ln_g, ln_b):
    t = bonus.shape[0]
    w = RWKV_WIDTH
    ind, ind_t = _head_indicator()
    tok = pl.BlockSpec((RW_ROWS, w), lambda i: (i, 0))
    full = lambda shape: pl.BlockSpec(shape, lambda i: (0,) * len(shape))
    return pl.pallas_call(
        _rwkv_post_kernel,
        grid=(t // RW_ROWS,),
        in_specs=[pl.BlockSpec((2, RW_ROWS, w), lambda i: (0, i, 0)), tok, tok, full((1, w)), full((1, w)),
                  full((w, 128)), full((128, w))],
        out_specs=tok,
        out_shape=jax.ShapeDtypeStruct((t, w), BF16),
        compiler_params=_params(("parallel",)),
        name="rwkv_post",
    )(y, bonus, g, ln_g.reshape(1, w), ln_b.reshape(1, w), ind, ind_t)


def _rwkv_mixer(proj_r, lp, s0_sample):
    r, kk, v, g, bonus, lw, kka, kd = _rwkv_prep(proj_r, lp)
    r2, y0, mneg, sadd, gc = _rwkv_chunks(r, kk, v, lw, kka, kd)
    s0 = jnp.concatenate([jnp.zeros((BATCH, 2, RW_PAIRS, 128, 128), F32), _pack_state(s0_sample)], axis=0)
    y, s_fin = _rwkv_state(r2, y0, mneg, sadd, gc, s0)
    return _rwkv_post(y, bonus, g, lp['rwkv_ln_g'], lp['rwkv_ln_b']), _unpack_state(s_fin[:BATCH])


MOE_ROWS = 256
MOE_BLOCKS = T_ALL * EXPERT_TOPK // MOE_ROWS + N_EXPERTS
MOE_PAD_ROWS = MOE_BLOCKS * MOE_ROWS
MOE_FF_TILE = 512
MOE_OUT_TILE = 2048


def _top2_sum(vals):
    best = None
    for a in range(len(vals)):
        for b in range(a + 1, len(vals)):
            s = vals[a] + vals[b]
            best = s if best is None else jnp.maximum(best, s)
    return best


def _first_argmax(vals):
    idx = jnp.zeros(vals[0].shape, jnp.int32)
    best = vals[0]
    for j in range(1, len(vals)):
        upd = vals[j] > best
        idx = jnp.where(upd, j, idx)
        best = jnp.where(upd, vals[j], best)
    return idx, best


def _pick(idx, vals):
    out = vals[-1]
    for j in range(len(vals) - 2, -1, -1):
        out = jnp.where(idx == j, vals[j], out)
    return out


def _ffn_norm_route_kernel(x_ref, g_ref, mod_ref, wh_ref, wl_ref, bias_ref, h_ref, idx_ref, wts_ref):
    y = _rms(x_ref[...], g_ref[...])
    h = y * (1.0 + mod_ref[0, 4:5, :]) + mod_ref[0, 3:4, :]
    h_ref[...] = h
    hh = h.astype(BF16)
    hl = (h - hh.astype(F32)).astype(BF16)
    wh, wl = wh_ref[...], wl_ref[...]
    logits = _dot_nt(wh, hh) + _dot_nt(wh, hl) + _dot_nt(wl, hh)
    scores = jax.nn.sigmoid(logits)
    sel = scores + bias_ref[...]
    gs = EXPERTS_PER_GROUP
    sel_rows = [sel[e:e + 1, :] for e in range(N_EXPERTS)]
    sc_rows = [scores[e:e + 1, :] for e in range(N_EXPERTS)]
    grp, _ = _first_argmax([_top2_sum(sel_rows[gs * g:gs * (g + 1)]) for g in range(N_EXPERT_GROUPS)])
    v = [_pick(grp, [sel_rows[gs * g + j] for g in range(N_EXPERT_GROUPS)]) for j in range(gs)]
    s = [_pick(grp, [sc_rows[gs * g + j] for g in range(N_EXPERT_GROUPS)]) for j in range(gs)]
    i1, _ = _first_argmax(v)
    i2, _ = _first_argmax([jnp.where(i1 == j, -jnp.inf, v[j]) for j in range(gs)])
    w1, w2 = _pick(i1, s), _pick(i2, s)
    tot = w1 + w2
    idx_ref[0:1, :] = grp * gs + i1
    idx_ref[1:2, :] = grp * gs + i2
    wts_ref[0:1, :] = w1 / tot
    wts_ref[1:2, :] = w2 / tot


def _ffn_norm_route(x, g, mod, w_router, router_bias):
    t = x.shape[0]
    wt = w_router.T
    wh = wt.astype(BF16)
    wl = (wt - wh.astype(F32)).astype(BF16)
    full = lambda shape: pl.BlockSpec(shape, lambda i: (0,) * len(shape))
    return pl.pallas_call(
        _ffn_norm_route_kernel,
        grid=(t // NORM_ROWS,),
        in_specs=[
            pl.BlockSpec((NORM_ROWS, D_MODEL), lambda i: (i, 0)),
            full((1, D_MODEL)),
            pl.BlockSpec((1, 6, D_MODEL), lambda i: (_cond_group(i, NORM_ROWS), 0, 0)),
            full((N_EXPERTS, D_MODEL)), full((N_EXPERTS, D_MODEL)), full((N_EXPERTS, 1)),
        ],
        out_specs=[pl.BlockSpec((NORM_ROWS, D_MODEL), lambda i: (i, 0)),
                   pl.BlockSpec((EXPERT_TOPK, NORM_ROWS), lambda i: (0, i)),
                   pl.BlockSpec((EXPERT_TOPK, NORM_ROWS), lambda i: (0, i))],
        out_shape=[jax.ShapeDtypeStruct((t, D_MODEL), F32),
                   jax.ShapeDtypeStruct((EXPERT_TOPK, t), jnp.int32),
                   jax.ShapeDtypeStruct((EXPERT_TOPK, t), F32)],
        compiler_params=_params(("parallel",)),
        name="ffn_norm_route",
    )(x, g.reshape(1, D_MODEL), mod, wh, wl, router_bias.reshape(N_EXPERTS, 1))


def _dispatch_plan(idx):
    t = idx.shape[1]
    flat_e = idx.T.reshape(-1)
    onehot = (flat_e[:, None] == jnp.arange(N_EXPERTS, dtype=jnp.int32)[None, :]).astype(jnp.int32)
    csum = jnp.cumsum(onehot, axis=0)
    rank = jnp.take_along_axis(csum, flat_e[:, None], axis=1)[:, 0] - 1
    counts = csum[-1]
    padded = ((counts + MOE_ROWS - 1) // MOE_ROWS) * MOE_ROWS
    pad_end = jnp.cumsum(padded)
    pad_start = pad_end - padded
    dest = (pad_start[flat_e] + rank).astype(jnp.int32)
    tok = jnp.arange(t * EXPERT_TOPK, dtype=jnp.int32) // EXPERT_TOPK
    row_tok = jnp.zeros((MOE_PAD_ROWS,), jnp.int32).at[dest].set(tok)
    block_row0 = jnp.arange(MOE_BLOCKS, dtype=jnp.int32) * MOE_ROWS
    block_e = jnp.minimum(jnp.sum((pad_end[None, :] <= block_row0[:, None]).astype(jnp.int32), axis=1),
                          N_EXPERTS - 1).astype(jnp.int32)
    n_used = (pad_end[-1] // MOE_ROWS).astype(jnp.int32).reshape(1)
    return row_tok, block_e, n_used, dest


DMA_UNROLL = 8


def _row_copy(src_hbm, row, dst, slot, sem):
    return pltpu.make_async_copy(src_hbm.at[pl.ds(row, 1), :], dst.at[pl.ds(slot, 1), :], sem)


def _start_rows(src_hbm, idx_ref, idx0, stride, n, dst, sem):
    def body(r, c):
        _row_copy(src_hbm, idx_ref[idx0 + stride * r], dst, r, sem).start()
        return c
    lax.fori_loop(0, n, body, 0, unroll=DMA_UNROLL)


def _wait_rows(src_hbm, n, dst, sem):
    def body(r, c):
        _row_copy(src_hbm, 0, dst, r, sem).wait()
        return c
    lax.fori_loop(0, n, body, 0, unroll=DMA_UNROLL)


def _gather_rows_kernel(tok_ref, nu_ref, h_hbm, o_ref, buf, sem):
    i = pl.program_id(0)
    n = nu_ref[0]

    @pl.when((i == 0) & (n > 0))
    def _():
        _start_rows(h_hbm, tok_ref, 0, 1, MOE_ROWS, buf.at[0], sem.at[0])

    @pl.when(i + 1 < n)
    def _():
        nxt = (i + 1) % 2
        _start_rows(h_hbm, tok_ref, (i + 1) * MOE_ROWS, 1, MOE_ROWS, buf.at[nxt], sem.at[nxt])

    @pl.when(i < n)
    def _():
        cur = i % 2
        _wait_rows(h_hbm, MOE_ROWS, buf.at[cur], sem.at[cur])
        o_ref[...] = buf[cur].astype(BF16)

    @pl.when(i >= n)
    def _():
        o_ref[...] = jnp.zeros_like(o_ref)


def _gather_rows(h, row_tok, n_used):
    return pl.pallas_call(
        _gather_rows_kernel,
        grid_spec=pltpu.PrefetchScalarGridSpec(
            num_scalar_prefetch=2,
            grid=(MOE_BLOCKS,),
            in_specs=[pl.BlockSpec(memory_space=pl.ANY)],
            out_specs=pl.BlockSpec((MOE_ROWS, D_MODEL), lambda i, tok, nu: (i, 0)),
            scratch_shapes=[pltpu.VMEM((2, MOE_ROWS, D_MODEL), F32), pltpu.SemaphoreType.DMA((2,))],
        ),
        out_shape=jax.ShapeDtypeStruct((MOE_PAD_ROWS, D_MODEL), BF16),
        compiler_params=_params(("arbitrary",)),
        name="moe_gather",
    )(row_tok, n_used, h)


def _expert_up_kernel(be_ref, nu_ref, x_ref, w1_ref, w3_ref, o_ref):
    i = pl.program_id(1)

    @pl.when(i < nu_ref[0])
    def _():
        x = x_ref[...]
        a = _dot(x, w1_ref[0, 0].astype(BF16))
        b = _dot(x, w3_ref[0, 0].astype(BF16))
        o_ref[...] = (a * jax.nn.sigmoid(a) * b).astype(BF16)

    @pl.when(i >= nu_ref[0])
    def _():
        o_ref[...] = jnp.zeros_like(o_ref)


def _expert_up(xb, w_gate, w_up, layer, block_e, n_used):
    wspec = pl.BlockSpec((1, 1, D_MODEL, MOE_FF_TILE), lambda j, i, be, nu: (layer, be[i], 0, j))
    return pl.pallas_call(
        _expert_up_kernel,
        grid_spec=pltpu.PrefetchScalarGridSpec(
            num_scalar_prefetch=2,
            grid=(EXPERT_FF // MOE_FF_TILE, MOE_BLOCKS),
            in_specs=[pl.BlockSpec((MOE_ROWS, D_MODEL), lambda j, i, be, nu: (i, 0)), wspec, wspec],
            out_specs=pl.BlockSpec((MOE_ROWS, MOE_FF_TILE), lambda j, i, be, nu: (i, j)),
        ),
        out_shape=jax.ShapeDtypeStruct((MOE_PAD_ROWS, EXPERT_FF), BF16),
        compiler_params=_params(("parallel", "arbitrary")),
        name="moe_up",
    )(block_e, n_used, xb, w_gate, w_up)


def _expert_down_kernel(be_ref, nu_ref, h_ref, w2_ref, o_ref):
    i = pl.program_id(1)

    @pl.when(i < nu_ref[0])
    def _():
        o_ref[...] = _dot(h_ref[...], w2_ref[0, 0].astype(BF16))

    @pl.when(i >= nu_ref[0])
    def _():
        o_ref[...] = jnp.zeros_like(o_ref)


def _expert_down(hmid, w_down, layer, block_e, n_used):
    return pl.pallas_call(
        _expert_down_kernel,
        grid_spec=pltpu.PrefetchScalarGridSpec(
            num_scalar_prefetch=2,
            grid=(D_MODEL // MOE_OUT_TILE, MOE_BLOCKS),
            in_specs=[
                pl.BlockSpec((MOE_ROWS, EXPERT_FF), lambda j, i, be, nu: (i, 0)),
                pl.BlockSpec((1, 1, EXPERT_FF, MOE_OUT_TILE), lambda j, i, be, nu: (layer, be[i], 0, j)),
            ],
            out_specs=pl.BlockSpec((MOE_ROWS, MOE_OUT_TILE), lambda j, i, be, nu: (i, j)),
        ),
        out_shape=jax.ShapeDtypeStruct((MOE_PAD_ROWS, D_MODEL), F32),
        compiler_params=_params(("parallel", "arbitrary")),
        name="moe_down",
    )(block_e, n_used, hmid, w_down)


COMBINE_ROWS = 256


def _combine_kernel(pos_ref, yb_hbm, x_ref, w_ref, mod_ref, o_ref, buf, sem):
    i = pl.program_id(0)
    n = pl.num_programs(0)
    per_block = COMBINE_ROWS * EXPERT_TOPK

    def start(blk, slot):
        for k in range(EXPERT_TOPK):
            _start_rows(yb_hbm, pos_ref, blk * per_block + k, EXPERT_TOPK, COMBINE_ROWS, buf.at[slot, k],
                        sem.at[slot])

    @pl.when(i == 0)
    def _():
        start(0, 0)

    @pl.when(i + 1 < n)
    def _():
        start(i + 1, (i + 1) % 2)

    cur = i % 2
    for k in range(EXPERT_TOPK):
        _wait_rows(yb_hbm, COMBINE_ROWS, buf.at[cur, k], sem.at[cur])
    y = buf[cur, 0] * w_ref[:, 0:1] + buf[cur, 1] * w_ref[:, 1:2]
    o_ref[...] = x_ref[...] + mod_ref[0, 5:6, :] * y


def _combine(yb, dest, wts, x, mod):
    t = x.shape[0]
    return pl.pallas_call(
        _combine_kernel,
        grid_spec=pltpu.PrefetchScalarGridSpec(
            num_scalar_prefetch=1,
            grid=(t // COMBINE_ROWS,),
            in_specs=[
                pl.BlockSpec(memory_space=pl.ANY),
                pl.BlockSpec((COMBINE_ROWS, D_MODEL), lambda i, pos: (i, 0)),
                pl.BlockSpec((COMBINE_ROWS, EXPERT_TOPK), lambda i, pos: (i, 0)),
                pl.BlockSpec((1, 6, D_MODEL), lambda i, pos: (_cond_group(i, COMBINE_ROWS), 0, 0)),
            ],
            out_specs=pl.BlockSpec((COMBINE_ROWS, D_MODEL), lambda i, pos: (i, 0)),
            scratch_shapes=[pltpu.VMEM((2, EXPERT_TOPK, COMBINE_ROWS, D_MODEL), F32),
                            pltpu.SemaphoreType.DMA((2,))],
        ),
        out_shape=jax.ShapeDtypeStruct((t, D_MODEL), F32),
        compiler_params=_params(("arbitrary",)),
        name="moe_combine",
    )(dest, yb, x, wts.T, mod)


def _moe_residual(x, g, mod, lp_router, w_gate, w_up, w_down, layer):
    h, idx, wts = _ffn_norm_route(x, g, mod, *lp_router)
    row_tok, block_e, n_used, dest = _dispatch_plan(idx)
    xb = _gather_rows(h, row_tok, n_used)
    hmid = _expert_up(xb, w_gate, w_up, layer, block_e, n_used)
    yb = _expert_down(hmid, w_down, layer, block_e, n_used)
    return _combine(yb, dest, wts, x, mod)


def _layer_params(args, layer):
    return {k: v[layer] for k, v in args.items()}


def kernel(x_prompt, x_sample, cache_na_k, cache_na_v, cache_gqa_k, cache_gqa_v, state_ssm, state_rwkv, c, c_ctx,
           w_ada, b_ada, norm_mix, norm_ffn, w_in, na_rpb, gqa_q_norm, gqa_k_norm, ssm_lam_re, ssm_lam_im,
           ssm_log_dt, ssm_b_re, ssm_b_im, ssm_c_re, ssm_c_im, ssm_d, ssm_w_glu, rwkv_mu, rwkv_w0, rwkv_w2, rwkv_a0,
           rwkv_a2, rwkv_k_k, rwkv_k_a, rwkv_r_k, rwkv_g2, rwkv_ln_g, rwkv_ln_b, w_branch, w_out, w_router,
           router_bias, w_exp_gate, w_exp_up, w_exp_down, norm_final):
    x = jnp.concatenate([x_prompt.reshape(T_PROMPT, D_MODEL), x_sample.reshape(T_SAMPLE, D_MODEL)], axis=0)
    cond = jnp.concatenate([c_ctx[None, :], c, jnp.zeros((COND_PAD - N_COND, D_MODEL), F32)], axis=0)
    mod_all = _ada_all(cond, w_ada, b_ada).reshape(DEPTH, COND_PAD, 6, D_MODEL)
    cos, sin = _rope_tables()
    rwkv_args = dict(rwkv_mu=rwkv_mu, rwkv_w0=rwkv_w0, rwkv_w2=rwkv_w2, rwkv_a0=rwkv_a0, rwkv_a2=rwkv_a2,
                     rwkv_k_k=rwkv_k_k, rwkv_k_a=rwkv_k_a, rwkv_r_k=rwkv_r_k.reshape(DEPTH, RWKV_WIDTH),
                     rwkv_g2=rwkv_g2, rwkv_ln_g=rwkv_ln_g, rwkv_ln_b=rwkv_ln_b)
    caches = [[] for _ in range(6)]
    for l in range(DEPTH):
        mod = mod_all[l]
        h = _norm_mod(x, norm_mix[l], mod, 0)
        proj = _in_proj(h, w_in, l, 0, OFF_RWKV)
        proj_r = _in_proj(h, w_in, l, OFF_RWKV, RWKV_IN_W)
        proj_g = _in_proj(h, w_in, l, OFF_GATE, N_BRANCH * D_MODEL)

        y_na_p, nk, nv = _ctx_na(proj)
        y_gqa_p, gk, gv = _ctx_gqa(proj, gqa_q_norm[l], gqa_k_norm[l])
        y_na_s = _lat_na(proj, cache_na_k, cache_na_v, l, _na_band_bias(na_rpb[l]))
        y_gqa_s = _lat_gqa(proj, cache_gqa_k, cache_gqa_v, l, gqa_q_norm[l], gqa_k_norm[l], cos, sin)

        a, bb, cc = _s5_params(ssm_lam_re[l], ssm_lam_im[l], ssm_log_dt[l], ssm_b_re[l], ssm_b_im[l],
                               ssm_c_re[l], ssm_c_im[l])
        h0 = jnp.concatenate([jnp.zeros((BATCH, 2, 2, S5_SLABS, 128), F32),
                              state_ssm[:, l].reshape(DEC_BATCH, 2, 2, S5_SLABS, 128)], axis=0)
        y_scan, ssm_fin = _s5_scan(proj, a, bb, cc, h0)
        ssm_state = ssm_fin[:BATCH]
        y_ssm = _s5_glu(proj, 0, y_scan, ssm_d, ssm_w_glu, l)

        y_rwkv, rwkv_state = _rwkv_mixer(proj_r, _layer_params(rwkv_args, l), state_rwkv[:, l])

        merged = _branch_merge((y_na_p, y_na_s), (y_gqa_p, y_gqa_s), y_ssm, y_rwkv, proj_g, w_branch, l)
        x = _out_proj_residual(merged, w_out, l, x, mod, 2)
        x = _moe_residual(x, norm_ffn[l], mod, (w_router, router_bias), w_exp_gate, w_exp_up, w_exp_down, l)

        for lst, val in zip(caches, (nk, nv, gk, gv, ssm_state.reshape(BATCH, 2, 2, SSM_GROUPS, SSM_STATE),
                                     rwkv_state)):
            lst.append(val)

    y = _final_norm(x, norm_final)
    outs = [jnp.stack(lst, axis=1) for lst in caches]
    return (y[:T_PROMPT].reshape(BATCH, SEQ, D_MODEL), y[T_PROMPT:].reshape(DEC_BATCH, DEC_SEQ, D_MODEL), *outs)
```

```python
import functools

import numpy as np
import jax
import jax.numpy as jnp
from jax import lax
from jax.experimental import pallas as pl
from jax.experimental.pallas import tpu as pltpu

F32 = jnp.float32
BF16 = jnp.bfloat16

D_MODEL = 4096
BATCH = 16
SEQ = 256
DEPTH = 4
DEC_BATCH = 2
DEC_SEQ = 1024
PAST_LEN = 512
GRID_W = 64
HEAD_DIM = 128
EPS = 1e-6
NEG = -1e30
NA_HEADS = 8
NA_WIN_ROWS = 8
NA_WIN_COLS = 16
NA_W = NA_HEADS * HEAD_DIM
GQA_HEADS = 8
GQA_KV_HEADS = 2
GQA_GROUP = GQA_HEADS // GQA_KV_HEADS
GQA_Q_W = GQA_HEADS * HEAD_DIM
GQA_KV_W = GQA_KV_HEADS * HEAD_DIM
GQA_QKV_W = GQA_Q_W + 2 * GQA_KV_W
ROPE_THETA = 10000.0
SSM_WIDTH = 1024
SSM_GROUP = 16
SSM_GROUPS = SSM_WIDTH // SSM_GROUP
SSM_STATE = 64
RWKV_WIDTH = 1024
RWKV_HEAD = 64
RWKV_HEADS = RWKV_WIDTH // RWKV_HEAD
DECAY_RANK = 64
ICLR_RANK = 64
GATE_RANK = 128
RWKV_IN_W = 3 * RWKV_WIDTH + DECAY_RANK + ICLR_RANK + GATE_RANK
RWKV_GN_EPS = 64e-5
N_BRANCH = 4
OFF_NA = 0
OFF_GQA = 3 * NA_W
OFF_SSM = OFF_GQA + GQA_QKV_W
OFF_RWKV = OFF_SSM + SSM_WIDTH
OFF_GATE = OFF_RWKV + RWKV_IN_W
IN_W = OFF_GATE + N_BRANCH * D_MODEL
N_EXPERTS = 16
N_EXPERT_GROUPS = 4
EXPERTS_PER_GROUP = N_EXPERTS // N_EXPERT_GROUPS
EXPERT_TOPK = 2
EXPERT_FF = 1024

T_PROMPT = BATCH * SEQ
T_SAMPLE = DEC_BATCH * DEC_SEQ
T_ALL = T_PROMPT + T_SAMPLE
N_COND = 1 + DEC_BATCH
COND_PAD = 8
ATT_SCALE = HEAD_DIM ** -0.5

VMEM_LIMIT = 56 * 1024 * 1024


def _params(sem):
    return pltpu.CompilerParams(dimension_semantics=sem, vmem_limit_bytes=VMEM_LIMIT)


def _cond_group(i, rows_per_block):
    n_prompt_blocks = T_PROMPT // rows_per_block
    blocks_per_seq = DEC_SEQ // rows_per_block
    return jnp.where(i < n_prompt_blocks, 0, 1 + (i - n_prompt_blocks) // blocks_per_seq)


def _dot(a, b):
    return jnp.dot(a, b, preferred_element_type=F32)


def _dot_nt(a, b):
    return lax.dot_general(a, b, (((1,), (1,)), ((), ())), preferred_element_type=F32)


def _dot_tn(a, b):
    return lax.dot_general(a, b, (((0,), (0,)), ((), ())), preferred_element_type=F32)


def _split3(x):
    hi = x.astype(BF16)
    r1 = x - hi.astype(F32)
    mid = r1.astype(BF16)
    lo = (r1 - mid.astype(F32)).astype(BF16)
    return hi, mid, lo


def _dot_exact_rhs(a_bf16, x):
    hi, mid, lo = _split3(x)
    return _dot(a_bf16, hi) + _dot(a_bf16, mid) + _dot(a_bf16, lo)


def _dot_exact_lhs(x, b_bf16):
    hi, mid, lo = _split3(x)
    return _dot(hi, b_bf16) + _dot(mid, b_bf16) + _dot(lo, b_bf16)


def _rms(x, g):
    return x * lax.rsqrt(jnp.mean(x * x, axis=-1, keepdims=True) + EPS) * g


def _softmax_rows(s):
    m = jnp.max(s, axis=-1, keepdims=True)
    e = jnp.exp(s - m)
    return e / jnp.sum(e, axis=-1, keepdims=True)


def _ada_kernel(c_ref, w_ref, b_ref, o_ref):
    c = c_ref[...]
    s = (c * jax.nn.sigmoid(c)).astype(BF16)
    o_ref[0] = _dot(s, w_ref[0].astype(BF16)) + b_ref[0]


def _ada_all(cond, w_ada, b_ada):
    tn = 512
    n6 = 6 * D_MODEL
    return pl.pallas_call(
        _ada_kernel,
        grid=(DEPTH, n6 // tn),
        in_specs=[
            pl.BlockSpec((COND_PAD, D_MODEL), lambda l, j: (0, 0)),
            pl.BlockSpec((1, D_MODEL, tn), lambda l, j: (l, 0, j)),
            pl.BlockSpec((1, 1, tn), lambda l, j: (l, 0, j)),
        ],
        out_specs=pl.BlockSpec((1, COND_PAD, tn), lambda l, j: (l, 0, j)),
        out_shape=jax.ShapeDtypeStruct((DEPTH, COND_PAD, n6), F32),
        compiler_params=_params(("parallel", "parallel")),
        name="ada",
    )(cond, w_ada, b_ada.reshape(DEPTH, 1, n6))


NORM_ROWS = 256


def _norm_mod_kernel(x_ref, g_ref, mod_ref, o_ref, *, shift_idx):
    y = _rms(x_ref[...], g_ref[...])
    scale = mod_ref[0, shift_idx + 1:shift_idx + 2, :]
    shift = mod_ref[0, shift_idx:shift_idx + 1, :]
    o_ref[...] = (y * (1.0 + scale) + shift).astype(o_ref.dtype)


def _norm_mod(x, g, mod, shift_idx, out_dtype=BF16):
    t = x.shape[0]
    return pl.pallas_call(
        functools.partial(_norm_mod_kernel, shift_idx=shift_idx),
        grid=(t // NORM_ROWS,),
        in_specs=[
            pl.BlockSpec((NORM_ROWS, D_MODEL), lambda i: (i, 0)),
            pl.BlockSpec((1, D_MODEL), lambda i: (0, 0)),
            pl.BlockSpec((1, 6, D_MODEL), lambda i: (_cond_group(i, NORM_ROWS), 0, 0)),
        ],
        out_specs=pl.BlockSpec((NORM_ROWS, D_MODEL), lambda i: (i, 0)),
        out_shape=jax.ShapeDtypeStruct((t, D_MODEL), out_dtype),
        compiler_params=_params(("parallel",)),
        name="norm_mod",
    )(x, g.reshape(1, D_MODEL), mod)


def _final_norm_kernel(x_ref, g_ref, o_ref):
    o_ref[...] = _rms(x_ref[...], g_ref[...])


def _final_norm(x, g):
    t = x.shape[0]
    return pl.pallas_call(
        _final_norm_kernel,
        grid=(t // NORM_ROWS,),
        in_specs=[pl.BlockSpec((NORM_ROWS, D_MODEL), lambda i: (i, 0)),
                  pl.BlockSpec((1, D_MODEL), lambda i: (0, 0))],
        out_specs=pl.BlockSpec((NORM_ROWS, D_MODEL), lambda i: (i, 0)),
        out_shape=jax.ShapeDtypeStruct((t, D_MODEL), F32),
        compiler_params=_params(("parallel",)),
        name="final_norm",
    )(x, g.reshape(1, D_MODEL))


MM_TM = 1024
MM_TN = 256
IN_TM = 2048
OUT_TN = 512


def _proj_kernel(x_ref, w_ref, o_ref):
    o_ref[...] = _dot(x_ref[...], w_ref[0].astype(BF16))


def _in_proj(h, w_in, layer, col0, n_cols):
    t = h.shape[0]
    blk0 = col0 // MM_TN
    return pl.pallas_call(
        _proj_kernel,
        grid=(t // IN_TM, n_cols // MM_TN),
        in_specs=[
            pl.BlockSpec((IN_TM, D_MODEL), lambda i, j: (i, 0)),
            pl.BlockSpec((1, D_MODEL, MM_TN), lambda i, j: (layer, 0, blk0 + j)),
        ],
        out_specs=pl.BlockSpec((IN_TM, MM_TN), lambda i, j: (i, j)),
        out_shape=jax.ShapeDtypeStruct((t, n_cols), F32),
        compiler_params=_params(("parallel", "parallel")),
        name="in_proj",
    )(h, w_in)


def _merge_kernel(y0p, y0s, y1p, y1s, y2, y3, wb_ref, g0, g1, g2, g3, o_ref):
    in_prompt = pl.program_id(0) < T_PROMPT // MM_TM
    ys = (jnp.where(in_prompt, y0p[...], y0s[...]), jnp.where(in_prompt, y1p[...], y1s[...]), y2[...], y3[...])
    acc = None
    for n, (y, g) in enumerate(zip(ys, (g0, g1, g2, g3))):
        br = _dot(y, wb_ref[0, n].astype(BF16))
        term = jax.nn.sigmoid(g[...]) * br
        acc = term if acc is None else acc + term
    o_ref[...] = acc.astype(BF16)


def _branch_merge(y_na, y_gqa, y_ssm, y_rwkv, proj, w_branch, layer):
    t = proj.shape[0]
    bw = y_ssm.shape[1]
    per_branch = D_MODEL // MM_TN
    prompt_blocks = T_PROMPT // MM_TM
    y_spec = pl.BlockSpec((MM_TM, bw), lambda i, j: (i, 0))
    yp_spec = pl.BlockSpec((MM_TM, bw), lambda i, j: (jnp.minimum(i, prompt_blocks - 1), 0))
    ys_spec = pl.BlockSpec((MM_TM, bw), lambda i, j: (jnp.maximum(i - prompt_blocks, 0), 0))

    def gate_spec(n):
        return pl.BlockSpec((MM_TM, MM_TN), lambda i, j: (i, n * per_branch + j))

    return pl.pallas_call(
        _merge_kernel,
        grid=(t // MM_TM, D_MODEL // MM_TN),
        in_specs=[yp_spec, ys_spec, yp_spec, ys_spec, y_spec, y_spec]
        + [pl.BlockSpec((1, N_BRANCH, bw, MM_TN), lambda i, j: (layer, 0, 0, j))]
        + [gate_spec(n) for n in range(N_BRANCH)],
        out_specs=pl.BlockSpec((MM_TM, MM_TN), lambda i, j: (i, j)),
        out_shape=jax.ShapeDtypeStruct((t, D_MODEL), BF16),
        compiler_params=_params(("parallel", "parallel")),
        name="branch_merge",
    )(*y_na, *y_gqa, y_ssm, y_rwkv, w_branch, proj, proj, proj, proj)


def _out_proj_kernel(m_ref, w_ref, x_ref, mod_ref, o_ref, *, gate_idx):
    out = _dot(m_ref[...], w_ref[0].astype(BF16))
    o_ref[...] = x_ref[...] + mod_ref[0, gate_idx:gate_idx + 1, :] * out


def _out_proj_residual(merged, w_out, layer, x, mod, gate_idx):
    t = x.shape[0]
    tn = OUT_TN
    return pl.pallas_call(
        functools.partial(_out_proj_kernel, gate_idx=gate_idx),
        grid=(t // MM_TM, D_MODEL // tn),
        in_specs=[
            pl.BlockSpec((MM_TM, D_MODEL), lambda i, j: (i, 0)),
            pl.BlockSpec((1, D_MODEL, tn), lambda i, j: (layer, 0, j)),
            pl.BlockSpec((MM_TM, tn), lambda i, j: (i, j)),
            pl.BlockSpec((1, 6, tn), lambda i, j: (_cond_group(i, MM_TM), 0, j)),
        ],
        out_specs=pl.BlockSpec((MM_TM, tn), lambda i, j: (i, j)),
        out_shape=jax.ShapeDtypeStruct((t, D_MODEL), F32),
        compiler_params=_params(("parallel", "parallel")),
        name="out_proj",
    )(merged, w_out, x, mod)


def _attend(q, k, v):
    p = _softmax_rows(_dot_nt(q, k) * ATT_SCALE)
    return _dot(p.astype(BF16), v)


def _ctx_na_kernel(q_ref, k_ref, v_ref, y_ref, ko_ref, vo_ref):
    for h in range(NA_HEADS):
        sl = slice(HEAD_DIM * h, HEAD_DIM * (h + 1))
        kf = k_ref[:, sl]
        vf = v_ref[:, sl]
        y = _attend(q_ref[:, sl].astype(BF16), kf.astype(BF16), vf.astype(BF16))
        y_ref[:, sl] = y.astype(BF16)
        ko_ref[0, h] = kf
        vo_ref[0, h] = vf


def _ctx_na(proj):
    cache = jax.ShapeDtypeStruct((BATCH, NA_HEADS, SEQ, HEAD_DIM), F32)
    cache_spec = pl.BlockSpec((1, NA_HEADS, SEQ, HEAD_DIM), lambda b: (b, 0, 0, 0))
    return pl.pallas_call(
        _ctx_na_kernel,
        grid=(BATCH,),
        in_specs=[pl.BlockSpec((SEQ, NA_W), lambda b, c=c: (b, c)) for c in range(3)],
        out_specs=[pl.BlockSpec((SEQ, NA_W), lambda b: (b, 0)), cache_spec, cache_spec],
        out_shape=[jax.ShapeDtypeStruct((T_PROMPT, NA_W), BF16), cache, cache],
        compiler_params=_params(("parallel",)),
        name="ctx_na",
    )(proj, proj, proj)


def _ctx_gqa_kernel(q_ref, k_ref, v_ref, gq_ref, gk_ref, y_ref, ko_ref, vo_ref):
    for kv in range(GQA_KV_HEADS):
        sl = slice(HEAD_DIM * kv, HEAD_DIM * (kv + 1))
        kn = _rms(k_ref[:, sl], gk_ref[...])
        vf = v_ref[:, sl]
        ko_ref[0, kv] = kn
        vo_ref[0, kv] = vf
        kb = kn.astype(BF16)
        vb = vf.astype(BF16)
        for g in range(GQA_GROUP):
            h = kv * GQA_GROUP + g
            hs = slice(HEAD_DIM * h, HEAD_DIM * (h + 1))
            qn = _rms(q_ref[:, hs], gq_ref[...])
            y_ref[:, hs] = _attend(qn.astype(BF16), kb, vb).astype(BF16)


def _ctx_gqa(proj, gq, gk):
    cache = jax.ShapeDtypeStruct((BATCH, GQA_KV_HEADS, SEQ, HEAD_DIM), F32)
    cache_spec = pl.BlockSpec((1, GQA_KV_HEADS, SEQ, HEAD_DIM), lambda b: (b, 0, 0, 0))
    norm_spec = pl.BlockSpec((1, HEAD_DIM), lambda b: (0, 0))
    return pl.pallas_call(
        _ctx_gqa_kernel,
        grid=(BATCH,),
        in_specs=[
            pl.BlockSpec((SEQ, GQA_Q_W), lambda b: (b, OFF_GQA // GQA_Q_W)),
            pl.BlockSpec((SEQ, GQA_KV_W), lambda b: (b, (OFF_GQA + GQA_Q_W) // GQA_KV_W)),
            pl.BlockSpec((SEQ, GQA_KV_W), lambda b: (b, (OFF_GQA + GQA_Q_W) // GQA_KV_W + 1)),
            norm_spec, norm_spec,
        ],
        out_specs=[pl.BlockSpec((SEQ, GQA_Q_W), lambda b: (b, 0)), cache_spec, cache_spec],
        out_shape=[jax.ShapeDtypeStruct((T_PROMPT, GQA_Q_W), BF16), cache, cache],
        compiler_params=_params(("parallel",)),
        name="ctx_gqa",
    )(proj, proj, proj, gq.reshape(1, HEAD_DIM), gk.reshape(1, HEAD_DIM))


LAT_Q_ROWS = 256
LAT_ROW0 = T_PROMPT // DEC_SEQ


def _rope_tables():
    t = np.arange(DEC_SEQ)
    half = HEAD_DIM // 2
    inv = jnp.asarray(ROPE_THETA, F32) ** (-jnp.arange(0, half, 2, dtype=F32) / half)
    rows = jnp.asarray(t // GRID_W, F32)
    cols = jnp.asarray(t % GRID_W, F32)
    ang_r = rows[:, None] * inv[None, :]
    ang_c = cols[:, None] * inv[None, :]
    cos = jnp.concatenate([jnp.cos(ang_r)] * 2 + [jnp.cos(ang_c)] * 2, axis=-1)
    sin = jnp.concatenate([-jnp.sin(ang_r), jnp.sin(ang_r), -jnp.sin(ang_c), jnp.sin(ang_c)], axis=-1)
    return cos, sin


def _rope(x, cos, sin):
    q = HEAD_DIM // 4
    lane = lax.broadcasted_iota(jnp.int32, x.shape, 1)
    partner = jnp.where((lane % (2 * q)) < q, pltpu.roll(x, HEAD_DIM - q, 1), pltpu.roll(x, q, 1))
    return x * cos + partner * sin


def _lat_gqa_kernel(q_ref, k_ref, v_ref, ck_ref, cv_ref, gq_ref, gk_ref, cos_ref, sin_ref, y_ref, kall, vall):
    cos = cos_ref[...]
    sin = sin_ref[...]
    kall[0:PAST_LEN, :] = ck_ref[0, 0, 0].astype(BF16)
    vall[0:PAST_LEN, :] = cv_ref[0, 0, 0].astype(BF16)
    kall[PAST_LEN:, :] = _rope(_rms(k_ref[...], gk_ref[...]), cos, sin).astype(BF16)
    vall[PAST_LEN:, :] = v_ref[...].astype(BF16)
    for g in range(GQA_GROUP):
        hs = slice(HEAD_DIM * g, HEAD_DIM * (g + 1))
        for qb in range(DEC_SEQ // LAT_Q_ROWS):
            rs = slice(LAT_Q_ROWS * qb, LAT_Q_ROWS * (qb + 1))
            qn = _rope(_rms(q_ref[rs, hs], gq_ref[...]), cos[rs], sin[rs])
            y_ref[rs, hs] = _attend(qn.astype(BF16), kall[...], vall[...]).astype(BF16)


def _lat_gqa(proj, cache_k, cache_v, layer, gq, gk, cos, sin):
    group_w = GQA_GROUP * HEAD_DIM
    kcol = (OFF_GQA + GQA_Q_W) // HEAD_DIM
    cache_spec = pl.BlockSpec((1, 1, 1, PAST_LEN, HEAD_DIM), lambda b, kv: (b, layer, kv, 0, 0))
    norm_spec = pl.BlockSpec((1, HEAD_DIM), lambda b, kv: (0, 0))
    tab_spec = pl.BlockSpec((DEC_SEQ, HEAD_DIM), lambda b, kv: (0, 0))
    return pl.pallas_call(
        _lat_gqa_kernel,
        grid=(DEC_BATCH, GQA_KV_HEADS),
        in_specs=[
            pl.BlockSpec((DEC_SEQ, group_w), lambda b, kv: (LAT_ROW0 + b, OFF_GQA // group_w + kv)),
            pl.BlockSpec((DEC_SEQ, HEAD_DIM), lambda b, kv: (LAT_ROW0 + b, kcol + kv)),
            pl.BlockSpec((DEC_SEQ, HEAD_DIM), lambda b, kv: (LAT_ROW0 + b, kcol + GQA_KV_HEADS + kv)),
            cache_spec, cache_spec, norm_spec, norm_spec, tab_spec, tab_spec,
        ],
        out_specs=pl.BlockSpec((DEC_SEQ, group_w), lambda b, kv: (b, kv)),
        out_shape=jax.ShapeDtypeStruct((T_SAMPLE, GQA_Q_W), BF16),
        scratch_shapes=[pltpu.VMEM((PAST_LEN + DEC_SEQ, HEAD_DIM), BF16)] * 2,
        compiler_params=_params(("parallel", "parallel")),
        name="lat_gqa",
    )(proj, proj, proj, cache_k, cache_v, gq.reshape(1, HEAD_DIM), gk.reshape(1, HEAD_DIM), cos, sin)


NA_ROWS = DEC_SEQ // GRID_W
NA_WR = min(NA_WIN_ROWS, NA_ROWS)
NA_BAND = NA_WR * GRID_W
NA_DR = 2 * NA_WIN_ROWS - 1
NA_DC = 2 * NA_WIN_COLS - 1
NA_GROUP_ROWS = 5


def _na_row_window(r):
    r0 = min(max(r - NA_WR // 2, 0), NA_ROWS - NA_WR)
    return r0, r0 - r + NA_WIN_ROWS - 1


def _na_table_kernel(rpb_ref, sel_ref, ok_ref, o_ref):
    picked = _dot_exact_lhs(rpb_ref[...], sel_ref[...])
    o_ref[...] = jnp.where(ok_ref[...] > 0.5, picked, NEG)


def _na_band_bias(rpb):
    c = np.arange(GRID_W)
    dc = c[None, :] - c[:, None] + NA_WIN_COLS - 1
    c0 = np.clip(c - NA_WIN_COLS // 2, 0, GRID_W - NA_WIN_COLS)
    ok = (c[None, :] >= c0[:, None]) & (c[None, :] < c0[:, None] + NA_WIN_COLS)
    sel = np.zeros((128, GRID_W * GRID_W), np.float32)
    flat_dc = np.clip(dc, 0, NA_DC - 1).reshape(-1)
    sel[flat_dc, np.arange(GRID_W * GRID_W)] = 1.0
    n = NA_HEADS * NA_DR
    rpb2 = jnp.pad(rpb.reshape(n, NA_DC).astype(F32), ((0, 0), (0, 128 - NA_DC)))
    full = lambda shape: pl.BlockSpec(shape, lambda: (0,) * len(shape))
    table = pl.pallas_call(
        _na_table_kernel,
        in_specs=[full((n, 128)), full((128, GRID_W * GRID_W)), full((1, GRID_W * GRID_W))],
        out_specs=full((n, GRID_W * GRID_W)),
        out_shape=jax.ShapeDtypeStruct((n, GRID_W * GRID_W), F32),
        name="na_table",
    )(rpb2, jnp.asarray(sel, BF16), jnp.asarray(ok.reshape(1, -1), F32))
    table = table.reshape(NA_HEADS, NA_DR, GRID_W, GRID_W)
    neg = jnp.full((NA_HEADS, GRID_W, GRID_W), NEG, F32)
    tiles = []
    for a0 in range(NA_DR - NA_WR + 1):
        band = table[:, a0:a0 + NA_WR].transpose(0, 2, 1, 3).reshape(NA_HEADS, GRID_W, NA_BAND)
        tiles += [jnp.concatenate([band, neg], axis=-1), jnp.concatenate([neg, band], axis=-1)]
    return jnp.stack(tiles, axis=1)


def _na_groups():
    groups, r = [], 0
    while r < NA_ROWS:
        k0, end = _na_row_window(r)[0], r + 1
        while (end < NA_ROWS and end - r < NA_GROUP_ROWS
               and _na_row_window(end)[0] + NA_WR - k0 <= NA_WR + 1):
            end += 1
        groups.append((r, end, k0, _na_row_window(end - 1)[0] + NA_WR - k0))
        r = end
    return groups


def _lat_na_kernel(q_ref, k_ref, v_ref, ck_ref, cv_ref, bias_ref, y_ref):
    kb = k_ref[...].astype(BF16)
    vb = v_ref[...].astype(BF16)
    ck = ck_ref[0, 0, 0].astype(BF16)
    cv = cv_ref[0, 0, 0].astype(BF16)
    for ra, rb, k0, nk in _na_groups():
        rs = slice(GRID_W * ra, GRID_W * rb)
        ks = slice(GRID_W * k0, GRID_W * (k0 + nk))
        q = q_ref[rs, :].astype(BF16)
        bias = jnp.concatenate(
            [bias_ref[0, 2 * _na_row_window(r)[1] + (_na_row_window(r)[0] - k0), :, 0:GRID_W * nk]
             for r in range(ra, rb)], axis=0)
        s_loc = _dot_nt(q, kb[ks]) * ATT_SCALE + bias
        s_ctx = _dot_nt(q, ck) * ATT_SCALE
        m = jnp.maximum(jnp.max(s_loc, axis=-1, keepdims=True), jnp.max(s_ctx, axis=-1, keepdims=True))
        e_loc = jnp.exp(s_loc - m)
        e_ctx = jnp.exp(s_ctx - m)
        den = jnp.sum(e_loc, axis=-1, keepdims=True) + jnp.sum(e_ctx, axis=-1, keepdims=True)
        out = _dot((e_loc / den).astype(BF16), vb[ks]) + _dot((e_ctx / den).astype(BF16), cv)
        y_ref[rs, :] = out.astype(BF16)


def _lat_na(proj, cache_k, cache_v, layer, bias):
    cache_spec = pl.BlockSpec((1, 1, 1, PAST_LEN, HEAD_DIM), lambda h, b: (b, layer, h, 0, 0))

    def qkv_spec(c):
        return pl.BlockSpec((DEC_SEQ, HEAD_DIM), lambda h, b: (LAT_ROW0 + b, c * NA_HEADS + h))

    return pl.pallas_call(
        _lat_na_kernel,
        grid=(NA_HEADS, DEC_BATCH),
        in_specs=[qkv_spec(0), qkv_spec(1), qkv_spec(2), cache_spec, cache_spec,
                  pl.BlockSpec((1, 2 * (NA_DR - NA_WR + 1), GRID_W, NA_BAND + GRID_W), lambda h, b: (h, 0, 0, 0))],
        out_specs=pl.BlockSpec((DEC_SEQ, HEAD_DIM), lambda h, b: (b, h)),
        out_shape=jax.ShapeDtypeStruct((T_SAMPLE, NA_W), BF16),
        compiler_params=_params(("parallel", "parallel")),
        name="lat_na",
    )(proj, proj, proj, cache_k, cache_v, bias)


SEQ_BLOCK = 256
N_SEQ_BLOCKS = T_ALL // SEQ_BLOCK
N_SEQS = BATCH + DEC_BATCH


def _walk_block(d, i):
    return jnp.where(d == 0, i, N_SEQ_BLOCKS - 1 - i)


def _block_seq(j):
    prompt_blocks = T_PROMPT // SEQ_BLOCK
    per_prompt, per_sample = SEQ // SEQ_BLOCK, DEC_SEQ // SEQ_BLOCK
    js = j - prompt_blocks
    in_prompt = j < prompt_blocks
    return (jnp.where(in_prompt, j // per_prompt, BATCH + js // per_sample),
            jnp.where(in_prompt, j % per_prompt, js % per_sample),
            jnp.where(in_prompt, per_prompt, per_sample))


def _walk_flags(d, i):
    _, pos, n = _block_seq(_walk_block(d, i))
    return pos == jnp.where(d == 0, 0, n - 1), pos == jnp.where(d == 0, n - 1, 0)


S5_TC = SEQ_BLOCK
S5_PITCH = S5_TC + 8
S5_SLABS = SSM_GROUPS * SSM_STATE // 128
S5_QB = 4
S5_SLABS_Q = S5_SLABS // S5_QB


def _s5_disc_kernel(lr_ref, li_ref, dt_ref, br_ref, bi_ref, ar_ref, ai_ref, bbr_ref, bbi_ref):
    lr, li, dt = lr_ref[...], li_ref[...], dt_ref[...]
    mag = jnp.exp(lr * dt)
    ab_re = mag * jnp.cos(li * dt)
    ab_im = mag * jnp.sin(li * dt)
    den = lr * lr + li * li
    nr, ni = ab_re - 1.0, ab_im
    co_re = (nr * lr + ni * li) / den
    co_im = (ni * lr - nr * li) / den
    br, bi = br_ref[...], bi_ref[...]
    ar_ref[...] = ab_re
    ai_ref[...] = ab_im
    bbr_ref[...] = co_re * br - co_im * bi
    bbi_ref[...] = co_re * bi + co_im * br


def _s5_params(lam_re, lam_im, log_dt, b_re, b_im, c_re, c_im):
    g, p, c = SSM_GROUPS, SSM_STATE, SSM_GROUP
    shape = (2, g, c, p)
    n = 2 * g * c
    bc = lambda a: jnp.broadcast_to(a, shape).reshape(n, p)
    dt = jnp.exp(log_dt)
    args = (bc(lam_re[:, :, None, :]), bc(lam_im[:, :, None, :]), bc(dt[:, :, None, None]),
            jnp.transpose(b_re, (0, 1, 3, 2)).reshape(n, p), jnp.transpose(b_im, (0, 1, 3, 2)).reshape(n, p))
    spec = pl.BlockSpec((n, p), lambda: (0, 0))
    out = jax.ShapeDtypeStruct((n, p), F32)
    ab_re, ab_im, bb_re, bb_im = pl.pallas_call(
        _s5_disc_kernel, in_specs=[spec] * 5, out_specs=[spec] * 4, out_shape=[out] * 4, name="s5_disc")(*args)
    a = jnp.stack([ab_re.reshape(shape)[:, :, 0, :], ab_im.reshape(shape)[:, :, 0, :]], axis=1)
    a = a.reshape(2, 2, S5_SLABS, 128)
    eye = jnp.eye(16, dtype=F32)

    def blockdiag_in(bb):
        return jnp.einsum('dqgcp,gh->dqgchp', bb.reshape(2, S5_QB, 16, c, p), eye).reshape(2, S5_QB, 16 * c, 16 * p)

    def blockdiag_out(cc):
        return jnp.einsum('dqgcp,gh->dqgphc', cc.reshape(2, S5_QB, 16, c, p), eye).reshape(2, S5_QB, 16 * p, 16 * c)

    bb = jnp.concatenate([blockdiag_in(bb_re.reshape(shape)), blockdiag_in(bb_im.reshape(shape))], axis=-1)
    cc = jnp.stack([blockdiag_out(c_re).reshape(2, S5_QB, S5_SLABS_Q, 128, 16 * c),
                    blockdiag_out(-c_im).reshape(2, S5_QB, S5_SLABS_Q, 128, 16 * c)], axis=3)
    cc = cc.reshape(2, S5_QB, S5_SLABS_Q, 256, 16 * c)
    return a, bb.astype(BF16), cc.astype(BF16)


def _s5_scan_kernel(u0, u1, u2, u3, bb_ref, cc_ref, a_ref, h0_ref, y_ref, hout_ref,
                    bur, bui, xsr, xsi, st):
    d = pl.program_id(0)
    starts, ends = _walk_flags(d, pl.program_id(1))

    @pl.when(starts)
    def _():
        st[0] = h0_ref[0, 0, 0]
        st[1] = h0_ref[0, 0, 1]

    half = S5_SLABS_Q * 128
    for q, u in enumerate((u0, u1, u2, u3)):
        buq = _dot(u[...].astype(BF16), bb_ref[0, q])
        for j in range(S5_SLABS_Q):
            row0 = (q * S5_SLABS_Q + j) * S5_PITCH
            bur[row0:row0 + S5_TC, :] = buq[:, 128 * j:128 * (j + 1)]
            bui[row0:row0 + S5_TC, :] = buq[:, half + 128 * j:half + 128 * (j + 1)]

    ar = a_ref[0, 0]
    ai = a_ref[0, 1]

    def step(i, carry):
        xr, xi = carry
        t = jnp.where(d == 0, i, S5_TC - 1 - i)
        rows = pl.ds(t, S5_SLABS, stride=S5_PITCH)
        nxr = ar * xr - ai * xi + bur[rows, :]
        nxi = ar * xi + ai * xr + bui[rows, :]
        xsr[rows, :] = nxr
        xsi[rows, :] = nxi
        return nxr, nxi

    xr, xi = lax.fori_loop(0, S5_TC, step, (st[0], st[1]), unroll=4)
    st[0] = xr
    st[1] = xi

    @pl.when(ends)
    def _():
        hout_ref[0, 0, 0] = xr
        hout_ref[0, 0, 1] = xi

    for q in range(S5_QB):
        acc = jnp.zeros((S5_TC, 16 * SSM_GROUP), F32)
        for j in range(S5_SLABS_Q):
            row0 = (q * S5_SLABS_Q + j) * S5_PITCH
            x_ri = jnp.concatenate([xsr[row0:row0 + S5_TC, :], xsi[row0:row0 + S5_TC, :]], axis=1)
            acc += _dot(x_ri.astype(BF16), cc_ref[0, q, j])
        y_ref[0, :, 256 * q:256 * (q + 1)] = acc


def _s5_scan(proj, a, bb, cc, h0):
    ucol = OFF_SSM // 256

    def u_spec(q):
        return pl.BlockSpec((S5_TC, 256), lambda d, i: (_walk_block(d, i), ucol + q))

    state = pl.BlockSpec((1, 1, 2, S5_SLABS, 128), lambda d, i: (_block_seq(_walk_block(d, i))[0], d, 0, 0, 0))
    slab = pltpu.VMEM((S5_SLABS * S5_PITCH, 128), F32)
    return pl.pallas_call(
        _s5_scan_kernel,
        grid=(2, N_SEQ_BLOCKS),
        in_specs=[u_spec(q) for q in range(S5_QB)] + [
            pl.BlockSpec((1, S5_QB, 256, 2 * S5_SLABS_Q * 128), lambda d, i: (d, 0, 0, 0)),
            pl.BlockSpec((1, S5_QB, S5_SLABS_Q, 256, 256), lambda d, i: (d, 0, 0, 0, 0)),
            pl.BlockSpec((1, 2, S5_SLABS, 128), lambda d, i: (d, 0, 0, 0)),
            state,
        ],
        out_specs=[pl.BlockSpec((1, S5_TC, SSM_WIDTH), lambda d, i: (d, _walk_block(d, i), 0)), state],
        out_shape=[jax.ShapeDtypeStruct((2, T_ALL, SSM_WIDTH), F32),
                   jax.ShapeDtypeStruct((N_SEQS, 2, 2, S5_SLABS, 128), F32)],
        scratch_shapes=[slab, slab, slab, slab, pltpu.VMEM((2, S5_SLABS, 128), F32)],
        compiler_params=_params(("parallel", "arbitrary")),
        name="s5_scan",
    )(proj, proj, proj, proj, bb, cc, a, h0)


def _s5_glu_kernel(ua_ref, ub_ref, y0_ref, y1_ref, d_ref, w_ref, o_ref):
    u = jnp.concatenate([ua_ref[...], ub_ref[...]], axis=-1)
    y = jax.nn.gelu(u * d_ref[0] + y0_ref[0] + y1_ref[0])
    o_ref[...] = (y * jax.nn.sigmoid(_dot(y.astype(BF16), w_ref[0].astype(BF16)))).astype(BF16)


S5_GLU_ROWS = 512


def _s5_glu(proj, row0, y, ssm_d, w_glu, layer):
    rows = y.shape[1]
    tr = S5_GLU_ROWS
    blk0 = row0 // tr
    ucol = OFF_SSM // 512
    return pl.pallas_call(
        _s5_glu_kernel,
        grid=(rows // tr,),
        in_specs=[
            pl.BlockSpec((tr, 512), lambda i: (blk0 + i, ucol)),
            pl.BlockSpec((tr, 512), lambda i: (blk0 + i, ucol + 1)),
            pl.BlockSpec((1, tr, SSM_WIDTH), lambda i: (0, i, 0)),
            pl.BlockSpec((1, tr, SSM_WIDTH), lambda i: (1, i, 0)),
            pl.BlockSpec((1, 1, SSM_WIDTH), lambda i: (layer, 0, 0)),
            pl.BlockSpec((1, SSM_WIDTH, SSM_WIDTH), lambda i: (layer, 0, 0)),
        ],
        out_specs=pl.BlockSpec((tr, SSM_WIDTH), lambda i: (i, 0)),
        out_shape=jax.ShapeDtypeStruct((rows, SSM_WIDTH), BF16),
        compiler_params=_params(("parallel",)),
        name="s5_glu",
    )(proj, proj, y, y, ssm_d.reshape(DEPTH, 1, SSM_WIDTH), w_glu)


RW_ROWS = 256
RW_C = 64
RW_PAIRS = RWKV_HEADS // 2
RW_LOWRANK0 = 3 * RWKV_WIDTH
RW_STEP_CHUNKS = 2
RW_GROUP = 16


def _head_indicator():
    ind = (np.arange(RWKV_WIDTH)[:, None] // RWKV_HEAD == np.arange(128)[None, :]).astype(np.float32)
    return jnp.asarray(ind, BF16), jnp.asarray(ind.T, BF16)


def _head_sum(x, ind, ind_t):
    return _dot_exact_lhs(_dot_exact_lhs(x, ind), ind_t)


def _rwkv_prep_kernel(x_ref, prev_ref, next_ref, mu_ref, kk_ref, ka_ref, rk_ref, w0_ref, a0_ref, w2_ref, a2_ref,
                      g2_ref, ind_ref, indt_ref,
                      r_out, kkn_out, v_out, g_out, bonus_out, lw_out, kka_out, kd_out, *, seq_blocks):
    i = pl.program_id(0)
    n_prompt_blocks = T_PROMPT // RW_ROWS
    j = i - n_prompt_blocks
    in_seq = jnp.where(i < n_prompt_blocks, i % seq_blocks[0], j % seq_blocks[1])
    n_in_seq = jnp.where(i < n_prompt_blocks, seq_blocks[0], seq_blocks[1])
    first = in_seq == 0
    last = in_seq == n_in_seq - 1
    x = x_ref[...]
    row = lax.broadcasted_iota(jnp.int32, x.shape, 0)
    prev_row = jnp.where(first, 0.0, prev_ref[7:8, :])
    next_row = jnp.where(last, 0.0, next_ref[0:1, :])
    xp = jnp.where(row == 0, prev_row, pltpu.roll(x, 1, 0))
    xn = jnp.where(row == RW_ROWS - 1, next_row, pltpu.roll(x, RW_ROWS - 1, 0))
    z = x + mu_ref[...] * (0.5 * (xp + xn) - x)

    w = RWKV_WIDTH
    r, k, v = z[:, 0:w], z[:, w:2 * w], z[:, 2 * w:3 * w]
    low = z[:, RW_LOWRANK0:RW_LOWRANK0 + 128]
    gl = z[:, RW_LOWRANK0 + 128:RW_LOWRANK0 + 256]
    ind, ind_t = ind_ref[...], indt_ref[...]

    kk = k * kk_ref[...]
    kk = kk * lax.rsqrt(_head_sum(kk * kk, ind, ind_t) + 1e-12)
    r_out[...] = r
    kkn_out[...] = kk
    v_out[...] = v
    g_out[...] = _dot(jax.nn.sigmoid(gl).astype(BF16), g2_ref[...].astype(BF16))
    tanh_low = jnp.tanh(low).astype(BF16)
    low_b = low.astype(BF16)
    bonus = jnp.zeros_like(r)
    for d in range(2):
        w_log = -jax.nn.softplus(-(w0_ref[d:d + 1, :] + _dot(tanh_low, w2_ref[d].astype(BF16)))) - 0.5
        a = jax.nn.sigmoid(a0_ref[d:d + 1, :] + _dot(low_b, a2_ref[d].astype(BF16)))
        kd = k * (1.0 + (a - 1.0) * ka_ref[...])
        lw_out[d] = -jnp.exp(w_log)
        kka_out[d] = kk * a
        kd_out[d] = kd
        bonus = bonus + _head_sum(r * kd * rk_ref[...], ind, ind_t) * v
    bonus_out[...] = bonus


def _rwkv_prep(proj_r, lp):
    t = proj_r.shape[0]
    nb = t // RW_ROWS
    halo = RW_ROWS // 8
    w = RWKV_WIDTH
    zpad = jnp.zeros((2, 64, w), F32)
    w2 = jnp.concatenate([lp['rwkv_w2'], zpad], axis=1)
    a2 = jnp.concatenate([zpad, lp['rwkv_a2']], axis=1)
    ind, ind_t = _head_indicator()
    row = lambda a: a.reshape(1, -1)
    full = lambda shape: pl.BlockSpec(shape, lambda i: (0,) * len(shape))
    tok = pl.BlockSpec((RW_ROWS, w), lambda i: (i, 0))
    tok2 = pl.BlockSpec((2, RW_ROWS, w), lambda i: (0, i, 0))
    f1 = jax.ShapeDtypeStruct((t, w), F32)
    f2 = jax.ShapeDtypeStruct((2, t, w), F32)
    return pl.pallas_call(
        functools.partial(_rwkv_prep_kernel, seq_blocks=(SEQ // RW_ROWS, DEC_SEQ // RW_ROWS)),
        grid=(nb,),
        in_specs=[
            pl.BlockSpec((RW_ROWS, RWKV_IN_W), lambda i: (i, 0)),
            pl.BlockSpec((8, RWKV_IN_W), lambda i: (jnp.maximum(i * halo - 1, 0), 0)),
            pl.BlockSpec((8, RWKV_IN_W), lambda i: (jnp.minimum((i + 1) * halo, nb * halo - 1), 0)),
            full((1, RWKV_IN_W)), full((1, w)), full((1, w)), full((1, w)), full((2, w)), full((2, w)),
            full((2, 128, w)), full((2, 128, w)), full((GATE_RANK, w)), full((w, 128)), full((128, w)),
        ],
        out_specs=[tok, tok, tok, tok, tok, tok2, tok2, tok2],
        out_shape=[f1, f1, f1, f1, f1, f2, f2, f2],
        compiler_params=_params(("parallel",)),
        name="rwkv_prep",
    )(proj_r, proj_r, proj_r, row(lp['rwkv_mu']), row(lp['rwkv_k_k']), row(lp['rwkv_k_a']), row(lp['rwkv_r_k']),
      lp['rwkv_w0'], lp['rwkv_a0'], w2, a2, lp['rwkv_g2'], ind, ind_t)


def _rwkv_chunk_kernel(r_ref, kk_ref, v_ref, lw_ref, kka_ref, kd_ref, r2_out, y0_out, mneg_out, sadd_out, gc_out):
    d = pl.program_id(0)
    c = RW_C
    ri = lax.broadcasted_iota(jnp.int32, (2 * c, 2 * c), 0)
    ci = lax.broadcasted_iota(jnp.int32, (2 * c, 2 * c), 1)
    lower_half_rows = ri >= c
    same_head = lower_half_rows == (ci >= c)
    ti, tj = ri % c, ci % c
    before = (tj - ti) * (1 - 2 * d) < 0
    strict = same_head & before
    incl = same_head & (before | (ti == tj))
    eye = (ri == ci).astype(F32)
    tri_c = (incl[0:c, 0:c]).astype(BF16)
    lane = lax.broadcasted_iota(jnp.int32, (1, 2 * c), 1)
    m_a = (lane < RWKV_HEAD).astype(F32)
    m_b = 1.0 - m_a
    bf = lambda a: a.astype(BF16)

    def group(items):
        each = lambda f, *xs: [f(*a) for a in zip(*xs)]
        sl = [slice(128 * p, 128 * (p + 1)) for _, p in items]
        rs = [slice(c * k, c * (k + 1)) for k, _ in items]
        r, kk, v = ([ref[q, s] for q, s in zip(rs, sl)] for ref in (r_ref, kk_ref, v_ref))
        lw, kka, kd = ([ref[0, q, s] for q, s in zip(rs, sl)] for ref in (lw_ref, kka_ref, kd_ref))
        cum = each(lambda x: _dot_exact_rhs(tri_c, x), lw)
        tot = each(lambda x: jnp.where(d == 0, x[c - 1:c, :], x[0:1, :]), cum)
        g_rem = each(lambda t, x: jnp.exp(t - x), tot, cum)
        g_inv = each(lambda x: jnp.exp(-x), cum)
        qk = each(lambda k_, x, l: k_ * jnp.exp(x - l), kk, cum, lw)
        rt = each(lambda r_, x: r_ * jnp.exp(x), r, cum)
        lhs_q = each(lambda q: jnp.concatenate([q * m_a, q * m_b], axis=0), qk)
        lhs_r = each(lambda q: jnp.concatenate([q * m_a, q * m_b], axis=0), rt)
        rhs = each(lambda k_, a_, g: bf(jnp.concatenate([k_ * g, a_ * g], axis=0)), kd, kka, g_inv)
        x = each(lambda l, rr: _dot_nt(bf(l), rr), lhs_q, rhs)
        z = each(lambda l, rr: _dot_nt(bf(l), rr), lhs_r, rhs)
        xr = each(lambda t: pltpu.roll(t, c, 1), x)
        zr = each(lambda t: pltpu.roll(t, c, 1), z)
        qk_bd = each(lambda t, tr: bf(jnp.where(strict, jnp.where(lower_half_rows, tr, t), 0.0)), x, xr)
        qa_bd = each(lambda t, tr: jnp.where(strict, jnp.where(lower_half_rows, t, tr), 0.0), x, xr)
        rk_bd = each(lambda t, tr: bf(jnp.where(incl, jnp.where(lower_half_rows, tr, t), 0.0)), z, zr)
        ra_bd = each(lambda t, tr: bf(jnp.where(incl, jnp.where(lower_half_rows, t, tr), 0.0)), z, zr)
        inv = each(lambda n_: eye - n_, qa_bd)
        power = qa_bd
        for _ in range(5):
            pb = each(bf, power)
            power = each(_dot, pb, pb)
            inv = each(lambda i_, p_: i_ + _dot(bf(i_), bf(p_)), inv, power)
        v_st = each(lambda t: bf(jnp.concatenate([t * m_a, t * m_b], axis=0)), v)
        av_st = each(_dot, qk_bd, v_st)
        sol = each(lambda i_, q, av: _dot(bf(i_), jnp.concatenate([bf(q), bf(av)], axis=1)), inv, lhs_q, av_st)
        ra_sol = each(lambda a_, s_: _dot(a_, bf(s_)), ra_bd, sol)
        rk_v = each(_dot, rk_bd, v_st)
        fold = lambda t: t[0:c] + t[c:2 * c]
        for i, (k, p) in enumerate(items):
            w_f = bf(fold(sol[i][:, 0:128]))
            u0_f = bf(fold(sol[i][:, 128:256]))
            ah_b = bf(kka[i] * g_rem[i])
            r2_out[0, rs[i], sl[i]] = bf(fold(lhs_r[i] - ra_sol[i][:, 0:128]))
            y0_out[0, rs[i], sl[i]] = fold(rk_v[i] - ra_sol[i][:, 128:256])
            mneg_out[0, k, p] = bf(jnp.where(same_head, _dot_tn(w_f, ah_b), 0.0))
            sadd_out[0, k, p] = jnp.where(
                same_head, _dot_tn(bf(v[i]), bf(kd[i] * g_rem[i])) - _dot_tn(u0_f, ah_b), 0.0)
            gc_out[0, k, :, sl[i]] = jnp.exp(tot[i])

    items = [(k, p) for k in range(RW_STEP_CHUNKS) for p in range(RW_PAIRS)]
    for i0 in range(0, len(items), RW_GROUP):
        group(items[i0:i0 + RW_GROUP])


def _rwkv_chunks(r, kk, v, lw, kka, kd):
    t = r.shape[0]
    nck = t // RW_C
    w = RWKV_WIDTH
    rows = RW_STEP_CHUNKS * RW_C
    tok = pl.BlockSpec((rows, w), lambda d, i: (i, 0))
    tok2 = pl.BlockSpec((1, rows, w), lambda d, i: (d, i, 0))
    mat = pl.BlockSpec((1, RW_STEP_CHUNKS, RW_PAIRS, 128, 128), lambda d, i: (d, i, 0, 0, 0))
    mat_shape = lambda dtype: jax.ShapeDtypeStruct((2, nck, RW_PAIRS, 128, 128), dtype)
    return pl.pallas_call(
        _rwkv_chunk_kernel,
        grid=(2, nck // RW_STEP_CHUNKS),
        in_specs=[tok, tok, tok, tok2, tok2, tok2],
        out_specs=[tok2, tok2, mat, mat, pl.BlockSpec((1, RW_STEP_CHUNKS, 1, w), lambda d, i: (d, i, 0, 0))],
        out_shape=[jax.ShapeDtypeStruct((2, t, w), BF16), jax.ShapeDtypeStruct((2, t, w), F32),
                   mat_shape(BF16), mat_shape(F32), jax.ShapeDtypeStruct((2, nck, 1, w), F32)],
        compiler_params=_params(("parallel", "parallel")),
        name="rwkv_chunks",
    )(r, kk, v, lw, kka, kd)


RW_BLOCK_CHUNKS = SEQ_BLOCK // RW_C


def _rwkv_state_kernel(r2_ref, y0_ref, mneg_ref, sadd_ref, gc_ref, s0_ref, y_out, s_out, st):
    d = pl.program_id(0)
    starts, ends = _walk_flags(d, pl.program_id(1))

    @pl.when(starts)
    def _():
        st[...] = s0_ref[0, 0]

    def chunk(i, carry):
        ck = jnp.where(d == 0, i, RW_BLOCK_CHUNKS - 1 - i)
        rows = pl.ds(pl.multiple_of(ck * RW_C, RW_C), RW_C)
        for p in range(RW_PAIRS):
            ls = slice(128 * p, 128 * (p + 1))
            s = st[p]
            sb = s.astype(BF16)
            y_out[0, rows, ls] = _dot_nt(r2_ref[0, rows, ls], sb) + y0_ref[0, rows, ls]
            st[p] = s * gc_ref[0, ck, :, ls] - _dot(sb, mneg_ref[0, ck, p]) + sadd_ref[0, ck, p]
        return carry

    lax.fori_loop(0, RW_BLOCK_CHUNKS, chunk, 0, unroll=True)

    @pl.when(ends)
    def _():
        s_out[0, 0] = st[...]


def _rwkv_state(r2, y0, mneg, sadd, gc, s0):
    w = RWKV_WIDTH
    tok = pl.BlockSpec((1, SEQ_BLOCK, w), lambda d, i: (d, _walk_block(d, i), 0))
    mat = pl.BlockSpec((1, RW_BLOCK_CHUNKS, RW_PAIRS, 128, 128), lambda d, i: (d, _walk_block(d, i), 0, 0, 0))
    state = pl.BlockSpec((1, 1, RW_PAIRS, 128, 128), lambda d, i: (_block_seq(_walk_block(d, i))[0], d, 0, 0, 0))
    return pl.pallas_call(
        _rwkv_state_kernel,
        grid=(2, N_SEQ_BLOCKS),
        in_specs=[tok, tok, mat, mat,
                  pl.BlockSpec((1, RW_BLOCK_CHUNKS, 1, w), lambda d, i: (d, _walk_block(d, i), 0, 0)), state],
        out_specs=[tok, state],
        out_shape=[jax.ShapeDtypeStruct((2, T_ALL, w), F32),
                   jax.ShapeDtypeStruct((N_SEQS, 2, RW_PAIRS, 128, 128), F32)],
        scratch_shapes=[pltpu.VMEM((RW_PAIRS, 128, 128), F32)],
        compiler_params=_params(("parallel", "arbitrary")),
        name="rwkv_state",
    )(r2, y0, mneg, sadd, gc, s0)


def _pack_state(s):
    n = s.shape[0]
    sp = s.reshape(n, 2, RW_PAIRS, 2, RWKV_HEAD, RWKV_HEAD)
    z = jnp.zeros_like(sp[:, :, :, 0])
    top = jnp.concatenate([sp[:, :, :, 0], z], axis=-1)
    bot = jnp.concatenate([z, sp[:, :, :, 1]], axis=-1)
    return jnp.concatenate([top, bot], axis=-2)


def _unpack_state(sp):
    n = sp.shape[0]
    h = RWKV_HEAD
    return jnp.stack([sp[..., 0:h, 0:h], sp[..., h:, h:]], axis=3).reshape(n, 2, RWKV_HEADS, h, h)


def _rwkv_post_kernel(y_ref, bonus_ref, g_ref, lng_ref, lnb_ref, ind_ref, indt_ref, o_ref):
    ind, ind_t = ind_ref[...], indt_ref[...]
    y = y_ref[0] + y_ref[1]
    mu = _head_sum(y, ind, ind_t) * (1.0 / RWKV_HEAD)
    yc = y - mu
    var = _head_sum(yc * yc, ind, ind_t) * (1.0 / RWKV_HEAD)
    yn = yc * lax.rsqrt(var + RWKV_GN_EPS)
    o_ref[...] = ((yn * lng_ref[...] + lnb_ref[...] + bonus_ref[...]) * g_ref[...]).astype(BF16)


def _rwkv_post(y, bonus, g, ln_g, ln_b):
    t = bonus.shape[0]
    w = RWKV_WIDTH
    ind, ind_t = _head_indicator()
    tok = pl.BlockSpec((RW_ROWS, w), lambda i: (i, 0))
    full = lambda shape: pl.BlockSpec(shape, lambda i: (0,) * len(shape))
    return pl.pallas_call(
        _rwkv_post_kernel,
        grid=(t // RW_ROWS,),
        in_specs=[pl.BlockSpec((2, RW_ROWS, w), lambda i: (0, i, 0)), tok, tok, full((1, w)), full((1, w)),
                  full((w, 128)), full((128, w))],
        out_specs=tok,
        out_shape=jax.ShapeDtypeStruct((t, w), BF16),
        compiler_params=_params(("parallel",)),
        name="rwkv_post",
    )(y, bonus, g, ln_g.reshape(1, w), ln_b.reshape(1, w), ind, ind_t)


def _rwkv_mixer(proj_r, lp, s0_sample):
    r, kk, v, g, bonus, lw, kka, kd = _rwkv_prep(proj_r, lp)
    r2, y0, mneg, sadd, gc = _rwkv_chunks(r, kk, v, lw, kka, kd)
    s0 = jnp.concatenate([jnp.zeros((BATCH, 2, RW_PAIRS, 128, 128), F32), _pack_state(s0_sample)], axis=0)
    y, s_fin = _rwkv_state(r2, y0, mneg, sadd, gc, s0)
    return _rwkv_post(y, bonus, g, lp['rwkv_ln_g'], lp['rwkv_ln_b']), _unpack_state(s_fin[:BATCH])


MOE_ROWS = 256
MOE_BLOCKS = T_ALL * EXPERT_TOPK // MOE_ROWS + N_EXPERTS
MOE_PAD_ROWS = MOE_BLOCKS * MOE_ROWS
MOE_FF_TILE = 512
MOE_OUT_TILE = 2048


def _top2_sum(vals):
    best = None
    for a in range(len(vals)):
        for b in range(a + 1, len(vals)):
            s = vals[a] + vals[b]
            best = s if best is None else jnp.maximum(best, s)
    return best


def _first_argmax(vals):
    idx = jnp.zeros(vals[0].shape, jnp.int32)
    best = vals[0]
    for j in range(1, len(vals)):
        upd = vals[j] > best
        idx = jnp.where(upd, j, idx)
        best = jnp.where(upd, vals[j], best)
    return idx, best


def _pick(idx, vals):
    out = vals[-1]
    for j in range(len(vals) - 2, -1, -1):
        out = jnp.where(idx == j, vals[j], out)
    return out


def _ffn_norm_route_kernel(x_ref, g_ref, mod_ref, wh_ref, wl_ref, bias_ref, h_ref, idx_ref, wts_ref):
    y = _rms(x_ref[...], g_ref[...])
    h = y * (1.0 + mod_ref[0, 4:5, :]) + mod_ref[0, 3:4, :]
    hh = h.astype(BF16)
    hl = (h - hh.astype(F32)).astype(BF16)
    bits = lax.bitcast_convert_type(hh.astype(F32), jnp.uint32)
    h_ref[...] = (bits[:, :D_MODEL // 2] >> 16) | (bits[:, D_MODEL // 2:] & jnp.uint32(0xFFFF0000))
    wh, wl = wh_ref[...], wl_ref[...]
    logits = _dot_nt(wh, hh) + _dot_nt(wh, hl) + _dot_nt(wl, hh)
    scores = jax.nn.sigmoid(logits)
    sel = scores + bias_ref[...]
    gs = EXPERTS_PER_GROUP
    sel_rows = [sel[e:e + 1, :] for e in range(N_EXPERTS)]
    sc_rows = [scores[e:e + 1, :] for e in range(N_EXPERTS)]
    grp, _ = _first_argmax([_top2_sum(sel_rows[gs * g:gs * (g + 1)]) for g in range(N_EXPERT_GROUPS)])
    v = [_pick(grp, [sel_rows[gs * g + j] for g in range(N_EXPERT_GROUPS)]) for j in range(gs)]
    s = [_pick(grp, [sc_rows[gs * g + j] for g in range(N_EXPERT_GROUPS)]) for j in range(gs)]
    i1, _ = _first_argmax(v)
    i2, _ = _first_argmax([jnp.where(i1 == j, -jnp.inf, v[j]) for j in range(gs)])
    w1, w2 = _pick(i1, s), _pick(i2, s)
    tot = w1 + w2
    idx_ref[0:1, :] = grp * gs + i1
    idx_ref[1:2, :] = grp * gs + i2
    wts_ref[0:1, :] = w1 / tot
    wts_ref[1:2, :] = w2 / tot


def _ffn_norm_route(x, g, mod, w_router, router_bias):
    t = x.shape[0]
    wt = w_router.T
    wh = wt.astype(BF16)
    wl = (wt - wh.astype(F32)).astype(BF16)
    full = lambda shape: pl.BlockSpec(shape, lambda i: (0,) * len(shape))
    return pl.pallas_call(
        _ffn_norm_route_kernel,
        grid=(t // NORM_ROWS,),
        in_specs=[
            pl.BlockSpec((NORM_ROWS, D_MODEL), lambda i: (i, 0)),
            full((1, D_MODEL)),
            pl.BlockSpec((1, 6, D_MODEL), lambda i: (_cond_group(i, NORM_ROWS), 0, 0)),
            full((N_EXPERTS, D_MODEL)), full((N_EXPERTS, D_MODEL)), full((N_EXPERTS, 1)),
        ],
        out_specs=[pl.BlockSpec((NORM_ROWS, D_MODEL // 2), lambda i: (i, 0)),
                   pl.BlockSpec((EXPERT_TOPK, NORM_ROWS), lambda i: (0, i)),
                   pl.BlockSpec((EXPERT_TOPK, NORM_ROWS), lambda i: (0, i))],
        out_shape=[jax.ShapeDtypeStruct((t, D_MODEL // 2), jnp.uint32),
                   jax.ShapeDtypeStruct((EXPERT_TOPK, t), jnp.int32),
                   jax.ShapeDtypeStruct((EXPERT_TOPK, t), F32)],
        compiler_params=_params(("parallel",)),
        name="ffn_norm_route",
    )(x, g.reshape(1, D_MODEL), mod, wh, wl, router_bias.reshape(N_EXPERTS, 1))


def _dispatch_plan(idx):
    t = idx.shape[1]
    flat_e = idx.T.reshape(-1)
    onehot = (flat_e[:, None] == jnp.arange(N_EXPERTS, dtype=jnp.int32)[None, :]).astype(jnp.int32)
    csum = jnp.cumsum(onehot, axis=0)
    rank = jnp.take_along_axis(csum, flat_e[:, None], axis=1)[:, 0] - 1
    counts = csum[-1]
    padded = ((counts + MOE_ROWS - 1) // MOE_ROWS) * MOE_ROWS
    pad_end = jnp.cumsum(padded)
    pad_start = pad_end - padded
    dest = (pad_start[flat_e] + rank).astype(jnp.int32)
    tok = jnp.arange(t * EXPERT_TOPK, dtype=jnp.int32) // EXPERT_TOPK
    row_tok = jnp.zeros((MOE_PAD_ROWS,), jnp.int32).at[dest].set(tok)
    block_row0 = jnp.arange(MOE_BLOCKS, dtype=jnp.int32) * MOE_ROWS
    block_e = jnp.minimum(jnp.sum((pad_end[None, :] <= block_row0[:, None]).astype(jnp.int32), axis=1),
                          N_EXPERTS - 1).astype(jnp.int32)
    n_used = (pad_end[-1] // MOE_ROWS).astype(jnp.int32).reshape(1)
    return row_tok, block_e, n_used, dest


DMA_UNROLL = 8


def _row_copy(src_hbm, row, dst, slot, sem):
    return pltpu.make_async_copy(src_hbm.at[pl.ds(row, 1), :], dst.at[pl.ds(slot, 1), :], sem)


def _start_rows(src_hbm, idx_ref, idx0, stride, n, dst, sem):
    def body(r, c):
        _row_copy(src_hbm, idx_ref[idx0 + stride * r], dst, r, sem).start()
        return c
    lax.fori_loop(0, n, body, 0, unroll=DMA_UNROLL)


def _wait_rows(src_hbm, n, dst, sem):
    def body(r, c):
        _row_copy(src_hbm, 0, dst, r, sem).wait()
        return c
    lax.fori_loop(0, n, body, 0, unroll=DMA_UNROLL)


def _gather_rows_kernel(tok_ref, nu_ref, h_hbm, o_ref, buf, sem):
    i = pl.program_id(0)
    n = nu_ref[0]

    @pl.when((i == 0) & (n > 0))
    def _():
        _start_rows(h_hbm, tok_ref, 0, 1, MOE_ROWS, buf.at[0], sem.at[0])

    @pl.when(i + 1 < n)
    def _():
        nxt = (i + 1) % 2
        _start_rows(h_hbm, tok_ref, (i + 1) * MOE_ROWS, 1, MOE_ROWS, buf.at[nxt], sem.at[nxt])

    @pl.when(i < n)
    def _():
        cur = i % 2
        _wait_rows(h_hbm, MOE_ROWS, buf.at[cur], sem.at[cur])
        u = buf[cur]
        lo = lax.bitcast_convert_type(u << 16, F32).astype(BF16)
        hi = lax.bitcast_convert_type(u & jnp.uint32(0xFFFF0000), F32).astype(BF16)
        o_ref[...] = jnp.concatenate([lo, hi], axis=1)

    @pl.when(i >= n)
    def _():
        o_ref[...] = jnp.zeros_like(o_ref)


def _gather_rows(h, row_tok, n_used):
    return pl.pallas_call(
        _gather_rows_kernel,
        grid_spec=pltpu.PrefetchScalarGridSpec(
            num_scalar_prefetch=2,
            grid=(MOE_BLOCKS,),
            in_specs=[pl.BlockSpec(memory_space=pl.ANY)],
            out_specs=pl.BlockSpec((MOE_ROWS, D_MODEL), lambda i, tok, nu: (i, 0)),
            scratch_shapes=[pltpu.VMEM((2, MOE_ROWS, D_MODEL // 2), jnp.uint32), pltpu.SemaphoreType.DMA((2,))],
        ),
        out_shape=jax.ShapeDtypeStruct((MOE_PAD_ROWS, D_MODEL), BF16),
        compiler_params=_params(("arbitrary",)),
        name="moe_gather",
    )(row_tok, n_used, h)


def _expert_up_kernel(be_ref, nu_ref, x_ref, w1_ref, w3_ref, o_ref):
    i = pl.program_id(1)

    @pl.when(i < nu_ref[0])
    def _():
        x = x_ref[...]
        a = _dot(x, w1_ref[0, 0].astype(BF16))
        b = _dot(x, w3_ref[0, 0].astype(BF16))
        o_ref[...] = (a * jax.nn.sigmoid(a) * b).astype(BF16)

    @pl.when(i >= nu_ref[0])
    def _():
        o_ref[...] = jnp.zeros_like(o_ref)


def _expert_up(xb, w_gate, w_up, layer, block_e, n_used):
    wspec = pl.BlockSpec((1, 1, D_MODEL, MOE_FF_TILE), lambda j, i, be, nu: (layer, be[i], 0, j))
    return pl.pallas_call(
        _expert_up_kernel,
        grid_spec=pltpu.PrefetchScalarGridSpec(
            num_scalar_prefetch=2,
            grid=(EXPERT_FF // MOE_FF_TILE, MOE_BLOCKS),
            in_specs=[pl.BlockSpec((MOE_ROWS, D_MODEL), lambda j, i, be, nu: (i, 0)), wspec, wspec],
            out_specs=pl.BlockSpec((MOE_ROWS, MOE_FF_TILE), lambda j, i, be, nu: (i, j)),
        ),
        out_shape=jax.ShapeDtypeStruct((MOE_PAD_ROWS, EXPERT_FF), BF16),
        compiler_params=_params(("parallel", "arbitrary")),
        name="moe_up",
    )(block_e, n_used, xb, w_gate, w_up)


def _expert_down_kernel(be_ref, nu_ref, h_ref, w2_ref, o_ref):
    i = pl.program_id(1)

    @pl.when(i < nu_ref[0])
    def _():
        o_ref[...] = _dot(h_ref[...], w2_ref[0, 0].astype(BF16))

    @pl.when(i >= nu_ref[0])
    def _():
        o_ref[...] = jnp.zeros_like(o_ref)


def _expert_down(hmid, w_down, layer, block_e, n_used):
    return pl.pallas_call(
        _expert_down_kernel,
        grid_spec=pltpu.PrefetchScalarGridSpec(
            num_scalar_prefetch=2,
            grid=(D_MODEL // MOE_OUT_TILE, MOE_BLOCKS),
            in_specs=[
                pl.BlockSpec((MOE_ROWS, EXPERT_FF), lambda j, i, be, nu: (i, 0)),
                pl.BlockSpec((1, 1, EXPERT_FF, MOE_OUT_TILE), lambda j, i, be, nu: (layer, be[i], 0, j)),
            ],
            out_specs=pl.BlockSpec((MOE_ROWS, MOE_OUT_TILE), lambda j, i, be, nu: (i, j)),
        ),
        out_shape=jax.ShapeDtypeStruct((MOE_PAD_ROWS, D_MODEL), F32),
        compiler_params=_params(("parallel", "arbitrary")),
        name="moe_down",
    )(block_e, n_used, hmid, w_down)


COMBINE_ROWS = 256


def _combine_kernel(pos_ref, yb_hbm, x_ref, w_ref, mod_ref, o_ref, buf, sem):
    i = pl.program_id(0)
    n = pl.num_programs(0)
    per_block = COMBINE_ROWS * EXPERT_TOPK

    def start(blk, slot):
        for k in range(EXPERT_TOPK):
            _start_rows(yb_hbm, pos_ref, blk * per_block + k, EXPERT_TOPK, COMBINE_ROWS, buf.at[slot, k],
                        sem.at[slot])

    @pl.when(i == 0)
    def _():
        start(0, 0)

    @pl.when(i + 1 < n)
    def _():
        start(i + 1, (i + 1) % 2)

    cur = i % 2
    for k in range(EXPERT_TOPK):
        _wait_rows(yb_hbm, COMBINE_ROWS, buf.at[cur, k], sem.at[cur])
    y = buf[cur, 0] * w_ref[:, 0:1] + buf[cur, 1] * w_ref[:, 1:2]
    o_ref[...] = x_ref[...] + mod_ref[0, 5:6, :] * y


def _combine(yb, dest, wts, x, mod):
    t = x.shape[0]
    return pl.pallas_call(
        _combine_kernel,
        grid_spec=pltpu.PrefetchScalarGridSpec(
            num_scalar_prefetch=1,
            grid=(t // COMBINE_ROWS,),
            in_specs=[
                pl.BlockSpec(memory_space=pl.ANY),
                pl.BlockSpec((COMBINE_ROWS, D_MODEL), lambda i, pos: (i, 0)),
                pl.BlockSpec((COMBINE_ROWS, EXPERT_TOPK), lambda i, pos: (i, 0)),
                pl.BlockSpec((1, 6, D_MODEL), lambda i, pos: (_cond_group(i, COMBINE_ROWS), 0, 0)),
            ],
            out_specs=pl.BlockSpec((COMBINE_ROWS, D_MODEL), lambda i, pos: (i, 0)),
            scratch_shapes=[pltpu.VMEM((2, EXPERT_TOPK, COMBINE_ROWS, D_MODEL), F32),
                            pltpu.SemaphoreType.DMA((2,))],
        ),
        out_shape=jax.ShapeDtypeStruct((t, D_MODEL), F32),
        compiler_params=_params(("arbitrary",)),
        name="moe_combine",
    )(dest, yb, x, wts.T, mod)


def _moe_residual(x, g, mod, lp_router, w_gate, w_up, w_down, layer):
    h, idx, wts = _ffn_norm_route(x, g, mod, *lp_router)
    row_tok, block_e, n_used, dest = _dispatch_plan(idx)
    xb = _gather_rows(h, row_tok, n_used)
    hmid = _expert_up(xb, w_gate, w_up, layer, block_e, n_used)
    yb = _expert_down(hmid, w_down, layer, block_e, n_used)
    return _combine(yb, dest, wts, x, mod)


def _layer_params(args, layer):
    return {k: v[layer] for k, v in args.items()}


def kernel(x_prompt, x_sample, cache_na_k, cache_na_v, cache_gqa_k, cache_gqa_v, state_ssm, state_rwkv, c, c_ctx,
           w_ada, b_ada, norm_mix, norm_ffn, w_in, na_rpb, gqa_q_norm, gqa_k_norm, ssm_lam_re, ssm_lam_im,
           ssm_log_dt, ssm_b_re, ssm_b_im, ssm_c_re, ssm_c_im, ssm_d, ssm_w_glu, rwkv_mu, rwkv_w0, rwkv_w2, rwkv_a0,
           rwkv_a2, rwkv_k_k, rwkv_k_a, rwkv_r_k, rwkv_g2, rwkv_ln_g, rwkv_ln_b, w_branch, w_out, w_router,
           router_bias, w_exp_gate, w_exp_up, w_exp_down, norm_final):
    x = jnp.concatenate([x_prompt.reshape(T_PROMPT, D_MODEL), x_sample.reshape(T_SAMPLE, D_MODEL)], axis=0)
    cond = jnp.concatenate([c_ctx[None, :], c, jnp.zeros((COND_PAD - N_COND, D_MODEL), F32)], axis=0)
    mod_all = _ada_all(cond, w_ada, b_ada).reshape(DEPTH, COND_PAD, 6, D_MODEL)
    cos, sin = _rope_tables()
    rwkv_args = dict(rwkv_mu=rwkv_mu, rwkv_w0=rwkv_w0, rwkv_w2=rwkv_w2, rwkv_a0=rwkv_a0, rwkv_a2=rwkv_a2,
                     rwkv_k_k=rwkv_k_k, rwkv_k_a=rwkv_k_a, rwkv_r_k=rwkv_r_k.reshape(DEPTH, RWKV_WIDTH),
                     rwkv_g2=rwkv_g2, rwkv_ln_g=rwkv_ln_g, rwkv_ln_b=rwkv_ln_b)
    caches = [[] for _ in range(6)]
    for l in range(DEPTH):
        mod = mod_all[l]
        h = _norm_mod(x, norm_mix[l], mod, 0)
        proj = _in_proj(h, w_in, l, 0, OFF_RWKV)
        proj_r = _in_proj(h, w_in, l, OFF_RWKV, RWKV_IN_W)
        proj_g = _in_proj(h, w_in, l, OFF_GATE, N_BRANCH * D_MODEL)

        y_na_p, nk, nv = _ctx_na(proj)
        y_gqa_p, gk, gv = _ctx_gqa(proj, gqa_q_norm[l], gqa_k_norm[l])
        y_na_s = _lat_na(proj, cache_na_k, cache_na_v, l, _na_band_bias(na_rpb[l]))
        y_gqa_s = _lat_gqa(proj, cache_gqa_k, cache_gqa_v, l, gqa_q_norm[l], gqa_k_norm[l], cos, sin)

        a, bb, cc = _s5_params(ssm_lam_re[l], ssm_lam_im[l], ssm_log_dt[l], ssm_b_re[l], ssm_b_im[l],
                               ssm_c_re[l], ssm_c_im[l])
        h0 = jnp.concatenate([jnp.zeros((BATCH, 2, 2, S5_SLABS, 128), F32),
                              state_ssm[:, l].reshape(DEC_BATCH, 2, 2, S5_SLABS, 128)], axis=0)
        y_scan, ssm_fin = _s5_scan(proj, a, bb, cc, h0)
        ssm_state = ssm_fin[:BATCH]
        y_ssm = _s5_glu(proj, 0, y_scan, ssm_d, ssm_w_glu, l)

        y_rwkv, rwkv_state = _rwkv_mixer(proj_r, _layer_params(rwkv_args, l), state_rwkv[:, l])

        merged = _branch_merge((y_na_p, y_na_s), (y_gqa_p, y_gqa_s), y_ssm, y_rwkv, proj_g, w_branch, l)
        x = _out_proj_residual(merged, w_out, l, x, mod, 2)
        x = _moe_residual(x, norm_ffn[l], mod, (w_router, router_bias), w_exp_gate, w_exp_up, w_exp_down, l)

        for lst, val in zip(caches, (nk, nv, gk, gv, ssm_state.reshape(BATCH, 2, 2, SSM_GROUPS, SSM_STATE),
                                     rwkv_state)):
            lst.append(val)

    y = _final_norm(x, norm_final)
    outs = [jnp.stack(lst, axis=1) for lst in caches]
    return (y[:T_PROMPT].reshape(BATCH, SEQ, D_MODEL), y[T_PROMPT:].reshape(DEC_BATCH, DEC_SEQ, D_MODEL), *outs)
```
